```python
import math
import jax, jax.numpy as jnp
from jax import lax
import numpy as np

D_MODEL = 1024
BATCH = 8
SEQ = 4096
DEPTH = 1

N_META = 16
MIX_WIDTH = D_MODEL
ATTN_WIDTH = MIX_WIDTH // 2
CONV_WIDTH = MIX_WIDTH - ATTN_WIDTH
N_ATTN_HEADS = 4
DIFF_HEAD_DIM = ATTN_WIDTH // (2 * N_ATTN_HEADS)
V_HEAD_DIM = 2 * DIFF_HEAD_DIM
CONV_KERNEL = 31
N_CONV_GROUPS = 8
CONV_GROUP_DIM = CONV_WIDTH // N_CONV_GROUPS
Q_BLOCK = 128
PEER_HEADS = 8
PEER_KEYS = 128
PEER_EXPERTS = PEER_KEYS * PEER_KEYS
PEER_QUERY_DIM = 256
PEER_HALF = PEER_QUERY_DIM // 2
PEER_TOPK = 16
PEER_CHUNK = 128
NORM_EPS = 1e-6
IN_COLS = 3 * ATTN_WIDTH + 2 * CONV_WIDTH

kernel_name = "hymba_diffattn_conformer_peer_block"


def lambda_init_for(layer):
    return 0.8 - 0.6 * math.exp(-0.3 * layer)


def alibi_slopes(n_heads):
    return jnp.asarray(np.array([2.0 ** (-8.0 * (i + 1) / n_heads) for i in range(n_heads)], dtype=np.float32))


def rms_norm(x, g):
    xf = x.astype(jnp.float32)
    y = xf * lax.rsqrt(jnp.mean(xf * xf, axis=-1, keepdims=True) + NORM_EPS)
    return (y * g.astype(jnp.float32)).astype(x.dtype)


def diff_attention(q, k, v, lam, lam_init, sub_g, slopes):
    B, Lp = q.shape[0], q.shape[1]
    n_blocks = Lp // Q_BLOCK
    scale = DIFF_HEAD_DIM ** -0.5
    k_pos = jnp.arange(Lp)

    def block(i):
        start = i * Q_BLOCK
        qb = lax.dynamic_slice_in_dim(q, start, Q_BLOCK, axis=1)
        s = jnp.einsum('bqhcd,bkhcd->bhcqk', qb, k).astype(jnp.float32) * scale
        dist = (start + jnp.arange(Q_BLOCK))[:, None] - k_pos[None, :]
        bias = jnp.where(dist[None] >= 0,
                         -slopes[:, None, None] * dist[None].astype(jnp.float32),
                         -jnp.inf)
        p = jax.nn.softmax(s + bias[None, :, None], axis=-1)
        a = p[:, :, 0] - lam * p[:, :, 1]
        return jnp.einsum('bhqk,bkhd->bqhd', a.astype(v.dtype), v)

    out = lax.map(block, jnp.arange(n_blocks))
    out = jnp.moveaxis(out, 0, 1).reshape(B, Lp, N_ATTN_HEADS, V_HEAD_DIM)
    out = rms_norm(out, sub_g) * (1.0 - lam_init)
    return out.reshape(B, Lp, ATTN_WIDTH)


def conformer_conv(a, g, conv_w, conv_b, gn_g, gn_b):
    B, Lp, C = a.shape
    u = a * jax.nn.sigmoid(g)
    y = lax.conv_general_dilated(u, conv_w[:, None, :], window_strides=(1,),
                                 padding=[(CONV_KERNEL - 1, 0)],
                                 dimension_numbers=('NWC', 'WIO', 'NWC'),
                                 feature_group_count=C) + conv_b
    yg = y.reshape(B, Lp, N_CONV_GROUPS, CONV_GROUP_DIM).astype(jnp.float32)
    mu = jnp.mean(yg, axis=-1, keepdims=True)
    var = jnp.mean(jnp.square(yg - mu), axis=-1, keepdims=True)
    yn = ((yg - mu) * lax.rsqrt(var + NORM_EPS)).reshape(B, Lp, C)
    yn = yn * gn_g.astype(jnp.float32) + gn_b.astype(jnp.float32)
    return jax.nn.silu(yn).astype(a.dtype)


def peer_ffn(xn, wq, subkeys, u_tab, v_tab):
    T = xn.shape[0]
    xc = xn.reshape(T // PEER_CHUNK, PEER_CHUNK, D_MODEL)

    def chunk(xt):
        q = (xt @ wq).reshape(PEER_CHUNK, PEER_HEADS, 2, PEER_HALF)
        s = jnp.einsum('thpk,hpnk->thpn', q, subkeys).astype(jnp.float32)
        s_top, i_top = lax.top_k(s, PEER_TOPK)
        cand = s_top[:, :, 0, :, None] + s_top[:, :, 1, None, :]
        cand = cand.reshape(PEER_CHUNK, PEER_HEADS, PEER_TOPK * PEER_TOPK)
        c_top, c_idx = lax.top_k(cand, PEER_TOPK)
        i1 = jnp.take_along_axis(i_top[:, :, 0, :], c_idx // PEER_TOPK, axis=-1)
        i2 = jnp.take_along_axis(i_top[:, :, 1, :], c_idx % PEER_TOPK, axis=-1)
        e = i1 * PEER_KEYS + i2
        gate = jax.nn.softmax(c_top, axis=-1)
        act = jax.nn.gelu(jnp.einsum('thkd,td->thk', u_tab[e], xt).astype(jnp.float32), approximate=False)
        w = (gate * act).astype(xt.dtype)
        return jnp.einsum('thk,thkd->td', w, v_tab[e])

    return lax.map(chunk, xc).reshape(T, D_MODEL)


def setup_inputs(seed: int = 0) -> dict:
    key = jax.random.key(seed)
    ks = jax.random.split(key, 24)
    f32 = jnp.float32
    nrm = lambda k, shape, s: jax.random.normal(k, shape, f32) * s
    return {
        "x": nrm(ks[0], (BATCH, SEQ, D_MODEL), 1.0),
        "meta_tokens": nrm(ks[1], (N_META, D_MODEL), 1.0),
        "norm1_g": 1.0 + nrm(ks[2], (DEPTH, D_MODEL), 0.02),
        "w_in": nrm(ks[3], (DEPTH, D_MODEL, IN_COLS), D_MODEL ** -0.5),
        "lambda_q1": nrm(ks[4], (DEPTH, DIFF_HEAD_DIM), 0.1),
        "lambda_k1": nrm(ks[5], (DEPTH, DIFF_HEAD_DIM), 0.1),
        "lambda_q2": nrm(ks[6], (DEPTH, DIFF_HEAD_DIM), 0.1),
        "lambda_k2": nrm(ks[7], (DEPTH, DIFF_HEAD_DIM), 0.1),
        "attn_subln_g": 1.0 + nrm(ks[8], (DEPTH, V_HEAD_DIM), 0.02),
        "conv_w": nrm(ks[9], (DEPTH, CONV_KERNEL, CONV_WIDTH), CONV_KERNEL ** -0.5),
        "conv_b": nrm(ks[10], (DEPTH, CONV_WIDTH), 0.02),
        "conv_norm_g": 1.0 + nrm(ks[11], (DEPTH, CONV_WIDTH), 0.02),
        "conv_norm_b": nrm(ks[12], (DEPTH, CONV_WIDTH), 0.02),
        "w_out": nrm(ks[13], (DEPTH, MIX_WIDTH, D_MODEL), MIX_WIDTH ** -0.5),
        "norm2_g": 1.0 + nrm(ks[14], (DEPTH, D_MODEL), 0.02),
        "peer_wq": nrm(ks[15], (DEPTH, D_MODEL, PEER_HEADS * PEER_QUERY_DIM), D_MODEL ** -0.5),
        "peer_subkeys": nrm(ks[16], (DEPTH, PEER_HEADS, 2, PEER_KEYS, PEER_HALF), PEER_HALF ** -0.5),
        "peer_u": nrm(ks[17], (DEPTH, PEER_EXPERTS, D_MODEL), D_MODEL ** -0.5),
        "peer_v": nrm(ks[18], (DEPTH, PEER_EXPERTS, D_MODEL), PEER_HEADS ** -0.5),
        "final_norm_g": 1.0 + nrm(ks[19], (D_MODEL,), 0.02),
    }


def reference(x, meta_tokens, norm1_g, w_in, lambda_q1, lambda_k1, lambda_q2, lambda_k2,
              attn_subln_g, conv_w, conv_b, conv_norm_g, conv_norm_b, w_out, norm2_g,
              peer_wq, peer_subkeys, peer_u, peer_v, final_norm_g):
    B, S = x.shape[0], x.shape[1]
    L = N_META + S
    Lp = ((L + Q_BLOCK - 1) // Q_BLOCK) * Q_BLOCK
    meta = jnp.broadcast_to(meta_tokens[None].astype(x.dtype), (B, N_META, D_MODEL))
    h = jnp.concatenate([meta, x], axis=1)
    h = jnp.pad(h, ((0, 0), (0, Lp - L), (0, 0)))
    slopes = alibi_slopes(N_ATTN_HEADS)

    for l in range(DEPTH):
        lam_init = lambda_init_for(l)
        xn = rms_norm(h, norm1_g[l])
        proj = xn @ w_in[l]
        q, k, v, ga, gg = jnp.split(
            proj, [ATTN_WIDTH, 2 * ATTN_WIDTH, 3 * ATTN_WIDTH, 3 * ATTN_WIDTH + CONV_WIDTH], axis=-1)
        q = q.reshape(B, Lp, N_ATTN_HEADS, 2, DIFF_HEAD_DIM)
        k = k.reshape(B, Lp, N_ATTN_HEADS, 2, DIFF_HEAD_DIM)
        v = v.reshape(B, Lp, N_ATTN_HEADS, V_HEAD_DIM)
        lam = (jnp.exp(jnp.sum(lambda_q1[l].astype(jnp.float32) * lambda_k1[l].astype(jnp.float32)))
               - jnp.exp(jnp.sum(lambda_q2[l].astype(jnp.float32) * lambda_k2[l].astype(jnp.float32)))
               + lam_init)
        attn_out = diff_attention(q, k, v, lam, lam_init, attn_subln_g[l], slopes)
        conv_out = conformer_conv(ga, gg, conv_w[l], conv_b[l], conv_norm_g[l], conv_norm_b[l])
        h = h + jnp.concatenate([attn_out, conv_out], axis=-1) @ w_out[l]

        xn2 = rms_norm(h, norm2_g[l])
        ffn = peer_ffn(xn2.reshape(B * Lp, D_MODEL), peer_wq[l], peer_subkeys[l], peer_u[l], peer_v[l])
        h = h + ffn.reshape(B, Lp, D_MODEL)

    h = rms_norm(h, final_norm_g)
    return h[:, N_META:N_META + S]
```

```python
import functools
import math

import jax
import jax.numpy as jnp
from jax import lax
from jax.experimental import pallas as pl
from jax.experimental.pallas import tpu as pltpu

F32 = jnp.float32
BF16 = jnp.bfloat16
I32 = jnp.int32
U32 = jnp.uint32

D_MODEL = 1024
N_META = 16
Q_BLOCK = 128
ATTN_WIDTH = 512
CONV_WIDTH = 512
N_HEADS = 4
HEAD_DIM = 64
V_DIM = 128
CONV_K = 31
CONV_GROUP = 64
PEER_HEADS = 8
PEER_KEYS = 128
PEER_TOPK = 16
PEER_PAIRS = PEER_HEADS * PEER_TOPK
N_EXPERTS = PEER_KEYS * PEER_KEYS
HALF_EXPERTS = N_EXPERTS // 2
EPS = 1e-6
NEG = -1e30

LANES = 128
SUBLANES = 8
ROW_TILES = D_MODEL // LANES

TM = 512
TQ = 384
CONV_ROWS = 64
CONV_PAD = 32
TOPK_TOKENS = 128
PEER_TB = 64
VMEM_TABLE_LIMIT = 52 * 1024 * 1024


def _inproj_kernel(h_ref, g_ref, w_ref, q_ref, k_ref, v_ref, u_ref):
    x = h_ref[...]
    ms = jnp.mean(x * x, axis=-1, keepdims=True)
    xn = (x * lax.rsqrt(ms + EPS) * g_ref[...]).astype(BF16)
    proj = jnp.dot(xn, w_ref[...], preferred_element_type=F32)
    q_ref[...] = (proj[:, 0:ATTN_WIDTH] * (HEAD_DIM ** -0.5)).astype(BF16)
    k_ref[...] = proj[:, ATTN_WIDTH:2 * ATTN_WIDTH].astype(BF16)
    v_ref[...] = proj[:, 2 * ATTN_WIDTH:3 * ATTN_WIDTH].astype(BF16)
    ga = proj[:, 3 * ATTN_WIDTH:3 * ATTN_WIDTH + CONV_WIDTH]
    gg = proj[:, 3 * ATTN_WIDTH + CONV_WIDTH:]
    u_ref[...] = ga * jax.nn.sigmoid(gg)


def _inproj(h2d, g, w_bf16):
    t = h2d.shape[0]
    n_cols = w_bf16.shape[1]
    return pl.pallas_call(
        _inproj_kernel,
        grid=(t // TM,),
        in_specs=[
            pl.BlockSpec((TM, D_MODEL), lambda i: (i, 0)),
            pl.BlockSpec((1, D_MODEL), lambda i: (0, 0)),
            pl.BlockSpec((D_MODEL, n_cols), lambda i: (0, 0)),
        ],
        out_specs=[
            pl.BlockSpec((TM, ATTN_WIDTH), lambda i: (i, 0)),
            pl.BlockSpec((TM, ATTN_WIDTH), lambda i: (i, 0)),
            pl.BlockSpec((TM, ATTN_WIDTH), lambda i: (i, 0)),
            pl.BlockSpec((TM, CONV_WIDTH), lambda i: (i, 0)),
        ],
        out_shape=[
            jax.ShapeDtypeStruct((t, ATTN_WIDTH), BF16),
            jax.ShapeDtypeStruct((t, ATTN_WIDTH), BF16),
            jax.ShapeDtypeStruct((t, ATTN_WIDTH), BF16),
            jax.ShapeDtypeStruct((t, CONV_WIDTH), F32),
        ],
        compiler_params=pltpu.CompilerParams(
            dimension_semantics=("arbitrary",), vmem_limit_bytes=48 * 1024 * 1024),
        name="inproj",
    )(h2d, g, w_bf16)


def _attn_kernel(slopes_ref, lq1_ref, lk1_ref, lq2_ref, lk2_ref, subg_ref,
                 q_ref, k_ref, v_ref, o_ref, *, lam_init):
    hd = pl.program_id(1)
    qi = pl.program_id(2)
    slope = slopes_ref[hd]
    lam = (jnp.exp(jnp.sum(lq1_ref[...] * lk1_ref[...], keepdims=True))
           - jnp.exp(jnp.sum(lq2_ref[...] * lk2_ref[...], keepdims=True)) + lam_init)

    q = q_ref[0]
    lane = lax.broadcasted_iota(I32, q.shape, 1)
    zero = jnp.zeros_like(q)
    qs = jnp.concatenate([jnp.where(lane < HEAD_DIM, q, zero),
                          jnp.where(lane >= HEAD_DIM, q, zero)], axis=0)

    q0 = qi * TQ
    col = lax.broadcasted_iota(I32, (1, TQ), 1)

    def step(j, carry, masked):
        m, l, acc = carry
        k0 = pl.multiple_of(j * TQ, TQ)
        kj = k_ref[0, pl.ds(k0, TQ), :]
        vj = v_ref[0, pl.ds(k0, TQ), :]
        s = lax.dot_general(qs, kj, (((1,), (1,)), ((), ())), preferred_element_type=F32)
        s = s + slope * (col + (k0 - q0)).astype(F32)
        if masked:
            row = lax.broadcasted_iota(I32, (2 * TQ, TQ), 0)
            row = jnp.where(row >= TQ, row - TQ, row)
            cc = lax.broadcasted_iota(I32, (2 * TQ, TQ), 1)
            s = jnp.where(cc <= row, s, NEG)
        m_new = jnp.maximum(m, jnp.max(s, axis=1, keepdims=True))
        alpha = jnp.exp(m - m_new)
        p = jnp.exp(s - m_new)
        l = alpha * l + jnp.sum(p, axis=1, keepdims=True)
        acc = alpha * acc + jnp.dot(p.astype(BF16), vj, preferred_element_type=F32)
        return m_new, l, acc

    init = (jnp.full((2 * TQ, 1), NEG, F32), jnp.zeros((2 * TQ, 1), F32),
            jnp.zeros((2 * TQ, V_DIM), F32))
    carry = lax.fori_loop(0, qi, lambda j, c: step(j, c, False), init)
    m, l, acc = step(qi, carry, True)
    o = acc / l
    a = o[:TQ] - lam * o[TQ:]
    ms = jnp.mean(a * a, axis=-1, keepdims=True)
    y = a * lax.rsqrt(ms + EPS) * subg_ref[...] * (1.0 - lam_init)
    o_ref[0] = y.astype(BF16)


def _attention(q, k, v, slopes, lq1, lk1, lq2, lk2, subg, lam_init):
    b, lp, _ = q.shape
    nq = lp // TQ
    vec = lambda n: pl.BlockSpec((1, n), lambda bi, hi, i: (0, 0))
    return pl.pallas_call(
        functools.partial(_attn_kernel, lam_init=lam_init),
        grid=(b, N_HEADS, nq),
        in_specs=[
            pl.BlockSpec(memory_space=pltpu.SMEM),
            vec(HEAD_DIM), vec(HEAD_DIM), vec(HEAD_DIM), vec(HEAD_DIM), vec(V_DIM),
            pl.BlockSpec((1, TQ, V_DIM), lambda bi, hi, i: (bi, i, hi)),
            pl.BlockSpec((1, lp, V_DIM), lambda bi, hi, i: (bi, 0, hi)),
            pl.BlockSpec((1, lp, V_DIM), lambda bi, hi, i: (bi, 0, hi)),
        ],
        out_specs=pl.BlockSpec((1, TQ, V_DIM), lambda bi, hi, i: (bi, i, hi)),
        out_shape=jax.ShapeDtypeStruct((b, lp, ATTN_WIDTH), BF16),
        compiler_params=pltpu.CompilerParams(
            dimension_semantics=("arbitrary", "arbitrary", "arbitrary"),
            vmem_limit_bytes=48 * 1024 * 1024),
        name="diff_attn",
    )(slopes, lq1, lk1, lq2, lk2, subg, q, k, v)


def _conv_kernel(u_ref, w_ref, b_ref, gavg_ref, g_ref, beta_ref, o_ref, upad_ref, y_ref):
    lp = u_ref.shape[1]
    upad_ref[0:CONV_PAD, :] = jnp.zeros((CONV_PAD, CONV_WIDTH), F32)
    upad_ref[CONV_PAD:, :] = u_ref[0]

    def chunk(c, _):
        base = pl.multiple_of(c * CONV_ROWS, CONV_ROWS)
        for lb in range(CONV_WIDTH // LANES):
            ls = slice(lb * LANES, (lb + 1) * LANES)
            acc = jnp.zeros((CONV_ROWS, LANES), F32) + b_ref[:, ls]
            win = upad_ref[pl.ds(base, CONV_ROWS + CONV_PAD), ls]
            for t in range(CONV_K):
                off = CONV_PAD - (CONV_K - 1) + t
                acc = acc + w_ref[t:t + 1, ls] * win[off:off + CONV_ROWS, :]
            y_ref[:, ls] = acc
        y = y_ref[...]
        mu = jnp.dot(y, gavg_ref[...], precision=lax.Precision.HIGHEST,
                     preferred_element_type=F32)
        d = y - mu
        var = jnp.dot(d * d, gavg_ref[...], precision=lax.Precision.HIGHEST,
                      preferred_element_type=F32)
        yn = d * lax.rsqrt(var + EPS) * g_ref[...] + beta_ref[...]
        o_ref[0, pl.ds(base, CONV_ROWS), :] = (yn * jax.nn.sigmoid(yn)).astype(BF16)
        return 0

    lax.fori_loop(0, lp // CONV_ROWS, chunk, 0)


def _conformer_conv(u, conv_w, conv_b, gavg, gn_g, gn_b):
    b, lp, c = u.shape
    full = lambda shape: pl.BlockSpec(shape, lambda bi: (0,) * len(shape))
    return pl.pallas_call(
        _conv_kernel,
        grid=(b,),
        in_specs=[
            pl.BlockSpec((1, lp, c), lambda bi: (bi, 0, 0)),
            full((CONV_K, c)), full((1, c)), full((c, c)), full((1, c)), full((1, c)),
        ],
        out_specs=pl.BlockSpec((1, lp, c), lambda bi: (bi, 0, 0)),
        out_shape=jax.ShapeDtypeStruct((b, lp, c), BF16),
        scratch_shapes=[pltpu.VMEM((lp + CONV_PAD, c), F32), pltpu.VMEM((CONV_ROWS, c), F32)],
        compiler_params=pltpu.CompilerParams(
            dimension_semantics=("arbitrary",), vmem_limit_bytes=56 * 1024 * 1024),
        name="conformer_conv",
    )(u, conv_w, conv_b, gavg, gn_g, gn_b)


def _outproj_kernel(a_ref, c_ref, h_ref, wo_ref, g_ref, wq_ref, h1_ref, xn_ref, qp_ref):
    mix = (jnp.dot(a_ref[...], wo_ref[0:ATTN_WIDTH, :], preferred_element_type=F32)
           + jnp.dot(c_ref[...], wo_ref[ATTN_WIDTH:, :], preferred_element_type=F32))
    h1 = h_ref[...] + mix
    h1_ref[...] = h1
    ms = jnp.mean(h1 * h1, axis=-1, keepdims=True)
    xn = h1 * lax.rsqrt(ms + EPS) * g_ref[...]
    xn_ref[...] = xn
    qp = jnp.dot(xn.astype(BF16), wq_ref[...], preferred_element_type=F32)
    for hp in range(2 * PEER_HEADS):
        qp_ref[hp] = qp[:, hp * PEER_KEYS:(hp + 1) * PEER_KEYS].astype(BF16)


def _outproj(attn2d, conv2d, h2d, wo_bf16, g2, wq_bf16):
    t = h2d.shape[0]
    nq = wq_bf16.shape[1]
    return pl.pallas_call(
        _outproj_kernel,
        grid=(t // TM,),
        in_specs=[
            pl.BlockSpec((TM, ATTN_WIDTH), lambda i: (i, 0)),
            pl.BlockSpec((TM, CONV_WIDTH), lambda i: (i, 0)),
            pl.BlockSpec((TM, D_MODEL), lambda i: (i, 0)),
            pl.BlockSpec((D_MODEL, D_MODEL), lambda i: (0, 0)),
            pl.BlockSpec((1, D_MODEL), lambda i: (0, 0)),
            pl.BlockSpec((D_MODEL, nq), lambda i: (0, 0)),
        ],
        out_specs=[
            pl.BlockSpec((TM, D_MODEL), lambda i: (i, 0)),
            pl.BlockSpec((TM, D_MODEL), lambda i: (i, 0)),
            pl.BlockSpec((2 * PEER_HEADS, TM, PEER_KEYS), lambda i: (0, i, 0)),
        ],
        out_shape=[
            jax.ShapeDtypeStruct((t, D_MODEL), F32),
            jax.ShapeDtypeStruct((t, D_MODEL), F32),
            jax.ShapeDtypeStruct((2 * PEER_HEADS, t, PEER_KEYS), BF16),
        ],
        compiler_params=pltpu.CompilerParams(
            dimension_semantics=("arbitrary",), vmem_limit_bytes=48 * 1024 * 1024),
        name="outproj_peerq",
    )(attn2d, conv2d, h2d, wo_bf16, g2, wq_bf16)


def _extract_top(s, n_rows, val_ref, idx_ref):
    iota = lax.broadcasted_iota(I32, s.shape, 0)
    for r in range(PEER_TOPK):
        m = jnp.max(s, axis=0, keepdims=True)
        idx = jnp.min(jnp.where(s == m, iota, n_rows), axis=0, keepdims=True)
        val_ref[r:r + 1, :] = m
        idx_ref[r:r + 1, :] = idx
        s = jnp.where(iota == idx, NEG, s)


def _topk_kernel(qp_ref, keys_ref, r_ref, sh_ref, gate_ref,
                 s1_ref, i1_ref, s2_ref, i2_ref, ct_ref, ci_ref, e_ref, g_ref):
    def head(h, _):
        for p, (sv, si) in enumerate(((s1_ref, i1_ref), (s2_ref, i2_ref))):
            hp = 2 * h + p
            st = lax.dot_general(keys_ref[hp], qp_ref[hp], (((1,), (1,)), ((), ())),
                                 preferred_element_type=F32)
            _extract_top(st, PEER_KEYS, sv, si)
        s1 = s1_ref[...]
        s2 = s2_ref[...]
        cand = jnp.concatenate(
            [s1[a:a + 1, :] + s2 for a in range(PEER_TOPK)], axis=0)
        _extract_top(cand, PEER_TOPK * PEER_TOPK, ct_ref, ci_ref)
        ct = ct_ref[...]
        ci = ci_ref[...]
        hi = ci >> 4
        lo = ci & (PEER_TOPK - 1)
        i1 = i1_ref[...]
        i2 = i2_ref[...]
        e1 = jnp.zeros_like(ci)
        e2 = jnp.zeros_like(ci)
        for a in range(PEER_TOPK):
            e1 = jnp.where(hi == a, i1[a:a + 1, :], e1)
            e2 = jnp.where(lo == a, i2[a:a + 1, :], e2)
        e = e1 * PEER_KEYS + e2
        ex = jnp.exp(ct - jnp.max(ct, axis=0, keepdims=True))
        gate = ex / jnp.sum(ex, axis=0, keepdims=True)
        row0 = pl.multiple_of(h * PEER_TOPK, PEER_TOPK)
        e_ref[pl.ds(row0, PEER_TOPK), :] = e
        g_ref[pl.ds(row0, PEER_TOPK), :] = gate
        return 0

    lax.fori_loop(0, PEER_HEADS, head, 0)
    e = e_ref[...].T
    r_ref[...] = e & (HALF_EXPERTS - 1)
    sh_ref[...] = (e >> 13) << 4
    gate_ref[...] = g_ref[...].T


def _peer_topk(qp, keys_bf16):
    t = qp.shape[1]
    tt = TOPK_TOKENS
    sc = lambda dt: pltpu.VMEM((PEER_TOPK, tt), dt)
    return pl.pallas_call(
        _topk_kernel,
        grid=(t // tt,),
        in_specs=[
            pl.BlockSpec((2 * PEER_HEADS, tt, PEER_KEYS), lambda i: (0, i, 0)),
            pl.BlockSpec((2 * PEER_HEADS, PEER_KEYS, PEER_KEYS), lambda i: (0, 0, 0)),
        ],
        out_specs=[
            pl.BlockSpec((tt, PEER_PAIRS), lambda i: (i, 0)),
            pl.BlockSpec((tt, PEER_PAIRS), lambda i: (i, 0)),
            pl.BlockSpec((tt, PEER_PAIRS), lambda i: (i, 0)),
        ],
        out_shape=[
            jax.ShapeDtypeStruct((t, PEER_PAIRS), I32),
            jax.ShapeDtypeStruct((t, PEER_PAIRS), I32),
            jax.ShapeDtypeStruct((t, PEER_PAIRS), F32),
        ],
        scratch_shapes=[sc(F32), sc(I32), sc(F32), sc(I32), sc(F32), sc(I32),
                        pltpu.VMEM((PEER_PAIRS, tt), I32), pltpu.VMEM((PEER_PAIRS, tt), F32)],
        compiler_params=pltpu.CompilerParams(dimension_semantics=("arbitrary",)),
        name="peer_topk",
    )(qp, keys_bf16)


def _pack_kernel(hi_ref, lo_ref, o_ref):
    hi = pltpu.bitcast(hi_ref[...].astype(BF16).astype(F32), U32)
    lo = pltpu.bitcast(lo_ref[...].astype(BF16).astype(F32), U32)
    o_ref[...] = hi | (lo >> 16)


def _pack_table(tab):
    rows = 512
    nb = HALF_EXPERTS // rows
    packed = pl.pallas_call(
        _pack_kernel,
        grid=(nb,),
        in_specs=[pl.BlockSpec((rows, D_MODEL), lambda i: (i, 0)),
                  pl.BlockSpec((rows, D_MODEL), lambda i: (i + nb, 0))],
        out_specs=pl.BlockSpec((rows, D_MODEL), lambda i: (i, 0)),
        out_shape=jax.ShapeDtypeStruct((HALF_EXPERTS, D_MODEL), U32),
        compiler_params=pltpu.CompilerParams(dimension_semantics=("arbitrary",)),
        name="pack_table",
    )(tab, tab)
    return packed.reshape(HALF_EXPERTS, ROW_TILES, LANES)


def _expert_row(tab_ref, r, sh):
    word = tab_ref[r]
    return pltpu.bitcast((word << sh.astype(U32)) & jnp.uint32(0xFFFF0000), F32)


def _peer_act_kernel(r_ref, sh_ref, x_ref, gate_ref, tab_ref, w_ref, part_ref, a_ref):
    ones = jnp.ones((SUBLANES, LANES), F32)

    def token(t, _):
        xt = x_ref[t]
        for j in range(PEER_PAIRS):
            f = _expert_row(tab_ref, r_ref[t, j], sh_ref[t, j])
            part_ref[j:j + 1, :] = jnp.sum(f * xt, axis=0, keepdims=True)
        tot = lax.dot_general(ones, part_ref[...], (((1,), (1,)), ((), ())),
                              precision=lax.Precision.HIGHEST, preferred_element_type=F32)
        a_ref[pl.ds(t, 1), :] = tot[0:1, :]
        return 0

    lax.fori_loop(0, PEER_TB, token, 0)
    a = a_ref[...]
    act = 0.5 * a * (1.0 + lax.erf(a * (2.0 ** -0.5)))
    w_ref[...] = gate_ref[...] * act


def _peer_act(r, sh, x3, gate, tab):
    t = r.shape[0]
    return pl.pallas_call(
        _peer_act_kernel,
        grid=(t // PEER_TB,),
        in_specs=[
            pl.BlockSpec((PEER_TB, PEER_PAIRS), lambda i: (i, 0), memory_space=pltpu.SMEM),
            pl.BlockSpec((PEER_TB, PEER_PAIRS), lambda i: (i, 0), memory_space=pltpu.SMEM),
            pl.BlockSpec((PEER_TB, ROW_TILES, LANES), lambda i: (i, 0, 0)),
            pl.BlockSpec((PEER_TB, PEER_PAIRS), lambda i: (i, 0)),
            pl.BlockSpec((HALF_EXPERTS, ROW_TILES, LANES), lambda i: (0, 0, 0),
                         pipeline_mode=pl.Buffered(1)),
        ],
        out_specs=pl.BlockSpec((PEER_TB, PEER_PAIRS), lambda i: (i, 0)),
        out_shape=jax.ShapeDtypeStruct((t, PEER_PAIRS), F32),
        scratch_shapes=[pltpu.VMEM((PEER_PAIRS, LANES), F32),
                        pltpu.VMEM((PEER_TB, PEER_PAIRS), F32)],
        compiler_params=pltpu.CompilerParams(
            dimension_semantics=("arbitrary",), vmem_limit_bytes=VMEM_TABLE_LIMIT),
        name="peer_act",
    )(r, sh, x3, gate, tab)


def _peer_out_kernel(r_ref, sh_ref, w_ref, h_ref, g_ref, tab_ref, o_ref):
    n_acc = 4

    def token(t, _):
        accs = [jnp.zeros((SUBLANES, LANES), F32) for _ in range(n_acc)]
        for j in range(PEER_PAIRS):
            f = _expert_row(tab_ref, r_ref[t, j], sh_ref[t, j])
            accs[j % n_acc] = accs[j % n_acc] + w_ref[t, j] * f
        y = h_ref[t] + ((accs[0] + accs[1]) + (accs[2] + accs[3]))
        ms = jnp.sum(y * y, keepdims=True) * (1.0 / D_MODEL)
        o_ref[t] = y * lax.rsqrt(ms + EPS) * g_ref[...]
        return 0

    lax.fori_loop(0, PEER_TB, token, 0)


def _peer_out(r, sh, w, h3, g3, tab):
    t = r.shape[0]
    smem = lambda: pl.BlockSpec((PEER_TB, PEER_PAIRS), lambda i: (i, 0), memory_space=pltpu.SMEM)
    return pl.pallas_call(
        _peer_out_kernel,
        grid=(t // PEER_TB,),
        in_specs=[
            smem(), smem(), smem(),
            pl.BlockSpec((PEER_TB, ROW_TILES, LANES), lambda i: (i, 0, 0)),
            pl.BlockSpec((ROW_TILES, LANES), lambda i: (0, 0)),
            pl.BlockSpec((HALF_EXPERTS, ROW_TILES, LANES), lambda i: (0, 0, 0),
                         pipeline_mode=pl.Buffered(1)),
        ],
        out_specs=pl.BlockSpec((PEER_TB, ROW_TILES, LANES), lambda i: (i, 0, 0)),
        out_shape=jax.ShapeDtypeStruct((t, ROW_TILES, LANES), F32),
        compiler_params=pltpu.CompilerParams(
            dimension_semantics=("arbitrary",), vmem_limit_bytes=VMEM_TABLE_LIMIT),
        name="peer_out",
    )(r, sh, w, h3, g3, tab)


def kernel(x, meta_tokens, norm1_g, w_in, lambda_q1, lambda_k1, lambda_q2, lambda_k2,
           attn_subln_g, conv_w, conv_b, conv_norm_g, conv_norm_b, w_out, norm2_g,
           peer_wq, peer_subkeys, peer_u, peer_v, final_norm_g):
    b, s, _ = x.shape
    seq = N_META + s
    lp = ((seq + Q_BLOCK - 1) // Q_BLOCK) * Q_BLOCK
    t = b * lp
    assert lp % TQ == 0 and lp % CONV_ROWS == 0 and t % TM == 0 and t % PEER_TB == 0

    meta = jnp.broadcast_to(meta_tokens[None].astype(x.dtype), (b, N_META, D_MODEL))
    h = jnp.concatenate([meta, x, jnp.zeros((b, lp - seq, D_MODEL), x.dtype)], axis=1)
    h2d = h.reshape(t, D_MODEL)

    lam_init = 0.8 - 0.6 * math.exp(-0.3 * 0)
    slopes = jnp.asarray([2.0 ** (-8.0 * (i + 1) / N_HEADS) for i in range(N_HEADS)], F32)
    group = jnp.arange(CONV_WIDTH) // CONV_GROUP
    gavg = (group[:, None] == group[None, :]).astype(F32) * (1.0 / CONV_GROUP)

    q, k, v, u = _inproj(h2d, norm1_g[0][None], w_in[0].astype(BF16))
    attn = _attention(q.reshape(b, lp, -1), k.reshape(b, lp, -1), v.reshape(b, lp, -1), slopes,
                      lambda_q1[0][None], lambda_k1[0][None], lambda_q2[0][None],
                      lambda_k2[0][None], attn_subln_g[0][None], lam_init)
    conv = _conformer_conv(u.reshape(b, lp, -1), conv_w[0], conv_b[0][None], gavg,
                           conv_norm_g[0][None], conv_norm_b[0][None])
    h1, xn2, qp = _outproj(attn.reshape(t, -1), conv.reshape(t, -1), h2d,
                           w_out[0].astype(BF16), norm2_g[0][None], peer_wq[0].astype(BF16))
    keys = peer_subkeys[0].reshape(2 * PEER_HEADS, PEER_KEYS, PEER_KEYS).astype(BF16)
    r, sh, gate = _peer_topk(qp, keys)
    w = _peer_act(r, sh, xn2.reshape(t, ROW_TILES, LANES), gate, _pack_table(peer_u[0]))
    out = _peer_out(r, sh, w, h1.reshape(t, ROW_TILES, LANES),
                    final_norm_g.reshape(ROW_TILES, LANES), _pack_table(peer_v[0]))
    return out.reshape(b, lp, D_MODEL)[:, N_META:N_META + s]
```

```python
import functools
import math

import jax
import jax.numpy as jnp
from jax import lax
from jax.experimental import pallas as pl
from jax.experimental.pallas import tpu as pltpu

F32 = jnp.float32
BF16 = jnp.bfloat16
I32 = jnp.int32
U32 = jnp.uint32

D_MODEL = 1024
N_META = 16
Q_BLOCK = 128
ATTN_WIDTH = 512
CONV_WIDTH = 512
N_HEADS = 4
HEAD_DIM = 64
V_DIM = 128
CONV_K = 31
CONV_GROUP = 64
PEER_HEADS = 8
PEER_KEYS = 128
PEER_TOPK = 16
PEER_PAIRS = PEER_HEADS * PEER_TOPK
N_EXPERTS = PEER_KEYS * PEER_KEYS
HALF_EXPERTS = N_EXPERTS // 2
EPS = 1e-6
NEG = -1e30

LANES = 128
SUBLANES = 8
ROW_TILES = D_MODEL // LANES

TM = 512
TQ = 384
CONV_ROWS = 64
CONV_PAD = 32
TOPK_TOKENS = 128
PEER_TB = 64
PAIR_GROUP = 32
PREP_UNROLL = 4
VMEM_TABLE_LIMIT = 52 * 1024 * 1024


def _inproj_kernel(h_ref, g_ref, w_ref, q_ref, k_ref, v_ref, u_ref):
    x = h_ref[...]
    ms = jnp.mean(x * x, axis=-1, keepdims=True)
    xn = (x * lax.rsqrt(ms + EPS) * g_ref[...]).astype(BF16)
    proj = jnp.dot(xn, w_ref[...], preferred_element_type=F32)
    q_ref[...] = (proj[:, 0:ATTN_WIDTH] * (HEAD_DIM ** -0.5)).astype(BF16)
    k_ref[...] = proj[:, ATTN_WIDTH:2 * ATTN_WIDTH].astype(BF16)
    v_ref[...] = proj[:, 2 * ATTN_WIDTH:3 * ATTN_WIDTH].astype(BF16)
    ga = proj[:, 3 * ATTN_WIDTH:3 * ATTN_WIDTH + CONV_WIDTH]
    gg = proj[:, 3 * ATTN_WIDTH + CONV_WIDTH:]
    u_ref[...] = ga * jax.nn.sigmoid(gg)


def _inproj(h2d, g, w_bf16):
    t = h2d.shape[0]
    n_cols = w_bf16.shape[1]
    return pl.pallas_call(
        _inproj_kernel,
        grid=(t // TM,),
        in_specs=[
            pl.BlockSpec((TM, D_MODEL), lambda i: (i, 0)),
            pl.BlockSpec((1, D_MODEL), lambda i: (0, 0)),
            pl.BlockSpec((D_MODEL, n_cols), lambda i: (0, 0)),
        ],
        out_specs=[
            pl.BlockSpec((TM, ATTN_WIDTH), lambda i: (i, 0)),
            pl.BlockSpec((TM, ATTN_WIDTH), lambda i: (i, 0)),
            pl.BlockSpec((TM, ATTN_WIDTH), lambda i: (i, 0)),
            pl.BlockSpec((TM, CONV_WIDTH), lambda i: (i, 0)),
        ],
        out_shape=[
            jax.ShapeDtypeStruct((t, ATTN_WIDTH), BF16),
            jax.ShapeDtypeStruct((t, ATTN_WIDTH), BF16),
            jax.ShapeDtypeStruct((t, ATTN_WIDTH), BF16),
            jax.ShapeDtypeStruct((t, CONV_WIDTH), F32),
        ],
        compiler_params=pltpu.CompilerParams(
            dimension_semantics=("arbitrary",), vmem_limit_bytes=48 * 1024 * 1024),
        name="inproj",
    )(h2d, g, w_bf16)


def _attn_kernel(slopes_ref, lq1_ref, lk1_ref, lq2_ref, lk2_ref, subg_ref,
                 q_ref, k_ref, v_ref, o_ref, *, lam_init):
    hd = pl.program_id(1)
    qi = pl.program_id(2)
    slope = slopes_ref[hd]
    lam = (jnp.exp(jnp.sum(lq1_ref[...] * lk1_ref[...], keepdims=True))
           - jnp.exp(jnp.sum(lq2_ref[...] * lk2_ref[...], keepdims=True)) + lam_init)

    q = q_ref[0]
    lane = lax.broadcasted_iota(I32, q.shape, 1)
    zero = jnp.zeros_like(q)
    qs = jnp.concatenate([jnp.where(lane < HEAD_DIM, q, zero),
                          jnp.where(lane >= HEAD_DIM, q, zero)], axis=0)

    q0 = qi * TQ
    col = lax.broadcasted_iota(I32, (1, TQ), 1)

    def step(j, carry, masked):
        m, l, acc = carry
        k0 = pl.multiple_of(j * TQ, TQ)
        kj = k_ref[0, pl.ds(k0, TQ), :]
        vj = v_ref[0, pl.ds(k0, TQ), :]
        s = lax.dot_general(qs, kj, (((1,), (1,)), ((), ())), preferred_element_type=F32)
        s = s + slope * (col + (k0 - q0)).astype(F32)
        if masked:
            row = lax.broadcasted_iota(I32, (2 * TQ, TQ), 0)
            row = jnp.where(row >= TQ, row - TQ, row)
            cc = lax.broadcasted_iota(I32, (2 * TQ, TQ), 1)
            s = jnp.where(cc <= row, s, NEG)
        m_new = jnp.maximum(m, jnp.max(s, axis=1, keepdims=True))
        alpha = jnp.exp(m - m_new)
        p = jnp.exp(s - m_new)
        l = alpha * l + jnp.sum(p, axis=1, keepdims=True)
        acc = alpha * acc + jnp.dot(p.astype(BF16), vj, preferred_element_type=F32)
        return m_new, l, acc

    init = (jnp.full((2 * TQ, 1), NEG, F32), jnp.zeros((2 * TQ, 1), F32),
            jnp.zeros((2 * TQ, V_DIM), F32))
    carry = lax.fori_loop(0, qi, lambda j, c: step(j, c, False), init)
    m, l, acc = step(qi, carry, True)
    o = acc / l
    a = o[:TQ] - lam * o[TQ:]
    ms = jnp.mean(a * a, axis=-1, keepdims=True)
    y = a * lax.rsqrt(ms + EPS) * subg_ref[...] * (1.0 - lam_init)
    o_ref[0] = y.astype(BF16)


def _attention(q, k, v, slopes, lq1, lk1, lq2, lk2, subg, lam_init):
    b, lp, _ = q.shape
    nq = lp // TQ
    vec = lambda n: pl.BlockSpec((1, n), lambda bi, hi, i: (0, 0))
    return pl.pallas_call(
        functools.partial(_attn_kernel, lam_init=lam_init),
        grid=(b, N_HEADS, nq),
        in_specs=[
            pl.BlockSpec(memory_space=pltpu.SMEM),
            vec(HEAD_DIM), vec(HEAD_DIM), vec(HEAD_DIM), vec(HEAD_DIM), vec(V_DIM),
            pl.BlockSpec((1, TQ, V_DIM), lambda bi, hi, i: (bi, i, hi)),
            pl.BlockSpec((1, lp, V_DIM), lambda bi, hi, i: (bi, 0, hi)),
            pl.BlockSpec((1, lp, V_DIM), lambda bi, hi, i: (bi, 0, hi)),
        ],
        out_specs=pl.BlockSpec((1, TQ, V_DIM), lambda bi, hi, i: (bi, i, hi)),
        out_shape=jax.ShapeDtypeStruct((b, lp, ATTN_WIDTH), BF16),
        compiler_params=pltpu.CompilerParams(
            dimension_semantics=("arbitrary", "arbitrary", "arbitrary"),
            vmem_limit_bytes=48 * 1024 * 1024),
        name="diff_attn",
    )(slopes, lq1, lk1, lq2, lk2, subg, q, k, v)


def _conv_kernel(u_ref, w_ref, b_ref, gavg_ref, g_ref, beta_ref, o_ref, upad_ref, y_ref):
    lp = u_ref.shape[1]
    upad_ref[0:CONV_PAD, :] = jnp.zeros((CONV_PAD, CONV_WIDTH), F32)
    upad_ref[CONV_PAD:, :] = u_ref[0]

    def chunk(c, _):
        base = pl.multiple_of(c * CONV_ROWS, CONV_ROWS)
        for lb in range(CONV_WIDTH // LANES):
            ls = slice(lb * LANES, (lb + 1) * LANES)
            acc = jnp.zeros((CONV_ROWS, LANES), F32) + b_ref[:, ls]
            win = upad_ref[pl.ds(base, CONV_ROWS + CONV_PAD), ls]
            for t in range(CONV_K):
                off = CONV_PAD - (CONV_K - 1) + t
                acc = acc + w_ref[t:t + 1, ls] * win[off:off + CONV_ROWS, :]
            y_ref[:, ls] = acc
        y = y_ref[...]
        mu = jnp.dot(y, gavg_ref[...], precision=lax.Precision.HIGHEST,
                     preferred_element_type=F32)
        d = y - mu
        var = jnp.dot(d * d, gavg_ref[...], precision=lax.Precision.HIGHEST,
                      preferred_element_type=F32)
        yn = d * lax.rsqrt(var + EPS) * g_ref[...] + beta_ref[...]
        o_ref[0, pl.ds(base, CONV_ROWS), :] = (yn * jax.nn.sigmoid(yn)).astype(BF16)
        return 0

    lax.fori_loop(0, lp // CONV_ROWS, chunk, 0)


def _conformer_conv(u, conv_w, conv_b, gavg, gn_g, gn_b):
    b, lp, c = u.shape
    full = lambda shape: pl.BlockSpec(shape, lambda bi: (0,) * len(shape))
    return pl.pallas_call(
        _conv_kernel,
        grid=(b,),
        in_specs=[
            pl.BlockSpec((1, lp, c), lambda bi: (bi, 0, 0)),
            full((CONV_K, c)), full((1, c)), full((c, c)), full((1, c)), full((1, c)),
        ],
        out_specs=pl.BlockSpec((1, lp, c), lambda bi: (bi, 0, 0)),
        out_shape=jax.ShapeDtypeStruct((b, lp, c), BF16),
        scratch_shapes=[pltpu.VMEM((lp + CONV_PAD, c), F32), pltpu.VMEM((CONV_ROWS, c), F32)],
        compiler_params=pltpu.CompilerParams(
            dimension_semantics=("arbitrary",), vmem_limit_bytes=56 * 1024 * 1024),
        name="conformer_conv",
    )(u, conv_w, conv_b, gavg, gn_g, gn_b)


def _outproj_kernel(a_ref, c_ref, h_ref, wo_ref, g_ref, wq_ref, h1_ref, xn_ref, qp_ref):
    mix = (jnp.dot(a_ref[...], wo_ref[0:ATTN_WIDTH, :], preferred_element_type=F32)
           + jnp.dot(c_ref[...], wo_ref[ATTN_WIDTH:, :], preferred_element_type=F32))
    h1 = h_ref[...] + mix
    h1_ref[...] = h1
    ms = jnp.mean(h1 * h1, axis=-1, keepdims=True)
    xn = h1 * lax.rsqrt(ms + EPS) * g_ref[...]
    xn_ref[...] = xn
    qp = jnp.dot(xn.astype(BF16), wq_ref[...], preferred_element_type=F32)
    for hp in range(2 * PEER_HEADS):
        qp_ref[hp] = qp[:, hp * PEER_KEYS:(hp + 1) * PEER_KEYS].astype(BF16)


def _outproj(attn2d, conv2d, h2d, wo_bf16, g2, wq_bf16):
    t = h2d.shape[0]
    nq = wq_bf16.shape[1]
    return pl.pallas_call(
        _outproj_kernel,
        grid=(t // TM,),
        in_specs=[
            pl.BlockSpec((TM, ATTN_WIDTH), lambda i: (i, 0)),
            pl.BlockSpec((TM, CONV_WIDTH), lambda i: (i, 0)),
            pl.BlockSpec((TM, D_MODEL), lambda i: (i, 0)),
            pl.BlockSpec((D_MODEL, D_MODEL), lambda i: (0, 0)),
            pl.BlockSpec((1, D_MODEL), lambda i: (0, 0)),
            pl.BlockSpec((D_MODEL, nq), lambda i: (0, 0)),
        ],
        out_specs=[
            pl.BlockSpec((TM, D_MODEL), lambda i: (i, 0)),
            pl.BlockSpec((TM, D_MODEL), lambda i: (i, 0)),
            pl.BlockSpec((2 * PEER_HEADS, TM, PEER_KEYS), lambda i: (0, i, 0)),
        ],
        out_shape=[
            jax.ShapeDtypeStruct((t, D_MODEL), F32),
            jax.ShapeDtypeStruct((t, D_MODEL), F32),
            jax.ShapeDtypeStruct((2 * PEER_HEADS, t, PEER_KEYS), BF16),
        ],
        compiler_params=pltpu.CompilerParams(
            dimension_semantics=("arbitrary",), vmem_limit_bytes=48 * 1024 * 1024),
        name="outproj_peerq",
    )(attn2d, conv2d, h2d, wo_bf16, g2, wq_bf16)


def _extract_top(s, n_rows, val_ref, idx_ref):
    iota = lax.broadcasted_iota(I32, s.shape, 0)
    for r in range(PEER_TOPK):
        m = jnp.max(s, axis=0, keepdims=True)
        idx = jnp.min(jnp.where(s == m, iota, n_rows), axis=0, keepdims=True)
        val_ref[r:r + 1, :] = m
        idx_ref[r:r + 1, :] = idx
        s = jnp.where(iota == idx, NEG, s)


def _topk_kernel(qp_ref, keys_ref, r_ref, sh_ref, gate_ref,
                 s1_ref, i1_ref, s2_ref, i2_ref, ct_ref, ci_ref, e_ref, g_ref):
    def head(h, _):
        for p, (sv, si) in enumerate(((s1_ref, i1_ref), (s2_ref, i2_ref))):
            hp = 2 * h + p
            st = lax.dot_general(keys_ref[hp], qp_ref[hp], (((1,), (1,)), ((), ())),
                                 preferred_element_type=F32)
            _extract_top(st, PEER_KEYS, sv, si)
        s1 = s1_ref[...]
        s2 = s2_ref[...]
        cand = jnp.concatenate(
            [s1[a:a + 1, :] + s2 for a in range(PEER_TOPK)], axis=0)
        _extract_top(cand, PEER_TOPK * PEER_TOPK, ct_ref, ci_ref)
        ct = ct_ref[...]
        ci = ci_ref[...]
        hi = ci >> 4
        lo = ci & (PEER_TOPK - 1)
        i1 = i1_ref[...]
        i2 = i2_ref[...]
        e1 = jnp.zeros_like(ci)
        e2 = jnp.zeros_like(ci)
        for a in range(PEER_TOPK):
            e1 = jnp.where(hi == a, i1[a:a + 1, :], e1)
            e2 = jnp.where(lo == a, i2[a:a + 1, :], e2)
        e = e1 * PEER_KEYS + e2
        ex = jnp.exp(ct - jnp.max(ct, axis=0, keepdims=True))
        gate = ex / jnp.sum(ex, axis=0, keepdims=True)
        row0 = pl.multiple_of(h * PEER_TOPK, PEER_TOPK)
        e_ref[pl.ds(row0, PEER_TOPK), :] = e
        g_ref[pl.ds(row0, PEER_TOPK), :] = gate
        return 0

    lax.fori_loop(0, PEER_HEADS, head, 0)
    e = e_ref[...].T
    r_ref[...] = (e & (HALF_EXPERTS - 1)) * ROW_TILES
    sh_ref[...] = ((e >> 13) << 4).astype(F32)
    gate_ref[...] = g_ref[...].T


def _peer_topk(qp, keys_bf16):
    t = qp.shape[1]
    tt = TOPK_TOKENS
    sc = lambda dt: pltpu.VMEM((PEER_TOPK, tt), dt)
    return pl.pallas_call(
        _topk_kernel,
        grid=(t // tt,),
        in_specs=[
            pl.BlockSpec((2 * PEER_HEADS, tt, PEER_KEYS), lambda i: (0, i, 0)),
            pl.BlockSpec((2 * PEER_HEADS, PEER_KEYS, PEER_KEYS), lambda i: (0, 0, 0)),
        ],
        out_specs=[
            pl.BlockSpec((tt, PEER_PAIRS), lambda i: (i, 0)),
            pl.BlockSpec((tt, PEER_PAIRS), lambda i: (i, 0)),
            pl.BlockSpec((tt, PEER_PAIRS), lambda i: (i, 0)),
        ],
        out_shape=[
            jax.ShapeDtypeStruct((t, PEER_PAIRS), I32),
            jax.ShapeDtypeStruct((t, PEER_PAIRS), F32),
            jax.ShapeDtypeStruct((t, PEER_PAIRS), F32),
        ],
        scratch_shapes=[sc(F32), sc(I32), sc(F32), sc(I32), sc(F32), sc(I32),
                        pltpu.VMEM((PEER_PAIRS, tt), I32), pltpu.VMEM((PEER_PAIRS, tt), F32)],
        compiler_params=pltpu.CompilerParams(dimension_semantics=("arbitrary",)),
        name="peer_topk",
    )(qp, keys_bf16)


def _pack_kernel(hi_ref, lo_ref, o_ref):
    hi = pltpu.bitcast(hi_ref[...].astype(BF16).astype(F32), U32)
    lo = pltpu.bitcast(lo_ref[...].astype(BF16).astype(F32), U32)
    o_ref[...] = hi | (lo >> 16)


def _pack_table(tab):
    rows = 512
    nb = HALF_EXPERTS // rows
    packed = pl.pallas_call(
        _pack_kernel,
        grid=(nb,),
        in_specs=[pl.BlockSpec((rows, D_MODEL), lambda i: (i, 0)),
                  pl.BlockSpec((rows, D_MODEL), lambda i: (i + nb, 0))],
        out_specs=pl.BlockSpec((rows, D_MODEL), lambda i: (i, 0)),
        out_shape=jax.ShapeDtypeStruct((HALF_EXPERTS, D_MODEL), U32),
        compiler_params=pltpu.CompilerParams(dimension_semantics=("arbitrary",)),
        name="pack_table",
    )(tab, tab)
    return packed.reshape(HALF_EXPERTS * ROW_TILES, LANES)


def _expert_row(tab_ref, off, shv):
    word = tab_ref[pl.ds(pl.multiple_of(off, SUBLANES), SUBLANES), :]
    return pltpu.bitcast(jnp.left_shift(word, shv) & jnp.uint32(0xFFFF0000), F32)


def _rows_to_lanes(row):
    return jnp.transpose(jnp.broadcast_to(row, (PEER_PAIRS, LANES)))


def _shift_rows(shf_row):
    bits = pltpu.bitcast(_rows_to_lanes(shf_row), U32)
    return bits >> 26


def _fold8(prods):
    sub = lax.broadcasted_iota(I32, (SUBLANES, LANES), 0)
    cur = prods
    for sh in (1, 2, 4):
        keep = (sub & sh) == 0
        nxt = []
        for k in range(0, len(cur), 2):
            a = jnp.where(keep, cur[k], cur[k + 1])
            b = jnp.where(keep, cur[k + 1], cur[k])
            nxt.append(a + pltpu.roll(b, sh, axis=0))
        cur = nxt
    return cur[0]


def _peer_act_kernel(off_ref, shf_ref, x_ref, gate_ref, tab_ref, w_ref, shb_ref, part_ref, a_ref):
    def prep(t, _):
        r0 = pl.multiple_of(t * PEER_PAIRS, PEER_PAIRS)
        shb_ref[pl.ds(r0, PEER_PAIRS), :] = _shift_rows(shf_ref[pl.ds(t, 1), :])
        return 0

    lax.fori_loop(0, PEER_TB, prep, 0, unroll=PREP_UNROLL)

    groups_per_token = PEER_PAIRS // SUBLANES

    def group(gi, _):
        xt = x_ref[lax.shift_right_logical(gi, groups_per_token.bit_length() - 1)]
        j0 = pl.multiple_of(gi * SUBLANES, SUBLANES)
        offs = off_ref.at[pl.ds(j0, SUBLANES)]
        prods = []
        for jj in range(SUBLANES):
            f = _expert_row(tab_ref, offs[jj], shb_ref[pl.ds(j0 + jj, 1), :])
            prods.append(f * xt)
        part_ref[pl.ds(j0, SUBLANES), :] = _fold8(prods)
        return 0

    lax.fori_loop(0, PEER_TB * groups_per_token, group, 0, unroll=4)

    def lane_sums(t, _):
        r0 = pl.multiple_of(t * PEER_PAIRS, PEER_PAIRS)
        part_t = jnp.transpose(part_ref[pl.ds(r0, PEER_PAIRS), :])
        a_ref[pl.ds(t, 1), :] = jnp.sum(part_t, axis=0, keepdims=True)
        return 0

    lax.fori_loop(0, PEER_TB, lane_sums, 0, unroll=PREP_UNROLL)
    a = a_ref[...]
    act = 0.5 * a * (1.0 + lax.erf(a * (2.0 ** -0.5)))
    w_ref[...] = gate_ref[...] * act


def _peer_act(off, shf, x3, gate, tab):
    t = shf.shape[0]
    return pl.pallas_call(
        _peer_act_kernel,
        grid=(t // PEER_TB,),
        in_specs=[
            pl.BlockSpec((PEER_TB * PEER_PAIRS,), lambda i: (i,), memory_space=pltpu.SMEM),
            pl.BlockSpec((PEER_TB, PEER_PAIRS), lambda i: (i, 0)),
            pl.BlockSpec((PEER_TB, ROW_TILES, LANES), lambda i: (i, 0, 0)),
            pl.BlockSpec((PEER_TB, PEER_PAIRS), lambda i: (i, 0)),
            pl.BlockSpec((HALF_EXPERTS * ROW_TILES, LANES), lambda i: (0, 0),
                         pipeline_mode=pl.Buffered(1)),
        ],
        out_specs=pl.BlockSpec((PEER_TB, PEER_PAIRS), lambda i: (i, 0)),
        out_shape=jax.ShapeDtypeStruct((t, PEER_PAIRS), F32),
        scratch_shapes=[pltpu.VMEM((PEER_TB * PEER_PAIRS, LANES), U32),
                        pltpu.VMEM((PEER_TB * PEER_PAIRS, LANES), F32),
                        pltpu.VMEM((PEER_TB, PEER_PAIRS), F32)],
        compiler_params=pltpu.CompilerParams(
            dimension_semantics=("arbitrary",), vmem_limit_bytes=VMEM_TABLE_LIMIT),
        name="peer_act",
    )(off, shf, x3, gate, tab)


def _peer_out_kernel(off_ref, shf_ref, w_ref, h_ref, g_ref, tab_ref, o_ref, shb_ref, wb_ref):
    n_acc = 4

    def prep(t, _):
        r0 = pl.multiple_of(t * PEER_PAIRS, PEER_PAIRS)
        shb_ref[pl.ds(r0, PEER_PAIRS), :] = _shift_rows(shf_ref[pl.ds(t, 1), :])
        wb_ref[pl.ds(r0, PEER_PAIRS), :] = _rows_to_lanes(w_ref[pl.ds(t, 1), :])
        return 0

    lax.fori_loop(0, PEER_TB, prep, 0, unroll=PREP_UNROLL)

    def token(t, _):
        def group(g, accs):
            j0 = pl.multiple_of(t * PEER_PAIRS + g * PAIR_GROUP, PAIR_GROUP)
            accs = list(accs)
            offs = off_ref.at[pl.ds(j0, PAIR_GROUP)]
            for jj in range(PAIR_GROUP):
                f = _expert_row(tab_ref, offs[jj], shb_ref[pl.ds(j0 + jj, 1), :])
                accs[jj % n_acc] = accs[jj % n_acc] + wb_ref[pl.ds(j0 + jj, 1), :] * f
            return tuple(accs)

        zero = jnp.zeros((SUBLANES, LANES), F32)
        accs = lax.fori_loop(0, PEER_PAIRS // PAIR_GROUP, group, (zero,) * n_acc)
        o_ref[t] = h_ref[t] + ((accs[0] + accs[1]) + (accs[2] + accs[3]))
        return 0

    lax.fori_loop(0, PEER_TB, token, 0)
    y = o_ref[...]
    ms = jnp.sum(jnp.sum(y * y, axis=2, keepdims=True), axis=1, keepdims=True) * (1.0 / D_MODEL)
    o_ref[...] = y * lax.rsqrt(ms + EPS) * g_ref[...]


def _peer_out(off, shf, w, h3, g3, tab):
    t = shf.shape[0]
    row = lambda: pl.BlockSpec((PEER_TB, PEER_PAIRS), lambda i: (i, 0))
    return pl.pallas_call(
        _peer_out_kernel,
        grid=(t // PEER_TB,),
        in_specs=[
            pl.BlockSpec((PEER_TB * PEER_PAIRS,), lambda i: (i,), memory_space=pltpu.SMEM),
            row(), row(),
            pl.BlockSpec((PEER_TB, ROW_TILES, LANES), lambda i: (i, 0, 0)),
            pl.BlockSpec((ROW_TILES, LANES), lambda i: (0, 0)),
            pl.BlockSpec((HALF_EXPERTS * ROW_TILES, LANES), lambda i: (0, 0),
                         pipeline_mode=pl.Buffered(1)),
        ],
        out_specs=pl.BlockSpec((PEER_TB, ROW_TILES, LANES), lambda i: (i, 0, 0)),
        out_shape=jax.ShapeDtypeStruct((t, ROW_TILES, LANES), F32),
        scratch_shapes=[pltpu.VMEM((PEER_TB * PEER_PAIRS, LANES), U32),
                        pltpu.VMEM((PEER_TB * PEER_PAIRS, LANES), F32)],
        compiler_params=pltpu.CompilerParams(
            dimension_semantics=("arbitrary",), vmem_limit_bytes=VMEM_TABLE_LIMIT),
        name="peer_out",
    )(off, shf, w, h3, g3, tab)


def kernel(x, meta_tokens, norm1_g, w_in, lambda_q1, lambda_k1, lambda_q2, lambda_k2,
           attn_subln_g, conv_w, conv_b, conv_norm_g, conv_norm_b, w_out, norm2_g,
           peer_wq, peer_subkeys, peer_u, peer_v, final_norm_g):
    b, s, _ = x.shape
    seq = N_META + s
    lp = ((seq + Q_BLOCK - 1) // Q_BLOCK) * Q_BLOCK
    t = b * lp
    assert lp % TQ == 0 and lp % CONV_ROWS == 0 and t % TM == 0 and t % PEER_TB == 0

    meta = jnp.broadcast_to(meta_tokens[None].astype(x.dtype), (b, N_META, D_MODEL))
    h = jnp.concatenate([meta, x, jnp.zeros((b, lp - seq, D_MODEL), x.dtype)], axis=1)
    h2d = h.reshape(t, D_MODEL)

    lam_init = 0.8 - 0.6 * math.exp(-0.3 * 0)
    slopes = jnp.asarray([2.0 ** (-8.0 * (i + 1) / N_HEADS) for i in range(N_HEADS)], F32)
    group = jnp.arange(CONV_WIDTH) // CONV_GROUP
    gavg = (group[:, None] == group[None, :]).astype(F32) * (1.0 / CONV_GROUP)

    q, k, v, u = _inproj(h2d, norm1_g[0][None], w_in[0].astype(BF16))
    attn = _attention(q.reshape(b, lp, -1), k.reshape(b, lp, -1), v.reshape(b, lp, -1), slopes,
                      lambda_q1[0][None], lambda_k1[0][None], lambda_q2[0][None],
                      lambda_k2[0][None], attn_subln_g[0][None], lam_init)
    conv = _conformer_conv(u.reshape(b, lp, -1), conv_w[0], conv_b[0][None], gavg,
                           conv_norm_g[0][None], conv_norm_b[0][None])
    h1, xn2, qp = _outproj(attn.reshape(t, -1), conv.reshape(t, -1), h2d,
                           w_out[0].astype(BF16), norm2_g[0][None], peer_wq[0].astype(BF16))
    keys = peer_subkeys[0].reshape(2 * PEER_HEADS, PEER_KEYS, PEER_KEYS).astype(BF16)
    off, shf, gate = _peer_topk(qp, keys)
    off = off.reshape(t * PEER_PAIRS)
    w = _peer_act(off, shf, xn2.reshape(t, ROW_TILES, LANES), gate, _pack_table(peer_u[0]))
    out = _peer_out(off, shf, w, h1.reshape(t, ROW_TILES, LANES),
                    final_norm_g.reshape(ROW_TILES, LANES), _pack_table(peer_v[0]))
    return out.reshape(b, lp, D_MODEL)[:, N_META:N_META + s]
```

```python
import functools
import math

import jax
import jax.numpy as jnp
from jax import lax
from jax.experimental import pallas as pl
from jax.experimental.pallas import tpu as pltpu

F32 = jnp.float32
BF16 = jnp.bfloat16
I32 = jnp.int32
U32 = jnp.uint32

D_MODEL = 1024
N_META = 16
Q_BLOCK = 128
ATTN_WIDTH = 512
CONV_WIDTH = 512
N_HEADS = 4
HEAD_DIM = 64
V_DIM = 128
CONV_K = 31
CONV_GROUP = 64
PEER_HEADS = 8
PEER_KEYS = 128
PEER_TOPK = 16
PEER_PAIRS = PEER_HEADS * PEER_TOPK
N_EXPERTS = PEER_KEYS * PEER_KEYS
HALF_EXPERTS = N_EXPERTS // 2
EPS = 1e-6
NEG = -1e30

LANES = 128
SUBLANES = 8
ROW_TILES = D_MODEL // LANES

TM = 512
TQ = 384
CONV_ROWS = 64
CONV_PAD = 32
TOPK_TOKENS = 128
PEER_TB = 64
PAIR_GROUP = 64
PREP_UNROLL = 8
VMEM_TABLE_LIMIT = 52 * 1024 * 1024


def _inproj_kernel(h_ref, g_ref, w_ref, q_ref, k_ref, v_ref, u_ref):
    x = h_ref[...]
    ms = jnp.mean(x * x, axis=-1, keepdims=True)
    xn = (x * lax.rsqrt(ms + EPS) * g_ref[...]).astype(BF16)
    proj = jnp.dot(xn, w_ref[...], preferred_element_type=F32)
    q_ref[...] = (proj[:, 0:ATTN_WIDTH] * (HEAD_DIM ** -0.5)).astype(BF16)
    k_ref[...] = proj[:, ATTN_WIDTH:2 * ATTN_WIDTH].astype(BF16)
    v_ref[...] = proj[:, 2 * ATTN_WIDTH:3 * ATTN_WIDTH].astype(BF16)
    ga = proj[:, 3 * ATTN_WIDTH:3 * ATTN_WIDTH + CONV_WIDTH]
    gg = proj[:, 3 * ATTN_WIDTH + CONV_WIDTH:]
    u_ref[...] = ga * jax.nn.sigmoid(gg)


def _inproj(h2d, g, w_bf16):
    t = h2d.shape[0]
    n_cols = w_bf16.shape[1]
    return pl.pallas_call(
        _inproj_kernel,
        grid=(t // TM,),
        in_specs=[
            pl.BlockSpec((TM, D_MODEL), lambda i: (i, 0)),
            pl.BlockSpec((1, D_MODEL), lambda i: (0, 0)),
            pl.BlockSpec((D_MODEL, n_cols), lambda i: (0, 0)),
        ],
        out_specs=[
            pl.BlockSpec((TM, ATTN_WIDTH), lambda i: (i, 0)),
            pl.BlockSpec((TM, ATTN_WIDTH), lambda i: (i, 0)),
            pl.BlockSpec((TM, ATTN_WIDTH), lambda i: (i, 0)),
            pl.BlockSpec((TM, CONV_WIDTH), lambda i: (i, 0)),
        ],
        out_shape=[
            jax.ShapeDtypeStruct((t, ATTN_WIDTH), BF16),
            jax.ShapeDtypeStruct((t, ATTN_WIDTH), BF16),
            jax.ShapeDtypeStruct((t, ATTN_WIDTH), BF16),
            jax.ShapeDtypeStruct((t, CONV_WIDTH), F32),
        ],
        compiler_params=pltpu.CompilerParams(
            dimension_semantics=("arbitrary",), vmem_limit_bytes=48 * 1024 * 1024),
        name="inproj",
    )(h2d, g, w_bf16)


def _attn_kernel(slopes_ref, lq1_ref, lk1_ref, lq2_ref, lk2_ref, subg_ref,
                 q_ref, k_ref, v_ref, o_ref, *, lam_init):
    hd = pl.program_id(1)
    qi = pl.program_id(2)
    slope = slopes_ref[hd]
    lam = (jnp.exp(jnp.sum(lq1_ref[...] * lk1_ref[...], keepdims=True))
           - jnp.exp(jnp.sum(lq2_ref[...] * lk2_ref[...], keepdims=True)) + lam_init)

    q = q_ref[0]
    lane = lax.broadcasted_iota(I32, q.shape, 1)
    zero = jnp.zeros_like(q)
    qs = jnp.concatenate([jnp.where(lane < HEAD_DIM, q, zero),
                          jnp.where(lane >= HEAD_DIM, q, zero)], axis=0)

    q0 = qi * TQ
    col = lax.broadcasted_iota(I32, (1, TQ), 1)

    def step(j, carry, masked):
        m, l, acc = carry
        k0 = pl.multiple_of(j * TQ, TQ)
        kj = k_ref[0, pl.ds(k0, TQ), :]
        vj = v_ref[0, pl.ds(k0, TQ), :]
        s = lax.dot_general(qs, kj, (((1,), (1,)), ((), ())), preferred_element_type=F32)
        s = s + slope * (col + (k0 - q0)).astype(F32)
        if masked:
            row = lax.broadcasted_iota(I32, (2 * TQ, TQ), 0)
            row = jnp.where(row >= TQ, row - TQ, row)
            cc = lax.broadcasted_iota(I32, (2 * TQ, TQ), 1)
            s = jnp.where(cc <= row, s, NEG)
        m_new = jnp.maximum(m, jnp.max(s, axis=1, keepdims=True))
        alpha = jnp.exp(m - m_new)
        p = jnp.exp(s - m_new)
        l = alpha * l + jnp.sum(p, axis=1, keepdims=True)
        acc = alpha * acc + jnp.dot(p.astype(BF16), vj, preferred_element_type=F32)
        return m_new, l, acc

    init = (jnp.full((2 * TQ, 1), NEG, F32), jnp.zeros((2 * TQ, 1), F32),
            jnp.zeros((2 * TQ, V_DIM), F32))
    carry = lax.fori_loop(0, qi, lambda j, c: step(j, c, False), init)
    m, l, acc = step(qi, carry, True)
    o = acc / l
    a = o[:TQ] - lam * o[TQ:]
    ms = jnp.mean(a * a, axis=-1, keepdims=True)
    y = a * lax.rsqrt(ms + EPS) * subg_ref[...] * (1.0 - lam_init)
    o_ref[0] = y.astype(BF16)


def _attention(q, k, v, slopes, lq1, lk1, lq2, lk2, subg, lam_init):
    b, lp, _ = q.shape
    nq = lp // TQ
    vec = lambda n: pl.BlockSpec((1, n), lambda bi, hi, i: (0, 0))
    return pl.pallas_call(
        functools.partial(_attn_kernel, lam_init=lam_init),
        grid=(b, N_HEADS, nq),
        in_specs=[
            pl.BlockSpec(memory_space=pltpu.SMEM),
            vec(HEAD_DIM), vec(HEAD_DIM), vec(HEAD_DIM), vec(HEAD_DIM), vec(V_DIM),
            pl.BlockSpec((1, TQ, V_DIM), lambda bi, hi, i: (bi, i, hi)),
            pl.BlockSpec((1, lp, V_DIM), lambda bi, hi, i: (bi, 0, hi)),
            pl.BlockSpec((1, lp, V_DIM), lambda bi, hi, i: (bi, 0, hi)),
        ],
        out_specs=pl.BlockSpec((1, TQ, V_DIM), lambda bi, hi, i: (bi, i, hi)),
        out_shape=jax.ShapeDtypeStruct((b, lp, ATTN_WIDTH), BF16),
        compiler_params=pltpu.CompilerParams(
            dimension_semantics=("arbitrary", "arbitrary", "arbitrary"),
            vmem_limit_bytes=48 * 1024 * 1024),
        name="diff_attn",
    )(slopes, lq1, lk1, lq2, lk2, subg, q, k, v)


def _conv_kernel(u_ref, w_ref, b_ref, gavg_ref, g_ref, beta_ref, o_ref, upad_ref, y_ref):
    lp = u_ref.shape[1]
    upad_ref[0:CONV_PAD, :] = jnp.zeros((CONV_PAD, CONV_WIDTH), F32)
    upad_ref[CONV_PAD:, :] = u_ref[0]

    def chunk(c, _):
        base = pl.multiple_of(c * CONV_ROWS, CONV_ROWS)
        for lb in range(CONV_WIDTH // LANES):
            ls = slice(lb * LANES, (lb + 1) * LANES)
            acc = jnp.zeros((CONV_ROWS, LANES), F32) + b_ref[:, ls]
            win = upad_ref[pl.ds(base, CONV_ROWS + CONV_PAD), ls]
            for t in range(CONV_K):
                off = CONV_PAD - (CONV_K - 1) + t
                acc = acc + w_ref[t:t + 1, ls] * win[off:off + CONV_ROWS, :]
            y_ref[:, ls] = acc
        y = y_ref[...]
        mu = jnp.dot(y, gavg_ref[...], precision=lax.Precision.HIGHEST,
                     preferred_element_type=F32)
        d = y - mu
        var = jnp.dot(d * d, gavg_ref[...], precision=lax.Precision.HIGHEST,
                      preferred_element_type=F32)
        yn = d * lax.rsqrt(var + EPS) * g_ref[...] + beta_ref[...]
        o_ref[0, pl.ds(base, CONV_ROWS), :] = (yn * jax.nn.sigmoid(yn)).astype(BF16)
        return 0

    lax.fori_loop(0, lp // CONV_ROWS, chunk, 0)


def _conformer_conv(u, conv_w, conv_b, gavg, gn_g, gn_b):
    b, lp, c = u.shape
    full = lambda shape: pl.BlockSpec(shape, lambda bi: (0,) * len(shape))
    return pl.pallas_call(
        _conv_kernel,
        grid=(b,),
        in_specs=[
            pl.BlockSpec((1, lp, c), lambda bi: (bi, 0, 0)),
            full((CONV_K, c)), full((1, c)), full((c, c)), full((1, c)), full((1, c)),
        ],
        out_specs=pl.BlockSpec((1, lp, c), lambda bi: (bi, 0, 0)),
        out_shape=jax.ShapeDtypeStruct((b, lp, c), BF16),
        scratch_shapes=[pltpu.VMEM((lp + CONV_PAD, c), F32), pltpu.VMEM((CONV_ROWS, c), F32)],
        compiler_params=pltpu.CompilerParams(
            dimension_semantics=("arbitrary",), vmem_limit_bytes=56 * 1024 * 1024),
        name="conformer_conv",
    )(u, conv_w, conv_b, gavg, gn_g, gn_b)


def _outproj_kernel(a_ref, c_ref, h_ref, wo_ref, g_ref, wq_ref, h1_ref, xn_ref, qp_ref):
    mix = (jnp.dot(a_ref[...], wo_ref[0:ATTN_WIDTH, :], preferred_element_type=F32)
           + jnp.dot(c_ref[...], wo_ref[ATTN_WIDTH:, :], preferred_element_type=F32))
    h1 = h_ref[...] + mix
    h1_ref[...] = h1
    ms = jnp.mean(h1 * h1, axis=-1, keepdims=True)
    xn = h1 * lax.rsqrt(ms + EPS) * g_ref[...]
    xn_ref[...] = xn
    qp = jnp.dot(xn.astype(BF16), wq_ref[...], preferred_element_type=F32)
    for hp in range(2 * PEER_HEADS):
        qp_ref[hp] = qp[:, hp * PEER_KEYS:(hp + 1) * PEER_KEYS].astype(BF16)


def _outproj(attn2d, conv2d, h2d, wo_bf16, g2, wq_bf16):
    t = h2d.shape[0]
    nq = wq_bf16.shape[1]
    return pl.pallas_call(
        _outproj_kernel,
        grid=(t // TM,),
        in_specs=[
            pl.BlockSpec((TM, ATTN_WIDTH), lambda i: (i, 0)),
            pl.BlockSpec((TM, CONV_WIDTH), lambda i: (i, 0)),
            pl.BlockSpec((TM, D_MODEL), lambda i: (i, 0)),
            pl.BlockSpec((D_MODEL, D_MODEL), lambda i: (0, 0)),
            pl.BlockSpec((1, D_MODEL), lambda i: (0, 0)),
            pl.BlockSpec((D_MODEL, nq), lambda i: (0, 0)),
        ],
        out_specs=[
            pl.BlockSpec((TM, D_MODEL), lambda i: (i, 0)),
            pl.BlockSpec((TM, D_MODEL), lambda i: (i, 0)),
            pl.BlockSpec((2 * PEER_HEADS, TM, PEER_KEYS), lambda i: (0, i, 0)),
        ],
        out_shape=[
            jax.ShapeDtypeStruct((t, D_MODEL), F32),
            jax.ShapeDtypeStruct((t, D_MODEL), F32),
            jax.ShapeDtypeStruct((2 * PEER_HEADS, t, PEER_KEYS), BF16),
        ],
        compiler_params=pltpu.CompilerParams(
            dimension_semantics=("arbitrary",), vmem_limit_bytes=48 * 1024 * 1024),
        name="outproj_peerq",
    )(attn2d, conv2d, h2d, wo_bf16, g2, wq_bf16)


def _extract_top(s, ids, val_ref, idx_ref):
    for r in range(PEER_TOPK):
        m = jnp.max(s, axis=0, keepdims=True)
        idx = jnp.min(jnp.where(s == m, ids, PEER_TOPK * PEER_TOPK), axis=0, keepdims=True)
        val_ref[r:r + 1, :] = m
        idx_ref[r:r + 1, :] = idx
        s = jnp.where(ids == idx, NEG, s)


def _candidates(s1, s2):
    tokens = s1.shape[1]
    sub = lax.broadcasted_iota(I32, (SUBLANES, tokens), 0)
    vals = [s1[0:1, :] + s2]
    ids = [sub, sub + SUBLANES]
    for a in range(1, SUBLANES):
        vals.append(s1[a:a + 1, :] + s2[0:SUBLANES, :])
        ids.append(sub + a * PEER_TOPK)
    vals.append(s1[SUBLANES:, :] + s2[0:1, :])
    ids.append((sub + SUBLANES) * PEER_TOPK)
    return jnp.concatenate(vals, axis=0), jnp.concatenate(ids, axis=0)


def _topk_kernel(qp_ref, keys_ref, r_ref, sh_ref, gate_ref,
                 s1_ref, i1_ref, s2_ref, i2_ref, ct_ref, ci_ref, e_ref, g_ref):
    def head(h, _):
        for p, (sv, si) in enumerate(((s1_ref, i1_ref), (s2_ref, i2_ref))):
            hp = 2 * h + p
            st = lax.dot_general(keys_ref[hp], qp_ref[hp], (((1,), (1,)), ((), ())),
                                 preferred_element_type=F32)
            _extract_top(st, lax.broadcasted_iota(I32, st.shape, 0), sv, si)
        cand, cand_ids = _candidates(s1_ref[...], s2_ref[...])
        _extract_top(cand, cand_ids, ct_ref, ci_ref)
        ct = ct_ref[...]
        ci = ci_ref[...]
        hi = ci >> 4
        lo = ci & (PEER_TOPK - 1)
        i1 = i1_ref[...]
        i2 = i2_ref[...]
        e1 = jnp.zeros_like(ci)
        e2 = jnp.zeros_like(ci)
        for a in range(PEER_TOPK):
            e1 = jnp.where(hi == a, i1[a:a + 1, :], e1)
            e2 = jnp.where(lo == a, i2[a:a + 1, :], e2)
        e = e1 * PEER_KEYS + e2
        ex = jnp.exp(ct - jnp.max(ct, axis=0, keepdims=True))
        gate = ex / jnp.sum(ex, axis=0, keepdims=True)
        row0 = pl.multiple_of(h * PEER_TOPK, PEER_TOPK)
        e_ref[pl.ds(row0, PEER_TOPK), :] = e
        g_ref[pl.ds(row0, PEER_TOPK), :] = gate
        return 0

    lax.fori_loop(0, PEER_HEADS, head, 0)
    e = e_ref[...].T
    r_ref[...] = (e & (HALF_EXPERTS - 1)) * ROW_TILES
    sh_ref[...] = ((e >> 13) << 4).astype(F32)
    gate_ref[...] = g_ref[...].T


def _peer_topk(qp, keys_bf16):
    t = qp.shape[1]
    tt = TOPK_TOKENS
    sc = lambda dt: pltpu.VMEM((PEER_TOPK, tt), dt)
    return pl.pallas_call(
        _topk_kernel,
        grid=(t // tt,),
        in_specs=[
            pl.BlockSpec((2 * PEER_HEADS, tt, PEER_KEYS), lambda i: (0, i, 0)),
            pl.BlockSpec((2 * PEER_HEADS, PEER_KEYS, PEER_KEYS), lambda i: (0, 0, 0)),
        ],
        out_specs=[
            pl.BlockSpec((tt, PEER_PAIRS), lambda i: (i, 0)),
            pl.BlockSpec((tt, PEER_PAIRS), lambda i: (i, 0)),
            pl.BlockSpec((tt, PEER_PAIRS), lambda i: (i, 0)),
        ],
        out_shape=[
            jax.ShapeDtypeStruct((t, PEER_PAIRS), I32),
            jax.ShapeDtypeStruct((t, PEER_PAIRS), F32),
            jax.ShapeDtypeStruct((t, PEER_PAIRS), F32),
        ],
        scratch_shapes=[sc(F32), sc(I32), sc(F32), sc(I32), sc(F32), sc(I32),
                        pltpu.VMEM((PEER_PAIRS, tt), I32), pltpu.VMEM((PEER_PAIRS, tt), F32)],
        compiler_params=pltpu.CompilerParams(dimension_semantics=("arbitrary",)),
        name="peer_topk",
    )(qp, keys_bf16)


def _pack_kernel(hi_ref, lo_ref, o_ref):
    hi = pltpu.bitcast(hi_ref[...].astype(BF16).astype(F32), U32)
    lo = pltpu.bitcast(lo_ref[...].astype(BF16).astype(F32), U32)
    o_ref[...] = hi | (lo >> 16)


def _pack_table(tab):
    rows = 512
    nb = HALF_EXPERTS // rows
    packed = pl.pallas_call(
        _pack_kernel,
        grid=(nb,),
        in_specs=[pl.BlockSpec((rows, D_MODEL), lambda i: (i, 0)),
                  pl.BlockSpec((rows, D_MODEL), lambda i: (i + nb, 0))],
        out_specs=pl.BlockSpec((rows, D_MODEL), lambda i: (i, 0)),
        out_shape=jax.ShapeDtypeStruct((HALF_EXPERTS, D_MODEL), U32),
        compiler_params=pltpu.CompilerParams(dimension_semantics=("arbitrary",)),
        name="pack_table",
    )(tab, tab)
    return packed.reshape(HALF_EXPERTS * ROW_TILES, LANES)


def _expert_row(tab_ref, off, shv):
    word = tab_ref[pl.ds(pl.multiple_of(off, SUBLANES), SUBLANES), :]
    return pltpu.bitcast(jnp.left_shift(word, shv) & jnp.uint32(0xFFFF0000), F32)


def _rows_to_lanes(row):
    return jnp.transpose(jnp.broadcast_to(row, (PEER_PAIRS, LANES)))


def _shift_rows(shf_row):
    ri = lax.broadcasted_iota(I32, (PEER_PAIRS, PEER_PAIRS), 0)
    ci = lax.broadcasted_iota(I32, (PEER_PAIRS, PEER_PAIRS), 1)
    diag = jnp.where(ri == ci, shf_row, 0.0).astype(BF16)
    rep = jnp.dot(diag, jnp.ones((PEER_PAIRS, LANES), BF16), preferred_element_type=F32)
    return pltpu.bitcast(rep, U32) >> 26


def _fold8(prods):
    sub = lax.broadcasted_iota(I32, (SUBLANES, LANES), 0)
    cur = prods
    for sh in (1, 2, 4):
        keep = (sub & sh) == 0
        nxt = []
        for k in range(0, len(cur), 2):
            a = jnp.where(keep, cur[k], cur[k + 1])
            b = jnp.where(keep, cur[k + 1], cur[k])
            nxt.append(a + pltpu.roll(b, sh, axis=0))
        cur = nxt
    return cur[0]


def _peer_act_kernel(off_ref, shf_ref, x_ref, gate_ref, tab_ref, w_ref, shb_ref, part_ref, a_ref):
    def prep(t, _):
        r0 = pl.multiple_of(t * PEER_PAIRS, PEER_PAIRS)
        shb_ref[pl.ds(r0, PEER_PAIRS), :] = _shift_rows(shf_ref[pl.ds(t, 1), :])
        return 0

    lax.fori_loop(0, PEER_TB, prep, 0, unroll=PREP_UNROLL)

    groups_per_token = PEER_PAIRS // SUBLANES

    def group(gi, _):
        xt = x_ref[lax.shift_right_logical(gi, groups_per_token.bit_length() - 1)]
        j0 = pl.multiple_of(gi * SUBLANES, SUBLANES)
        offs = off_ref.at[pl.ds(j0, SUBLANES)]
        prods = []
        for jj in range(SUBLANES):
            f = _expert_row(tab_ref, offs[jj], shb_ref[pl.ds(j0 + jj, 1), :])
            prods.append(f * xt)
        part_ref[pl.ds(j0, SUBLANES), :] = _fold8(prods)
        return 0

    lax.fori_loop(0, PEER_TB * groups_per_token, group, 0, unroll=8)

    def lane_sums(t, _):
        r0 = pl.multiple_of(t * PEER_PAIRS, PEER_PAIRS)
        part_t = jnp.transpose(part_ref[pl.ds(r0, PEER_PAIRS), :])
        a_ref[pl.ds(t, 1), :] = jnp.sum(part_t, axis=0, keepdims=True)
        return 0

    lax.fori_loop(0, PEER_TB, lane_sums, 0, unroll=PREP_UNROLL)
    a = a_ref[...]
    act = 0.5 * a * (1.0 + lax.erf(a * (2.0 ** -0.5)))
    w_ref[...] = gate_ref[...] * act


def _peer_act(off, shf, x3, gate, tab):
    t = shf.shape[0]
    return pl.pallas_call(
        _peer_act_kernel,
        grid=(t // PEER_TB,),
        in_specs=[
            pl.BlockSpec((PEER_TB * PEER_PAIRS,), lambda i: (i,), memory_space=pltpu.SMEM),
            pl.BlockSpec((PEER_TB, PEER_PAIRS), lambda i: (i, 0)),
            pl.BlockSpec((PEER_TB, ROW_TILES, LANES), lambda i: (i, 0, 0)),
            pl.BlockSpec((PEER_TB, PEER_PAIRS), lambda i: (i, 0)),
            pl.BlockSpec((HALF_EXPERTS * ROW_TILES, LANES), lambda i: (0, 0),
                         pipeline_mode=pl.Buffered(1)),
        ],
        out_specs=pl.BlockSpec((PEER_TB, PEER_PAIRS), lambda i: (i, 0)),
        out_shape=jax.ShapeDtypeStruct((t, PEER_PAIRS), F32),
        scratch_shapes=[pltpu.VMEM((PEER_TB * PEER_PAIRS, LANES), U32),
                        pltpu.VMEM((PEER_TB * PEER_PAIRS, LANES), F32),
                        pltpu.VMEM((PEER_TB, PEER_PAIRS), F32)],
        compiler_params=pltpu.CompilerParams(
            dimension_semantics=("arbitrary",), vmem_limit_bytes=VMEM_TABLE_LIMIT),
        name="peer_act",
    )(off, shf, x3, gate, tab)


def _peer_out_kernel(off_ref, shf_ref, w_ref, h_ref, g_ref, tab_ref, o_ref, shb_ref, wb_ref):
    n_acc = 4

    def prep(t, _):
        r0 = pl.multiple_of(t * PEER_PAIRS, PEER_PAIRS)
        shb_ref[pl.ds(r0, PEER_PAIRS), :] = _shift_rows(shf_ref[pl.ds(t, 1), :])
        wb_ref[pl.ds(r0, PEER_PAIRS), :] = _rows_to_lanes(w_ref[pl.ds(t, 1), :])
        return 0

    lax.fori_loop(0, PEER_TB, prep, 0, unroll=PREP_UNROLL)

    def token(t, _):
        def group(g, accs):
            j0 = pl.multiple_of(t * PEER_PAIRS + g * PAIR_GROUP, PAIR_GROUP)
            accs = list(accs)
            offs = off_ref.at[pl.ds(j0, PAIR_GROUP)]
            for jj in range(PAIR_GROUP):
                f = _expert_row(tab_ref, offs[jj], shb_ref[pl.ds(j0 + jj, 1), :])
                accs[jj % n_acc] = accs[jj % n_acc] + wb_ref[pl.ds(j0 + jj, 1), :] * f
            return tuple(accs)

        zero = jnp.zeros((SUBLANES, LANES), F32)
        accs = lax.fori_loop(0, PEER_PAIRS // PAIR_GROUP, group, (zero,) * n_acc)
        o_ref[t] = h_ref[t] + ((accs[0] + accs[1]) + (accs[2] + accs[3]))
        return 0

    lax.fori_loop(0, PEER_TB, token, 0)
    y = o_ref[...]
    ms = jnp.sum(jnp.sum(y * y, axis=2, keepdims=True), axis=1, keepdims=True) * (1.0 / D_MODEL)
    o_ref[...] = y * lax.rsqrt(ms + EPS) * g_ref[...]


def _peer_out(off, shf, w, h3, g3, tab):
    t = shf.shape[0]
    row = lambda: pl.BlockSpec((PEER_TB, PEER_PAIRS), lambda i: (i, 0))
    return pl.pallas_call(
        _peer_out_kernel,
        grid=(t // PEER_TB,),
        in_specs=[
            pl.BlockSpec((PEER_TB * PEER_PAIRS,), lambda i: (i,), memory_space=pltpu.SMEM),
            row(), row(),
            pl.BlockSpec((PEER_TB, ROW_TILES, LANES), lambda i: (i, 0, 0)),
            pl.BlockSpec((ROW_TILES, LANES), lambda i: (0, 0)),
            pl.BlockSpec((HALF_EXPERTS * ROW_TILES, LANES), lambda i: (0, 0),
                         pipeline_mode=pl.Buffered(1)),
        ],
        out_specs=pl.BlockSpec((PEER_TB, ROW_TILES, LANES), lambda i: (i, 0, 0)),
        out_shape=jax.ShapeDtypeStruct((t, ROW_TILES, LANES), F32),
        scratch_shapes=[pltpu.VMEM((PEER_TB * PEER_PAIRS, LANES), U32),
                        pltpu.VMEM((PEER_TB * PEER_PAIRS, LANES), F32)],
        compiler_params=pltpu.CompilerParams(
            dimension_semantics=("arbitrary",), vmem_limit_bytes=VMEM_TABLE_LIMIT),
        name="peer_out",
    )(off, shf, w, h3, g3, tab)


def kernel(x, meta_tokens, norm1_g, w_in, lambda_q1, lambda_k1, lambda_q2, lambda_k2,
           attn_subln_g, conv_w, conv_b, conv_norm_g, conv_norm_b, w_out, norm2_g,
           peer_wq, peer_subkeys, peer_u, peer_v, final_norm_g):
    b, s, _ = x.shape
    seq = N_META + s
    lp = ((seq + Q_BLOCK - 1) // Q_BLOCK) * Q_BLOCK
    t = b * lp
    assert lp % TQ == 0 and lp % CONV_ROWS == 0 and t % TM == 0 and t % PEER_TB == 0

    meta = jnp.broadcast_to(meta_tokens[None].astype(x.dtype), (b, N_META, D_MODEL))
    h = jnp.concatenate([meta, x, jnp.zeros((b, lp - seq, D_MODEL), x.dtype)], axis=1)
    h2d = h.reshape(t, D_MODEL)

    lam_init = 0.8 - 0.6 * math.exp(-0.3 * 0)
    slopes = jnp.asarray([2.0 ** (-8.0 * (i + 1) / N_HEADS) for i in range(N_HEADS)], F32)
    group = jnp.arange(CONV_WIDTH) // CONV_GROUP
    gavg = (group[:, None] == group[None, :]).astype(F32) * (1.0 / CONV_GROUP)

    q, k, v, u = _inproj(h2d, norm1_g[0][None], w_in[0].astype(BF16))
    attn = _attention(q.reshape(b, lp, -1), k.reshape(b, lp, -1), v.reshape(b, lp, -1), slopes,
                      lambda_q1[0][None], lambda_k1[0][None], lambda_q2[0][None],
                      lambda_k2[0][None], attn_subln_g[0][None], lam_init)
    conv = _conformer_conv(u.reshape(b, lp, -1), conv_w[0], conv_b[0][None], gavg,
                           conv_norm_g[0][None], conv_norm_b[0][None])
    h1, xn2, qp = _outproj(attn.reshape(t, -1), conv.reshape(t, -1), h2d,
                           w_out[0].astype(BF16), norm2_g[0][None], peer_wq[0].astype(BF16))
    keys = peer_subkeys[0].reshape(2 * PEER_HEADS, PEER_KEYS, PEER_KEYS).astype(BF16)
    off, shf, gate = _peer_topk(qp, keys)
    off = off.reshape(t * PEER_PAIRS)
    w = _peer_act(off, shf, xn2.reshape(t, ROW_TILES, LANES), gate, _pack_table(peer_u[0]))
    out = _peer_out(off, shf, w, h1.reshape(t, ROW_TILES, LANES),
                    final_norm_g.reshape(ROW_TILES, LANES), _pack_table(peer_v[0]))
    return out.reshape(b, lp, D_MODEL)[:, N_META:N_META + s]
```

```python
import functools
import math

import jax
import jax.numpy as jnp
from jax import lax
from jax.experimental import pallas as pl
from jax.experimental.pallas import tpu as pltpu

F32 = jnp.float32
BF16 = jnp.bfloat16
I32 = jnp.int32
U32 = jnp.uint32

D_MODEL = 1024
N_META = 16
Q_BLOCK = 128
ATTN_WIDTH = 512
CONV_WIDTH = 512
N_HEADS = 4
HEAD_DIM = 64
V_DIM = 128
CONV_K = 31
CONV_GROUP = 64
PEER_HEADS = 8
PEER_KEYS = 128
PEER_TOPK = 16
PEER_PAIRS = PEER_HEADS * PEER_TOPK
N_EXPERTS = PEER_KEYS * PEER_KEYS
HALF_EXPERTS = N_EXPERTS // 2
EPS = 1e-6
NEG = -1e30

LANES = 128
SUBLANES = 8
ROW_TILES = D_MODEL // LANES

TM = 512
TQ = 384
CONV_ROWS = 64
CONV_PAD = 32
TOPK_TOKENS = 128
TOPK_HEADS_PER_TRIP = 2
PEER_TB = 64
PAIR_GROUP = 64
PREP_UNROLL = 8
VMEM_TABLE_LIMIT = 52 * 1024 * 1024


def _inproj_kernel(h_ref, g_ref, w_ref, q_ref, k_ref, v_ref, u_ref):
    x = h_ref[...]
    ms = jnp.mean(x * x, axis=-1, keepdims=True)
    xn = (x * lax.rsqrt(ms + EPS) * g_ref[...]).astype(BF16)
    proj = jnp.dot(xn, w_ref[...], preferred_element_type=F32)
    q_ref[...] = (proj[:, 0:ATTN_WIDTH] * (HEAD_DIM ** -0.5)).astype(BF16)
    k_ref[...] = proj[:, ATTN_WIDTH:2 * ATTN_WIDTH].astype(BF16)
    v_ref[...] = proj[:, 2 * ATTN_WIDTH:3 * ATTN_WIDTH].astype(BF16)
    ga = proj[:, 3 * ATTN_WIDTH:3 * ATTN_WIDTH + CONV_WIDTH]
    gg = proj[:, 3 * ATTN_WIDTH + CONV_WIDTH:]
    u_ref[...] = ga * jax.nn.sigmoid(gg)


def _inproj(h2d, g, w_bf16):
    t = h2d.shape[0]
    n_cols = w_bf16.shape[1]
    return pl.pallas_call(
        _inproj_kernel,
        grid=(t // TM,),
        in_specs=[
            pl.BlockSpec((TM, D_MODEL), lambda i: (i, 0)),
            pl.BlockSpec((1, D_MODEL), lambda i: (0, 0)),
            pl.BlockSpec((D_MODEL, n_cols), lambda i: (0, 0)),
        ],
        out_specs=[
            pl.BlockSpec((TM, ATTN_WIDTH), lambda i: (i, 0)),
            pl.BlockSpec((TM, ATTN_WIDTH), lambda i: (i, 0)),
            pl.BlockSpec((TM, ATTN_WIDTH), lambda i: (i, 0)),
            pl.BlockSpec((TM, CONV_WIDTH), lambda i: (i, 0)),
        ],
        out_shape=[
            jax.ShapeDtypeStruct((t, ATTN_WIDTH), BF16),
            jax.ShapeDtypeStruct((t, ATTN_WIDTH), BF16),
            jax.ShapeDtypeStruct((t, ATTN_WIDTH), BF16),
            jax.ShapeDtypeStruct((t, CONV_WIDTH), F32),
        ],
        compiler_params=pltpu.CompilerParams(
            dimension_semantics=("arbitrary",), vmem_limit_bytes=48 * 1024 * 1024),
        name="inproj",
    )(h2d, g, w_bf16)


def _attn_kernel(slopes_ref, lq1_ref, lk1_ref, lq2_ref, lk2_ref, subg_ref,
                 q_ref, k_ref, v_ref, o_ref, *, lam_init):
    hd = pl.program_id(1)
    qi = pl.program_id(2)
    slope = slopes_ref[hd]
    lam = (jnp.exp(jnp.sum(lq1_ref[...] * lk1_ref[...], keepdims=True))
           - jnp.exp(jnp.sum(lq2_ref[...] * lk2_ref[...], keepdims=True)) + lam_init)

    q = q_ref[0]
    lane = lax.broadcasted_iota(I32, q.shape, 1)
    zero = jnp.zeros_like(q)
    qs = jnp.concatenate([jnp.where(lane < HEAD_DIM, q, zero),
                          jnp.where(lane >= HEAD_DIM, q, zero)], axis=0)

    q0 = qi * TQ
    col = lax.broadcasted_iota(I32, (1, TQ), 1)

    def step(j, carry, masked):
        m, l, acc = carry
        k0 = pl.multiple_of(j * TQ, TQ)
        kj = k_ref[0, pl.ds(k0, TQ), :]
        vj = v_ref[0, pl.ds(k0, TQ), :]
        s = lax.dot_general(qs, kj, (((1,), (1,)), ((), ())), preferred_element_type=F32)
        s = s + slope * (col + (k0 - q0)).astype(F32)
        if masked:
            row = lax.broadcasted_iota(I32, (2 * TQ, TQ), 0)
            row = jnp.where(row >= TQ, row - TQ, row)
            cc = lax.broadcasted_iota(I32, (2 * TQ, TQ), 1)
            s = jnp.where(cc <= row, s, NEG)
        m_new = jnp.maximum(m, jnp.max(s, axis=1, keepdims=True))
        alpha = jnp.exp(m - m_new)
        p = jnp.exp(s - m_new)
        l = alpha * l + jnp.sum(p, axis=1, keepdims=True)
        acc = alpha * acc + jnp.dot(p.astype(BF16), vj, preferred_element_type=F32)
        return m_new, l, acc

    init = (jnp.full((2 * TQ, 1), NEG, F32), jnp.zeros((2 * TQ, 1), F32),
            jnp.zeros((2 * TQ, V_DIM), F32))
    carry = lax.fori_loop(0, qi, lambda j, c: step(j, c, False), init)
    m, l, acc = step(qi, carry, True)
    o = acc / l
    a = o[:TQ] - lam * o[TQ:]
    ms = jnp.mean(a * a, axis=-1, keepdims=True)
    y = a * lax.rsqrt(ms + EPS) * subg_ref[...] * (1.0 - lam_init)
    o_ref[0] = y.astype(BF16)


def _attention(q, k, v, slopes, lq1, lk1, lq2, lk2, subg, lam_init):
    b, lp, _ = q.shape
    nq = lp // TQ
    vec = lambda n: pl.BlockSpec((1, n), lambda bi, hi, i: (0, 0))
    return pl.pallas_call(
        functools.partial(_attn_kernel, lam_init=lam_init),
        grid=(b, N_HEADS, nq),
        in_specs=[
            pl.BlockSpec(memory_space=pltpu.SMEM),
            vec(HEAD_DIM), vec(HEAD_DIM), vec(HEAD_DIM), vec(HEAD_DIM), vec(V_DIM),
            pl.BlockSpec((1, TQ, V_DIM), lambda bi, hi, i: (bi, i, hi)),
            pl.BlockSpec((1, lp, V_DIM), lambda bi, hi, i: (bi, 0, hi)),
            pl.BlockSpec((1, lp, V_DIM), lambda bi, hi, i: (bi, 0, hi)),
        ],
        out_specs=pl.BlockSpec((1, TQ, V_DIM), lambda bi, hi, i: (bi, i, hi)),
        out_shape=jax.ShapeDtypeStruct((b, lp, ATTN_WIDTH), BF16),
        compiler_params=pltpu.CompilerParams(
            dimension_semantics=("arbitrary", "arbitrary", "arbitrary"),
            vmem_limit_bytes=48 * 1024 * 1024),
        name="diff_attn",
    )(slopes, lq1, lk1, lq2, lk2, subg, q, k, v)


def _group_mean(v, gavg_bf16):
    hi = v.astype(BF16)
    lo = (v - hi.astype(F32)).astype(BF16)
    return (jnp.dot(hi, gavg_bf16, preferred_element_type=F32)
            + jnp.dot(lo, gavg_bf16, preferred_element_type=F32))


def _conv_kernel(u_ref, w_ref, b_ref, gavg_ref, g_ref, beta_ref, o_ref, upad_ref, y_ref):
    lp = u_ref.shape[1]
    upad_ref[0:CONV_PAD, :] = jnp.zeros((CONV_PAD, CONV_WIDTH), F32)
    upad_ref[CONV_PAD:, :] = u_ref[0]

    def chunk(c, _):
        base = pl.multiple_of(c * CONV_ROWS, CONV_ROWS)
        for lb in range(CONV_WIDTH // LANES):
            ls = slice(lb * LANES, (lb + 1) * LANES)
            acc = jnp.zeros((CONV_ROWS, LANES), F32) + b_ref[:, ls]
            win = upad_ref[pl.ds(base, CONV_ROWS + CONV_PAD), ls]
            shifted = [win] + [jnp.roll(win, -rho, axis=0) for rho in range(1, SUBLANES)]
            for t in range(CONV_K):
                off = CONV_PAD - (CONV_K - 1) + t
                rho = off % SUBLANES
                acc = acc + w_ref[t:t + 1, ls] * shifted[rho][off - rho:off - rho + CONV_ROWS, :]
            y_ref[:, ls] = acc
        y = y_ref[...]
        mu = _group_mean(y, gavg_ref[...])
        d = y - mu
        var = _group_mean(d * d, gavg_ref[...])
        yn = d * lax.rsqrt(var + EPS) * g_ref[...] + beta_ref[...]
        o_ref[0, pl.ds(base, CONV_ROWS), :] = (yn * jax.nn.sigmoid(yn)).astype(BF16)
        return 0

    lax.fori_loop(0, lp // CONV_ROWS, chunk, 0)


def _conformer_conv(u, conv_w, conv_b, gavg, gn_g, gn_b):
    b, lp, c = u.shape
    full = lambda shape: pl.BlockSpec(shape, lambda bi: (0,) * len(shape))
    return pl.pallas_call(
        _conv_kernel,
        grid=(b,),
        in_specs=[
            pl.BlockSpec((1, lp, c), lambda bi: (bi, 0, 0)),
            full((CONV_K, c)), full((1, c)), full((c, c)), full((1, c)), full((1, c)),
        ],
        out_specs=pl.BlockSpec((1, lp, c), lambda bi: (bi, 0, 0)),
        out_shape=jax.ShapeDtypeStruct((b, lp, c), BF16),
        scratch_shapes=[pltpu.VMEM((lp + CONV_PAD, c), F32), pltpu.VMEM((CONV_ROWS, c), F32)],
        compiler_params=pltpu.CompilerParams(
            dimension_semantics=("arbitrary",), vmem_limit_bytes=56 * 1024 * 1024),
        name="conformer_conv",
    )(u, conv_w, conv_b, gavg, gn_g, gn_b)


def _store_row_tiles(ref, val):
    for k in range(ROW_TILES):
        ref[:, k, :] = val[:, k * LANES:(k + 1) * LANES]


def _outproj_kernel(a_ref, c_ref, h_ref, wo_ref, g_ref, wq_ref, h1_ref, xn_ref, qp_ref):
    mix = (jnp.dot(a_ref[...], wo_ref[0:ATTN_WIDTH, :], preferred_element_type=F32)
           + jnp.dot(c_ref[...], wo_ref[ATTN_WIDTH:, :], preferred_element_type=F32))
    h1 = h_ref[...] + mix
    h1_ref[...] = h1
    ms = jnp.mean(h1 * h1, axis=-1, keepdims=True)
    xn = h1 * lax.rsqrt(ms + EPS) * g_ref[...]
    _store_row_tiles(xn_ref, xn)
    qp = jnp.dot(xn.astype(BF16), wq_ref[...], preferred_element_type=F32)
    for hp in range(2 * PEER_HEADS):
        qp_ref[hp] = qp[:, hp * PEER_KEYS:(hp + 1) * PEER_KEYS].astype(BF16)


def _outproj(attn2d, conv2d, h2d, wo_bf16, g2, wq_bf16):
    t = h2d.shape[0]
    nq = wq_bf16.shape[1]
    return pl.pallas_call(
        _outproj_kernel,
        grid=(t // TM,),
        in_specs=[
            pl.BlockSpec((TM, ATTN_WIDTH), lambda i: (i, 0)),
            pl.BlockSpec((TM, CONV_WIDTH), lambda i: (i, 0)),
            pl.BlockSpec((TM, D_MODEL), lambda i: (i, 0)),
            pl.BlockSpec((D_MODEL, D_MODEL), lambda i: (0, 0)),
            pl.BlockSpec((1, D_MODEL), lambda i: (0, 0)),
            pl.BlockSpec((D_MODEL, nq), lambda i: (0, 0)),
        ],
        out_specs=[
            pl.BlockSpec((TM, D_MODEL), lambda i: (i, 0)),
            pl.BlockSpec((TM, ROW_TILES, LANES), lambda i: (i, 0, 0)),
            pl.BlockSpec((2 * PEER_HEADS, TM, PEER_KEYS), lambda i: (0, i, 0)),
        ],
        out_shape=[
            jax.ShapeDtypeStruct((t, D_MODEL), F32),
            jax.ShapeDtypeStruct((t, ROW_TILES, LANES), F32),
            jax.ShapeDtypeStruct((2 * PEER_HEADS, t, PEER_KEYS), BF16),
        ],
        compiler_params=pltpu.CompilerParams(
            dimension_semantics=("arbitrary",), vmem_limit_bytes=48 * 1024 * 1024),
        name="outproj_peerq",
    )(attn2d, conv2d, h2d, wo_bf16, g2, wq_bf16)


def _extract_top(s, ids, val_ref, idx_ref):
    for r in range(PEER_TOPK):
        m = jnp.max(s, axis=0, keepdims=True)
        idx = jnp.min(jnp.where(s == m, ids, PEER_TOPK * PEER_TOPK), axis=0, keepdims=True)
        val_ref[r:r + 1, :] = m
        idx_ref[r:r + 1, :] = idx
        s = jnp.where(ids == idx, NEG, s)


def _candidates(s1, s2):
    tokens = s1.shape[1]
    sub = lax.broadcasted_iota(I32, (SUBLANES, tokens), 0)
    vals = [s1[0:1, :] + s2]
    ids = [sub, sub + SUBLANES]
    for a in range(1, SUBLANES):
        vals.append(s1[a:a + 1, :] + s2[0:SUBLANES, :])
        ids.append(sub + a * PEER_TOPK)
    vals.append(s1[SUBLANES:, :] + s2[0:1, :])
    ids.append((sub + SUBLANES) * PEER_TOPK)
    return jnp.concatenate(vals, axis=0), jnp.concatenate(ids, axis=0)


def _topk_kernel(qp_ref, keys_ref, r_ref, sh_ref, gate_ref,
                 s1_ref, i1_ref, s2_ref, i2_ref, ct_ref, ci_ref, e_ref, g_ref):
    def head(h, slot):
        s1_s, i1_s, s2_s, i2_s, ct_s, ci_s = (
            ref.at[slot] for ref in (s1_ref, i1_ref, s2_ref, i2_ref, ct_ref, ci_ref))
        for p, (sv, si) in enumerate(((s1_s, i1_s), (s2_s, i2_s))):
            hp = 2 * h + p
            st = lax.dot_general(keys_ref[hp], qp_ref[hp], (((1,), (1,)), ((), ())),
                                 preferred_element_type=F32)
            _extract_top(st, lax.broadcasted_iota(I32, st.shape, 0), sv, si)
        cand, cand_ids = _candidates(s1_s[...], s2_s[...])
        _extract_top(cand, cand_ids, ct_s, ci_s)
        ct = ct_s[...]
        ci = ci_s[...]
        hi = ci >> 4
        lo = ci & (PEER_TOPK - 1)
        i1 = i1_s[...]
        i2 = i2_s[...]
        e1 = jnp.zeros_like(ci)
        e2 = jnp.zeros_like(ci)
        for a in range(PEER_TOPK):
            e1 = jnp.where(hi == a, i1[a:a + 1, :], e1)
            e2 = jnp.where(lo == a, i2[a:a + 1, :], e2)
        e = e1 * PEER_KEYS + e2
        ex = jnp.exp(ct - jnp.max(ct, axis=0, keepdims=True))
        gate = ex / jnp.sum(ex, axis=0, keepdims=True)
        row0 = pl.multiple_of(h * PEER_TOPK, PEER_TOPK)
        e_ref[pl.ds(row0, PEER_TOPK), :] = e
        g_ref[pl.ds(row0, PEER_TOPK), :] = gate

    def heads(g, _):
        for slot in range(TOPK_HEADS_PER_TRIP):
            head(g * TOPK_HEADS_PER_TRIP + slot, slot)
        return 0

    lax.fori_loop(0, PEER_HEADS // TOPK_HEADS_PER_TRIP, heads, 0)
    e = e_ref[...].T
    r_ref[...] = (e & (HALF_EXPERTS - 1)) * ROW_TILES
    sh_ref[...] = ((e >> 13) << 4).astype(F32)
    gate_ref[...] = g_ref[...].T


def _peer_topk(qp, keys_bf16):
    t = qp.shape[1]
    tt = TOPK_TOKENS
    sc = lambda dt: pltpu.VMEM((TOPK_HEADS_PER_TRIP, PEER_TOPK, tt), dt)
    return pl.pallas_call(
        _topk_kernel,
        grid=(t // tt,),
        in_specs=[
            pl.BlockSpec((2 * PEER_HEADS, tt, PEER_KEYS), lambda i: (0, i, 0)),
            pl.BlockSpec((2 * PEER_HEADS, PEER_KEYS, PEER_KEYS), lambda i: (0, 0, 0)),
        ],
        out_specs=[
            pl.BlockSpec((tt, PEER_PAIRS), lambda i: (i, 0)),
            pl.BlockSpec((tt, PEER_PAIRS), lambda i: (i, 0)),
            pl.BlockSpec((tt, PEER_PAIRS), lambda i: (i, 0)),
        ],
        out_shape=[
            jax.ShapeDtypeStruct((t, PEER_PAIRS), I32),
            jax.ShapeDtypeStruct((t, PEER_PAIRS), F32),
            jax.ShapeDtypeStruct((t, PEER_PAIRS), F32),
        ],
        scratch_shapes=[sc(F32), sc(I32), sc(F32), sc(I32), sc(F32), sc(I32),
                        pltpu.VMEM((PEER_PAIRS, tt), I32), pltpu.VMEM((PEER_PAIRS, tt), F32)],
        compiler_params=pltpu.CompilerParams(dimension_semantics=("arbitrary",)),
        name="peer_topk",
    )(qp, keys_bf16)


def _pack_kernel(hi_ref, lo_ref, o_ref):
    hi = pltpu.bitcast(hi_ref[...].astype(BF16).astype(F32), U32)
    lo = pltpu.bitcast(lo_ref[...].astype(BF16).astype(F32), U32)
    o_ref[...] = hi | (lo >> 16)


def _pack_table(tab):
    rows = 512
    nb = HALF_EXPERTS // rows
    packed = pl.pallas_call(
        _pack_kernel,
        grid=(nb,),
        in_specs=[pl.BlockSpec((rows, D_MODEL), lambda i: (i, 0)),
                  pl.BlockSpec((rows, D_MODEL), lambda i: (i + nb, 0))],
        out_specs=pl.BlockSpec((rows, D_MODEL), lambda i: (i, 0)),
        out_shape=jax.ShapeDtypeStruct((HALF_EXPERTS, D_MODEL), U32),
        compiler_params=pltpu.CompilerParams(dimension_semantics=("arbitrary",)),
        name="pack_table",
    )(tab, tab)
    return packed.reshape(HALF_EXPERTS * ROW_TILES, LANES)


def _expert_row(tab_ref, off, shv):
    word = tab_ref[pl.ds(pl.multiple_of(off, SUBLANES), SUBLANES), :]
    return pltpu.bitcast(jnp.left_shift(word, shv) & jnp.uint32(0xFFFF0000), F32)


def _rows_to_lanes(row):
    return jnp.transpose(jnp.broadcast_to(row, (PEER_PAIRS, LANES)))


def _shift_rows(shf_row):
    ri = lax.broadcasted_iota(I32, (PEER_PAIRS, PEER_PAIRS), 0)
    ci = lax.broadcasted_iota(I32, (PEER_PAIRS, PEER_PAIRS), 1)
    diag = jnp.where(ri == ci, shf_row, 0.0).astype(BF16)
    rep = jnp.dot(diag, jnp.ones((PEER_PAIRS, LANES), BF16), preferred_element_type=F32)
    return pltpu.bitcast(rep, U32) >> 26


def _fold8(prods):
    sub = lax.broadcasted_iota(I32, (SUBLANES, LANES), 0)
    cur = prods
    for sh in (1, 2, 4):
        keep = (sub & sh) == 0
        nxt = []
        for k in range(0, len(cur), 2):
            a = jnp.where(keep, cur[k], cur[k + 1])
            b = jnp.where(keep, cur[k + 1], cur[k])
            nxt.append(a + pltpu.roll(b, sh, axis=0))
        cur = nxt
    return cur[0]


def _peer_act_kernel(off_ref, shf_ref, x_ref, gate_ref, tab_ref, w_ref, shb_ref, part_ref, a_ref):
    def prep(t, _):
        r0 = pl.multiple_of(t * PEER_PAIRS, PEER_PAIRS)
        shb_ref[pl.ds(r0, PEER_PAIRS), :] = _shift_rows(shf_ref[pl.ds(t, 1), :])
        return 0

    lax.fori_loop(0, PEER_TB, prep, 0, unroll=PREP_UNROLL)

    groups_per_token = PEER_PAIRS // SUBLANES

    def group(gi, _):
        xt = x_ref[lax.shift_right_logical(gi, groups_per_token.bit_length() - 1)]
        j0 = pl.multiple_of(gi * SUBLANES, SUBLANES)
        offs = off_ref.at[pl.ds(j0, SUBLANES)]
        prods = []
        for jj in range(SUBLANES):
            f = _expert_row(tab_ref, offs[jj], shb_ref[pl.ds(j0 + jj, 1), :])
            prods.append(f * xt)
        part_ref[pl.ds(j0, SUBLANES), :] = _fold8(prods)
        return 0

    lax.fori_loop(0, PEER_TB * groups_per_token, group, 0, unroll=8)

    def lane_sums(t, _):
        r0 = pl.multiple_of(t * PEER_PAIRS, PEER_PAIRS)
        part_t = jnp.transpose(part_ref[pl.ds(r0, PEER_PAIRS), :])
        a_ref[pl.ds(t, 1), :] = jnp.sum(part_t, axis=0, keepdims=True)
        return 0

    lax.fori_loop(0, PEER_TB, lane_sums, 0, unroll=PREP_UNROLL)
    a = a_ref[...]
    act = 0.5 * a * (1.0 + lax.erf(a * (2.0 ** -0.5)))
    w_ref[...] = gate_ref[...] * act


def _peer_act(off, shf, x3, gate, tab):
    t = shf.shape[0]
    return pl.pallas_call(
        _peer_act_kernel,
        grid=(t // PEER_TB,),
        in_specs=[
            pl.BlockSpec((PEER_TB * PEER_PAIRS,), lambda i: (i,), memory_space=pltpu.SMEM),
            pl.BlockSpec((PEER_TB, PEER_PAIRS), lambda i: (i, 0)),
            pl.BlockSpec((PEER_TB, ROW_TILES, LANES), lambda i: (i, 0, 0)),
            pl.BlockSpec((PEER_TB, PEER_PAIRS), lambda i: (i, 0)),
            pl.BlockSpec((HALF_EXPERTS * ROW_TILES, LANES), lambda i: (0, 0),
                         pipeline_mode=pl.Buffered(1)),
        ],
        out_specs=pl.BlockSpec((PEER_TB, PEER_PAIRS), lambda i: (i, 0)),
        out_shape=jax.ShapeDtypeStruct((t, PEER_PAIRS), F32),
        scratch_shapes=[pltpu.VMEM((PEER_TB * PEER_PAIRS, LANES), U32),
                        pltpu.VMEM((PEER_TB * PEER_PAIRS, LANES), F32),
                        pltpu.VMEM((PEER_TB, PEER_PAIRS), F32)],
        compiler_params=pltpu.CompilerParams(
            dimension_semantics=("arbitrary",), vmem_limit_bytes=VMEM_TABLE_LIMIT),
        name="peer_act",
    )(off, shf, x3, gate, tab)


def _peer_out_kernel(off_ref, shf_ref, w_ref, h_ref, g_ref, tab_ref, o_ref,
                     shb_ref, wb_ref, ffn_ref):
    n_acc = 4

    def prep(t, _):
        r0 = pl.multiple_of(t * PEER_PAIRS, PEER_PAIRS)
        shb_ref[pl.ds(r0, PEER_PAIRS), :] = _shift_rows(shf_ref[pl.ds(t, 1), :])
        wb_ref[pl.ds(r0, PEER_PAIRS), :] = _rows_to_lanes(w_ref[pl.ds(t, 1), :])
        return 0

    lax.fori_loop(0, PEER_TB, prep, 0, unroll=PREP_UNROLL)

    def token(t, _):
        def group(g, accs):
            j0 = pl.multiple_of(t * PEER_PAIRS + g * PAIR_GROUP, PAIR_GROUP)
            accs = list(accs)
            offs = off_ref.at[pl.ds(j0, PAIR_GROUP)]
            for jj in range(PAIR_GROUP):
                f = _expert_row(tab_ref, offs[jj], shb_ref[pl.ds(j0 + jj, 1), :])
                accs[jj % n_acc] = accs[jj % n_acc] + wb_ref[pl.ds(j0 + jj, 1), :] * f
            return tuple(accs)

        zero = jnp.zeros((SUBLANES, LANES), F32)
        accs = lax.fori_loop(0, PEER_PAIRS // PAIR_GROUP, group, (zero,) * n_acc)
        ffn_ref[t] = (accs[0] + accs[1]) + (accs[2] + accs[3])
        return 0

    lax.fori_loop(0, PEER_TB, token, 0)
    ffn = jnp.concatenate([ffn_ref[:, k, :] for k in range(ROW_TILES)], axis=1)
    y = h_ref[...] + ffn
    ms = jnp.mean(y * y, axis=-1, keepdims=True)
    o_ref[...] = y * lax.rsqrt(ms + EPS) * g_ref[...]


def _peer_out(off, shf, w, h3, g3, tab):
    t = shf.shape[0]
    row = lambda: pl.BlockSpec((PEER_TB, PEER_PAIRS), lambda i: (i, 0))
    return pl.pallas_call(
        _peer_out_kernel,
        grid=(t // PEER_TB,),
        in_specs=[
            pl.BlockSpec((PEER_TB * PEER_PAIRS,), lambda i: (i,), memory_space=pltpu.SMEM),
            row(), row(),
            pl.BlockSpec((PEER_TB, D_MODEL), lambda i: (i, 0)),
            pl.BlockSpec((1, D_MODEL), lambda i: (0, 0)),
            pl.BlockSpec((HALF_EXPERTS * ROW_TILES, LANES), lambda i: (0, 0),
                         pipeline_mode=pl.Buffered(1)),
        ],
        out_specs=pl.BlockSpec((PEER_TB, D_MODEL), lambda i: (i, 0)),
        out_shape=jax.ShapeDtypeStruct((t, D_MODEL), F32),
        scratch_shapes=[pltpu.VMEM((PEER_TB * PEER_PAIRS, LANES), U32),
                        pltpu.VMEM((PEER_TB * PEER_PAIRS, LANES), F32),
                        pltpu.VMEM((PEER_TB, ROW_TILES, LANES), F32)],
        compiler_params=pltpu.CompilerParams(
            dimension_semantics=("arbitrary",), vmem_limit_bytes=VMEM_TABLE_LIMIT),
        name="peer_out",
    )(off, shf, w, h3, g3, tab)


def kernel(x, meta_tokens, norm1_g, w_in, lambda_q1, lambda_k1, lambda_q2, lambda_k2,
           attn_subln_g, conv_w, conv_b, conv_norm_g, conv_norm_b, w_out, norm2_g,
           peer_wq, peer_subkeys, peer_u, peer_v, final_norm_g):
    b, s, _ = x.shape
    seq = N_META + s
    lp = ((seq + Q_BLOCK - 1) // Q_BLOCK) * Q_BLOCK
    t = b * lp
    assert lp % TQ == 0 and lp % CONV_ROWS == 0 and t % TM == 0 and t % PEER_TB == 0

    meta = jnp.broadcast_to(meta_tokens[None].astype(x.dtype), (b, N_META, D_MODEL))
    h = jnp.concatenate([meta, x, jnp.zeros((b, lp - seq, D_MODEL), x.dtype)], axis=1)
    h2d = h.reshape(t, D_MODEL)

    lam_init = 0.8 - 0.6 * math.exp(-0.3 * 0)
    slopes = jnp.asarray([2.0 ** (-8.0 * (i + 1) / N_HEADS) for i in range(N_HEADS)], F32)
    group = jnp.arange(CONV_WIDTH) // CONV_GROUP
    gavg = ((group[:, None] == group[None, :]).astype(F32) * (1.0 / CONV_GROUP)).astype(BF16)

    q, k, v, u = _inproj(h2d, norm1_g[0][None], w_in[0].astype(BF16))
    attn = _attention(q.reshape(b, lp, -1), k.reshape(b, lp, -1), v.reshape(b, lp, -1), slopes,
                      lambda_q1[0][None], lambda_k1[0][None], lambda_q2[0][None],
                      lambda_k2[0][None], attn_subln_g[0][None], lam_init)
    conv = _conformer_conv(u.reshape(b, lp, -1), conv_w[0], conv_b[0][None], gavg,
                           conv_norm_g[0][None], conv_norm_b[0][None])
    h1, xn2, qp = _outproj(attn.reshape(t, -1), conv.reshape(t, -1), h2d,
                           w_out[0].astype(BF16), norm2_g[0][None], peer_wq[0].astype(BF16))
    keys = peer_subkeys[0].reshape(2 * PEER_HEADS, PEER_KEYS, PEER_KEYS).astype(BF16)
    off, shf, gate = _peer_topk(qp, keys)
    off = off.reshape(t * PEER_PAIRS)
    w = _peer_act(off, shf, xn2, gate, _pack_table(peer_u[0]))
    out = _peer_out(off, shf, w, h1, final_norm_g[None], _pack_table(peer_v[0]))
    return out.reshape(b, lp, D_MODEL)[:, N_META:N_META + s]
```

```python
import functools
import math

import jax
import jax.numpy as jnp
from jax import lax
from jax.experimental import pallas as pl
from jax.experimental.pallas import tpu as pltpu

F32 = jnp.float32
BF16 = jnp.bfloat16
I32 = jnp.int32
U32 = jnp.uint32

D_MODEL = 1024
N_META = 16
Q_BLOCK = 128
ATTN_WIDTH = 512
CONV_WIDTH = 512
N_HEADS = 4
HEAD_DIM = 64
V_DIM = 128
CONV_K = 31
CONV_GROUP = 64
PEER_HEADS = 8
PEER_KEYS = 128
PEER_TOPK = 16
PEER_PAIRS = PEER_HEADS * PEER_TOPK
N_EXPERTS = PEER_KEYS * PEER_KEYS
HALF_EXPERTS = N_EXPERTS // 2
EPS = 1e-6
NEG = -1e30

LANES = 128
SUBLANES = 8
ROW_TILES = D_MODEL // LANES

TM = 512
TQ = 384
CONV_ROWS = 64
CONV_PAD = 32
TOPK_TOKENS = 128
TOPK_HEADS_PER_TRIP = 2
PEER_TB = 64
PAIR_GROUP = 64
PREP_UNROLL = 8
VMEM_TABLE_LIMIT = 52 * 1024 * 1024


def _inproj_kernel(h_ref, g_ref, w_ref, q_ref, k_ref, v_ref, u_ref):
    x = h_ref[...]
    ms = jnp.mean(x * x, axis=-1, keepdims=True)
    xn = (x * lax.rsqrt(ms + EPS) * g_ref[...]).astype(BF16)
    proj = jnp.dot(xn, w_ref[...], preferred_element_type=F32)
    q_ref[...] = (proj[:, 0:ATTN_WIDTH] * (HEAD_DIM ** -0.5)).astype(BF16)
    k_ref[...] = proj[:, ATTN_WIDTH:2 * ATTN_WIDTH].astype(BF16)
    v_ref[...] = proj[:, 2 * ATTN_WIDTH:3 * ATTN_WIDTH].astype(BF16)
    ga = proj[:, 3 * ATTN_WIDTH:3 * ATTN_WIDTH + CONV_WIDTH]
    gg = proj[:, 3 * ATTN_WIDTH + CONV_WIDTH:]
    u_ref[...] = ga * jax.nn.sigmoid(gg)


def _inproj(h2d, g, w_bf16):
    t = h2d.shape[0]
    n_cols = w_bf16.shape[1]
    return pl.pallas_call(
        _inproj_kernel,
        grid=(t // TM,),
        in_specs=[
            pl.BlockSpec((TM, D_MODEL), lambda i: (i, 0)),
            pl.BlockSpec((1, D_MODEL), lambda i: (0, 0)),
            pl.BlockSpec((D_MODEL, n_cols), lambda i: (0, 0)),
        ],
        out_specs=[
            pl.BlockSpec((TM, ATTN_WIDTH), lambda i: (i, 0)),
            pl.BlockSpec((TM, ATTN_WIDTH), lambda i: (i, 0)),
            pl.BlockSpec((TM, ATTN_WIDTH), lambda i: (i, 0)),
            pl.BlockSpec((TM, CONV_WIDTH), lambda i: (i, 0)),
        ],
        out_shape=[
            jax.ShapeDtypeStruct((t, ATTN_WIDTH), BF16),
            jax.ShapeDtypeStruct((t, ATTN_WIDTH), BF16),
            jax.ShapeDtypeStruct((t, ATTN_WIDTH), BF16),
            jax.ShapeDtypeStruct((t, CONV_WIDTH), F32),
        ],
        compiler_params=pltpu.CompilerParams(
            dimension_semantics=("arbitrary",), vmem_limit_bytes=48 * 1024 * 1024),
        name="inproj",
    )(h2d, g, w_bf16)


def _attn_kernel(slopes_ref, lq1_ref, lk1_ref, lq2_ref, lk2_ref, subg_ref,
                 q_ref, k_ref, v_ref, o_ref, *, lam_init):
    hd = pl.program_id(1)
    qi = pl.program_id(2)
    slope = slopes_ref[hd]
    lam = (jnp.exp(jnp.sum(lq1_ref[...] * lk1_ref[...], keepdims=True))
           - jnp.exp(jnp.sum(lq2_ref[...] * lk2_ref[...], keepdims=True)) + lam_init)

    q = q_ref[0]
    lane = lax.broadcasted_iota(I32, q.shape, 1)
    zero = jnp.zeros_like(q)
    qs = jnp.concatenate([jnp.where(lane < HEAD_DIM, q, zero),
                          jnp.where(lane >= HEAD_DIM, q, zero)], axis=0)

    q0 = qi * TQ
    col = lax.broadcasted_iota(I32, (1, TQ), 1)

    def step(j, carry, masked):
        m, l, acc = carry
        k0 = pl.multiple_of(j * TQ, TQ)
        kj = k_ref[0, pl.ds(k0, TQ), :]
        vj = v_ref[0, pl.ds(k0, TQ), :]
        s = lax.dot_general(qs, kj, (((1,), (1,)), ((), ())), preferred_element_type=F32)
        s = s + slope * (col + (k0 - q0)).astype(F32)
        if masked:
            row = lax.broadcasted_iota(I32, (2 * TQ, TQ), 0)
            row = jnp.where(row >= TQ, row - TQ, row)
            cc = lax.broadcasted_iota(I32, (2 * TQ, TQ), 1)
            s = jnp.where(cc <= row, s, NEG)
        m_new = jnp.maximum(m, jnp.max(s, axis=1, keepdims=True))
        alpha = jnp.exp(m - m_new)
        p = jnp.exp(s - m_new)
        l = alpha * l + jnp.sum(p, axis=1, keepdims=True)
        acc = alpha * acc + jnp.dot(p.astype(BF16), vj, preferred_element_type=F32)
        return m_new, l, acc

    init = (jnp.full((2 * TQ, 1), NEG, F32), jnp.zeros((2 * TQ, 1), F32),
            jnp.zeros((2 * TQ, V_DIM), F32))
    carry = lax.fori_loop(0, qi, lambda j, c: step(j, c, False), init)
    m, l, acc = step(qi, carry, True)
    o = acc / l
    a = o[:TQ] - lam * o[TQ:]
    ms = jnp.mean(a * a, axis=-1, keepdims=True)
    y = a * lax.rsqrt(ms + EPS) * subg_ref[...] * (1.0 - lam_init)
    o_ref[0] = y.astype(BF16)


def _attention(q, k, v, slopes, lq1, lk1, lq2, lk2, subg, lam_init):
    b, lp, _ = q.shape
    nq = lp // TQ
    vec = lambda n: pl.BlockSpec((1, n), lambda bi, hi, i: (0, 0))
    return pl.pallas_call(
        functools.partial(_attn_kernel, lam_init=lam_init),
        grid=(b, N_HEADS, nq),
        in_specs=[
            pl.BlockSpec(memory_space=pltpu.SMEM),
            vec(HEAD_DIM), vec(HEAD_DIM), vec(HEAD_DIM), vec(HEAD_DIM), vec(V_DIM),
            pl.BlockSpec((1, TQ, V_DIM), lambda bi, hi, i: (bi, i, hi)),
            pl.BlockSpec((1, lp, V_DIM), lambda bi, hi, i: (bi, 0, hi)),
            pl.BlockSpec((1, lp, V_DIM), lambda bi, hi, i: (bi, 0, hi)),
        ],
        out_specs=pl.BlockSpec((1, TQ, V_DIM), lambda bi, hi, i: (bi, i, hi)),
        out_shape=jax.ShapeDtypeStruct((b, lp, ATTN_WIDTH), BF16),
        compiler_params=pltpu.CompilerParams(
            dimension_semantics=("arbitrary", "arbitrary", "arbitrary"),
            vmem_limit_bytes=48 * 1024 * 1024),
        name="diff_attn",
    )(slopes, lq1, lk1, lq2, lk2, subg, q, k, v)


def _group_mean(v, gavg_bf16):
    hi = v.astype(BF16)
    lo = (v - hi.astype(F32)).astype(BF16)
    return (jnp.dot(hi, gavg_bf16, preferred_element_type=F32)
            + jnp.dot(lo, gavg_bf16, preferred_element_type=F32))


def _conv_kernel(u_ref, w_ref, b_ref, gavg_ref, g_ref, beta_ref, o_ref, upad_ref, y_ref):
    lp = u_ref.shape[1]
    upad_ref[0:CONV_PAD, :] = jnp.zeros((CONV_PAD, CONV_WIDTH), F32)
    upad_ref[CONV_PAD:, :] = u_ref[0]

    def chunk(c, _):
        base = pl.multiple_of(c * CONV_ROWS, CONV_ROWS)
        for lb in range(CONV_WIDTH // LANES):
            ls = slice(lb * LANES, (lb + 1) * LANES)
            acc = jnp.zeros((CONV_ROWS, LANES), F32) + b_ref[:, ls]
            win = upad_ref[pl.ds(base, CONV_ROWS + CONV_PAD), ls]
            shifted = [win] + [jnp.roll(win, -rho, axis=0) for rho in range(1, SUBLANES)]
            for t in range(CONV_K):
                off = CONV_PAD - (CONV_K - 1) + t
                rho = off % SUBLANES
                acc = acc + w_ref[t:t + 1, ls] * shifted[rho][off - rho:off - rho + CONV_ROWS, :]
            y_ref[:, ls] = acc
        y = y_ref[...]
        mu = _group_mean(y, gavg_ref[...])
        d = y - mu
        var = _group_mean(d * d, gavg_ref[...])
        yn = d * lax.rsqrt(var + EPS) * g_ref[...] + beta_ref[...]
        o_ref[0, pl.ds(base, CONV_ROWS), :] = (yn * jax.nn.sigmoid(yn)).astype(BF16)
        return 0

    lax.fori_loop(0, lp // CONV_ROWS, chunk, 0)


def _conformer_conv(u, conv_w, conv_b, gavg, gn_g, gn_b):
    b, lp, c = u.shape
    full = lambda shape: pl.BlockSpec(shape, lambda bi: (0,) * len(shape))
    return pl.pallas_call(
        _conv_kernel,
        grid=(b,),
        in_specs=[
            pl.BlockSpec((1, lp, c), lambda bi: (bi, 0, 0)),
            full((CONV_K, c)), full((1, c)), full((c, c)), full((1, c)), full((1, c)),
        ],
        out_specs=pl.BlockSpec((1, lp, c), lambda bi: (bi, 0, 0)),
        out_shape=jax.ShapeDtypeStruct((b, lp, c), BF16),
        scratch_shapes=[pltpu.VMEM((lp + CONV_PAD, c), F32), pltpu.VMEM((CONV_ROWS, c), F32)],
        compiler_params=pltpu.CompilerParams(
            dimension_semantics=("arbitrary",), vmem_limit_bytes=56 * 1024 * 1024),
        name="conformer_conv",
    )(u, conv_w, conv_b, gavg, gn_g, gn_b)


def _store_row_tiles(ref, val):
    for k in range(ROW_TILES):
        ref[:, k, :] = val[:, k * LANES:(k + 1) * LANES]


def _outproj_kernel(a_ref, c_ref, h_ref, wo_ref, g_ref, wq_ref, h1_ref, xn_ref, qp_ref):
    mix = (jnp.dot(a_ref[...], wo_ref[0:ATTN_WIDTH, :], preferred_element_type=F32)
           + jnp.dot(c_ref[...], wo_ref[ATTN_WIDTH:, :], preferred_element_type=F32))
    h1 = h_ref[...] + mix
    h1_ref[...] = h1
    ms = jnp.mean(h1 * h1, axis=-1, keepdims=True)
    xn = h1 * lax.rsqrt(ms + EPS) * g_ref[...]
    _store_row_tiles(xn_ref, xn)
    qp = jnp.dot(xn.astype(BF16), wq_ref[...], preferred_element_type=F32)
    for hp in range(2 * PEER_HEADS):
        qp_ref[hp] = qp[:, hp * PEER_KEYS:(hp + 1) * PEER_KEYS].astype(BF16)


def _outproj(attn2d, conv2d, h2d, wo_bf16, g2, wq_bf16):
    t = h2d.shape[0]
    nq = wq_bf16.shape[1]
    return pl.pallas_call(
        _outproj_kernel,
        grid=(t // TM,),
        in_specs=[
            pl.BlockSpec((TM, ATTN_WIDTH), lambda i: (i, 0)),
            pl.BlockSpec((TM, CONV_WIDTH), lambda i: (i, 0)),
            pl.BlockSpec((TM, D_MODEL), lambda i: (i, 0)),
            pl.BlockSpec((D_MODEL, D_MODEL), lambda i: (0, 0)),
            pl.BlockSpec((1, D_MODEL), lambda i: (0, 0)),
            pl.BlockSpec((D_MODEL, nq), lambda i: (0, 0)),
        ],
        out_specs=[
            pl.BlockSpec((TM, D_MODEL), lambda i: (i, 0)),
            pl.BlockSpec((TM, ROW_TILES, LANES), lambda i: (i, 0, 0)),
            pl.BlockSpec((2 * PEER_HEADS, TM, PEER_KEYS), lambda i: (0, i, 0)),
        ],
        out_shape=[
            jax.ShapeDtypeStruct((t, D_MODEL), F32),
            jax.ShapeDtypeStruct((t, ROW_TILES, LANES), F32),
            jax.ShapeDtypeStruct((2 * PEER_HEADS, t, PEER_KEYS), BF16),
        ],
        compiler_params=pltpu.CompilerParams(
            dimension_semantics=("arbitrary",), vmem_limit_bytes=48 * 1024 * 1024),
        name="outproj_peerq",
    )(attn2d, conv2d, h2d, wo_bf16, g2, wq_bf16)


def _extract_top(s, ids, val_ref, idx_ref):
    for r in range(PEER_TOPK):
        m = jnp.max(s, axis=0, keepdims=True)
        idx = jnp.min(jnp.where(s == m, ids, PEER_TOPK * PEER_TOPK), axis=0, keepdims=True)
        val_ref[r:r + 1, :] = m
        idx_ref[r:r + 1, :] = idx
        s = jnp.where(ids == idx, NEG, s)


def _candidates(s1, s2):
    tokens = s1.shape[1]
    sub = lax.broadcasted_iota(I32, (SUBLANES, tokens), 0)
    vals = [s1[0:1, :] + s2]
    ids = [sub, sub + SUBLANES]
    for a in range(1, SUBLANES):
        vals.append(s1[a:a + 1, :] + s2[0:SUBLANES, :])
        ids.append(sub + a * PEER_TOPK)
    vals.append(s1[SUBLANES:, :] + s2[0:1, :])
    ids.append((sub + SUBLANES) * PEER_TOPK)
    return jnp.concatenate(vals, axis=0), jnp.concatenate(ids, axis=0)


def _topk_kernel(qp_ref, keys_ref, r_ref, sh_ref, gate_ref,
                 s1_ref, i1_ref, s2_ref, i2_ref, ct_ref, ci_ref, e_ref, g_ref):
    def head(h, slot):
        s1_s, i1_s, s2_s, i2_s, ct_s, ci_s = (
            ref.at[slot] for ref in (s1_ref, i1_ref, s2_ref, i2_ref, ct_ref, ci_ref))
        for p, (sv, si) in enumerate(((s1_s, i1_s), (s2_s, i2_s))):
            hp = 2 * h + p
            st = lax.dot_general(keys_ref[hp], qp_ref[hp], (((1,), (1,)), ((), ())),
                                 preferred_element_type=F32)
            _extract_top(st, lax.broadcasted_iota(I32, st.shape, 0), sv, si)
        cand, cand_ids = _candidates(s1_s[...], s2_s[...])
        _extract_top(cand, cand_ids, ct_s, ci_s)
        ct = ct_s[...]
        ci = ci_s[...]
        hi = ci >> 4
        lo = ci & (PEER_TOPK - 1)
        i1 = i1_s[...]
        i2 = i2_s[...]
        e1 = jnp.zeros_like(ci)
        e2 = jnp.zeros_like(ci)
        for a in range(PEER_TOPK):
            e1 = jnp.where(hi == a, i1[a:a + 1, :], e1)
            e2 = jnp.where(lo == a, i2[a:a + 1, :], e2)
        e = e1 * PEER_KEYS + e2
        ex = jnp.exp(ct - jnp.max(ct, axis=0, keepdims=True))
        gate = ex / jnp.sum(ex, axis=0, keepdims=True)
        row0 = pl.multiple_of(h * PEER_TOPK, PEER_TOPK)
        e_ref[pl.ds(row0, PEER_TOPK), :] = e
        g_ref[pl.ds(row0, PEER_TOPK), :] = gate

    def heads(g, _):
        for slot in range(TOPK_HEADS_PER_TRIP):
            head(g * TOPK_HEADS_PER_TRIP + slot, slot)
        return 0

    lax.fori_loop(0, PEER_HEADS // TOPK_HEADS_PER_TRIP, heads, 0)
    e = e_ref[...].T
    r_ref[...] = (e & (HALF_EXPERTS - 1)) * ROW_TILES
    sh_ref[...] = ((e >> 13) << 4).astype(F32)
    gate_ref[...] = g_ref[...].T


def _peer_topk(qp, keys_bf16):
    t = qp.shape[1]
    tt = TOPK_TOKENS
    sc = lambda dt: pltpu.VMEM((TOPK_HEADS_PER_TRIP, PEER_TOPK, tt), dt)
    return pl.pallas_call(
        _topk_kernel,
        grid=(t // tt,),
        in_specs=[
            pl.BlockSpec((2 * PEER_HEADS, tt, PEER_KEYS), lambda i: (0, i, 0)),
            pl.BlockSpec((2 * PEER_HEADS, PEER_KEYS, PEER_KEYS), lambda i: (0, 0, 0)),
        ],
        out_specs=[
            pl.BlockSpec((tt, PEER_PAIRS), lambda i: (i, 0)),
            pl.BlockSpec((tt, PEER_PAIRS), lambda i: (i, 0)),
            pl.BlockSpec((tt, PEER_PAIRS), lambda i: (i, 0)),
        ],
        out_shape=[
            jax.ShapeDtypeStruct((t, PEER_PAIRS), I32),
            jax.ShapeDtypeStruct((t, PEER_PAIRS), F32),
            jax.ShapeDtypeStruct((t, PEER_PAIRS), F32),
        ],
        scratch_shapes=[sc(F32), sc(I32), sc(F32), sc(I32), sc(F32), sc(I32),
                        pltpu.VMEM((PEER_PAIRS, tt), I32), pltpu.VMEM((PEER_PAIRS, tt), F32)],
        compiler_params=pltpu.CompilerParams(dimension_semantics=("arbitrary",)),
        name="peer_topk",
    )(qp, keys_bf16)


def _pack_kernel(hi_ref, lo_ref, o_ref):
    hi = pltpu.bitcast(hi_ref[...].astype(BF16).astype(F32), U32)
    lo = pltpu.bitcast(lo_ref[...].astype(BF16).astype(F32), U32)
    o_ref[...] = hi | (lo >> 16)


def _pack_table(tab):
    rows = 512
    nb = HALF_EXPERTS // rows
    packed = pl.pallas_call(
        _pack_kernel,
        grid=(nb,),
        in_specs=[pl.BlockSpec((rows, D_MODEL), lambda i: (i, 0)),
                  pl.BlockSpec((rows, D_MODEL), lambda i: (i + nb, 0))],
        out_specs=pl.BlockSpec((rows, D_MODEL), lambda i: (i, 0)),
        out_shape=jax.ShapeDtypeStruct((HALF_EXPERTS, D_MODEL), U32),
        compiler_params=pltpu.CompilerParams(dimension_semantics=("arbitrary",)),
        name="pack_table",
    )(tab, tab)
    return packed.reshape(HALF_EXPERTS * ROW_TILES, LANES)


def _expert_row(tab_ref, off, shv):
    word = tab_ref[pl.ds(pl.multiple_of(off, SUBLANES), SUBLANES), :]
    return pltpu.bitcast(jnp.left_shift(word, shv) & jnp.uint32(0xFFFF0000), F32)


def _rows_to_lanes(row):
    return jnp.transpose(jnp.broadcast_to(row, (PEER_PAIRS, LANES)))


def _shift_rows(shf_row):
    ri = lax.broadcasted_iota(I32, (PEER_PAIRS, PEER_PAIRS), 0)
    ci = lax.broadcasted_iota(I32, (PEER_PAIRS, PEER_PAIRS), 1)
    diag = jnp.where(ri == ci, shf_row, 0.0).astype(BF16)
    rep = jnp.dot(diag, jnp.ones((PEER_PAIRS, LANES), BF16), preferred_element_type=F32)
    return pltpu.bitcast(rep, U32) >> 26


def _fold8(prods):
    sub = lax.broadcasted_iota(I32, (SUBLANES, LANES), 0)
    cur = prods
    for sh in (1, 2, 4):
        keep = (sub & sh) == 0
        nxt = []
        for k in range(0, len(cur), 2):
            a = jnp.where(keep, cur[k], cur[k + 1])
            b = jnp.where(keep, cur[k + 1], cur[k])
            nxt.append(a + pltpu.roll(b, sh, axis=0))
        cur = nxt
    return cur[0]


def _peer_act_kernel(off_ref, shf_ref, x_ref, gate_ref, tab_ref, w_ref, shb_ref, part_ref, a_ref):
    def prep(t, _):
        r0 = pl.multiple_of(t * PEER_PAIRS, PEER_PAIRS)
        shb_ref[pl.ds(r0, PEER_PAIRS), :] = _shift_rows(shf_ref[pl.ds(t, 1), :])
        return 0

    lax.fori_loop(0, PEER_TB, prep, 0, unroll=PREP_UNROLL)

    groups_per_token = PEER_PAIRS // SUBLANES

    def group(gi, _):
        xt = x_ref[lax.shift_right_logical(gi, groups_per_token.bit_length() - 1)]
        j0 = pl.multiple_of(gi * SUBLANES, SUBLANES)
        offs = off_ref.at[pl.ds(j0, SUBLANES)]
        prods = []
        for jj in range(SUBLANES):
            f = _expert_row(tab_ref, offs[jj], shb_ref[pl.ds(j0 + jj, 1), :])
            prods.append(f * xt)
        part_ref[pl.ds(j0, SUBLANES), :] = _fold8(prods)
        return 0

    lax.fori_loop(0, PEER_TB * groups_per_token, group, 0, unroll=8)

    def lane_sums(t, _):
        r0 = pl.multiple_of(t * PEER_PAIRS, PEER_PAIRS)
        part_t = jnp.transpose(part_ref[pl.ds(r0, PEER_PAIRS), :])
        a_ref[pl.ds(t, 1), :] = jnp.sum(part_t, axis=0, keepdims=True)
        return 0

    lax.fori_loop(0, PEER_TB, lane_sums, 0, unroll=PREP_UNROLL)
    a = a_ref[...]
    act = 0.5 * a * (1.0 + lax.erf(a * (2.0 ** -0.5)))
    w_ref[...] = gate_ref[...] * act


def _real_token_specs(lp, seq_real):
    tiles = seq_real // PEER_TB

    def start(i):
        return pl.multiple_of((i // tiles) * lp + N_META + (i % tiles) * PEER_TB, N_META)

    offsets = pl.BlockSpec((pl.Element(PEER_TB * PEER_PAIRS),),
                           lambda i: (start(i) * PEER_PAIRS,), memory_space=pltpu.SMEM)
    rows = lambda *tail: pl.BlockSpec((pl.Element(PEER_TB),) + tuple(pl.Element(d) for d in tail),
                                      lambda i: (start(i),) + (0,) * len(tail))
    return offsets, rows


def _peer_act(off, shf, x3, gate, tab, lp, seq_real):
    n_real = (shf.shape[0] // lp) * seq_real
    offsets, rows = _real_token_specs(lp, seq_real)
    return pl.pallas_call(
        _peer_act_kernel,
        grid=(n_real // PEER_TB,),
        in_specs=[
            offsets,
            rows(PEER_PAIRS),
            rows(ROW_TILES, LANES),
            rows(PEER_PAIRS),
            pl.BlockSpec((HALF_EXPERTS * ROW_TILES, LANES), lambda i: (0, 0),
                         pipeline_mode=pl.Buffered(1)),
        ],
        out_specs=pl.BlockSpec((PEER_TB, PEER_PAIRS), lambda i: (i, 0)),
        out_shape=jax.ShapeDtypeStruct((n_real, PEER_PAIRS), F32),
        scratch_shapes=[pltpu.VMEM((PEER_TB * PEER_PAIRS, LANES), U32),
                        pltpu.VMEM((PEER_TB * PEER_PAIRS, LANES), F32),
                        pltpu.VMEM((PEER_TB, PEER_PAIRS), F32)],
        compiler_params=pltpu.CompilerParams(
            dimension_semantics=("arbitrary",), vmem_limit_bytes=VMEM_TABLE_LIMIT),
        name="peer_act",
    )(off, shf, x3, gate, tab)


def _peer_out_kernel(off_ref, shf_ref, w_ref, h_ref, g_ref, tab_ref, o_ref,
                     shb_ref, wb_ref, ffn_ref):
    n_acc = 4

    def prep(t, _):
        r0 = pl.multiple_of(t * PEER_PAIRS, PEER_PAIRS)
        shb_ref[pl.ds(r0, PEER_PAIRS), :] = _shift_rows(shf_ref[pl.ds(t, 1), :])
        wb_ref[pl.ds(r0, PEER_PAIRS), :] = _rows_to_lanes(w_ref[pl.ds(t, 1), :])
        return 0

    lax.fori_loop(0, PEER_TB, prep, 0, unroll=PREP_UNROLL)

    def token(t, _):
        def group(g, accs):
            j0 = pl.multiple_of(t * PEER_PAIRS + g * PAIR_GROUP, PAIR_GROUP)
            accs = list(accs)
            offs = off_ref.at[pl.ds(j0, PAIR_GROUP)]
            for jj in range(PAIR_GROUP):
                f = _expert_row(tab_ref, offs[jj], shb_ref[pl.ds(j0 + jj, 1), :])
                accs[jj % n_acc] = accs[jj % n_acc] + wb_ref[pl.ds(j0 + jj, 1), :] * f
            return tuple(accs)

        zero = jnp.zeros((SUBLANES, LANES), F32)
        accs = lax.fori_loop(0, PEER_PAIRS // PAIR_GROUP, group, (zero,) * n_acc)
        ffn_ref[t] = (accs[0] + accs[1]) + (accs[2] + accs[3])
        return 0

    lax.fori_loop(0, PEER_TB, token, 0)
    ffn = jnp.concatenate([ffn_ref[:, k, :] for k in range(ROW_TILES)], axis=1)
    y = h_ref[...] + ffn
    ms = jnp.mean(y * y, axis=-1, keepdims=True)
    o_ref[...] = y * lax.rsqrt(ms + EPS) * g_ref[...]


def _peer_out(off, shf, w, h2d, g, tab, lp, seq_real):
    n_real = w.shape[0]
    offsets, rows = _real_token_specs(lp, seq_real)
    return pl.pallas_call(
        _peer_out_kernel,
        grid=(n_real // PEER_TB,),
        in_specs=[
            offsets,
            rows(PEER_PAIRS),
            pl.BlockSpec((PEER_TB, PEER_PAIRS), lambda i: (i, 0)),
            rows(D_MODEL),
            pl.BlockSpec((1, D_MODEL), lambda i: (0, 0)),
            pl.BlockSpec((HALF_EXPERTS * ROW_TILES, LANES), lambda i: (0, 0),
                         pipeline_mode=pl.Buffered(1)),
        ],
        out_specs=pl.BlockSpec((PEER_TB, D_MODEL), lambda i: (i, 0)),
        out_shape=jax.ShapeDtypeStruct((n_real, D_MODEL), F32),
        scratch_shapes=[pltpu.VMEM((PEER_TB * PEER_PAIRS, LANES), U32),
                        pltpu.VMEM((PEER_TB * PEER_PAIRS, LANES), F32),
                        pltpu.VMEM((PEER_TB, ROW_TILES, LANES), F32)],
        compiler_params=pltpu.CompilerParams(
            dimension_semantics=("arbitrary",), vmem_limit_bytes=VMEM_TABLE_LIMIT),
        name="peer_out",
    )(off, shf, w, h2d, g, tab)


def kernel(x, meta_tokens, norm1_g, w_in, lambda_q1, lambda_k1, lambda_q2, lambda_k2,
           attn_subln_g, conv_w, conv_b, conv_norm_g, conv_norm_b, w_out, norm2_g,
           peer_wq, peer_subkeys, peer_u, peer_v, final_norm_g):
    b, s, _ = x.shape
    seq = N_META + s
    lp = ((seq + Q_BLOCK - 1) // Q_BLOCK) * Q_BLOCK
    t = b * lp
    assert lp % TQ == 0 and lp % CONV_ROWS == 0 and t % TM == 0 and s % PEER_TB == 0

    meta = jnp.broadcast_to(meta_tokens[None].astype(x.dtype), (b, N_META, D_MODEL))
    h = jnp.concatenate([meta, x, jnp.zeros((b, lp - seq, D_MODEL), x.dtype)], axis=1)
    h2d = h.reshape(t, D_MODEL)

    lam_init = 0.8 - 0.6 * math.exp(-0.3 * 0)
    slopes = jnp.asarray([2.0 ** (-8.0 * (i + 1) / N_HEADS) for i in range(N_HEADS)], F32)
    group = jnp.arange(CONV_WIDTH) // CONV_GROUP
    gavg = ((group[:, None] == group[None, :]).astype(F32) * (1.0 / CONV_GROUP)).astype(BF16)

    q, k, v, u = _inproj(h2d, norm1_g[0][None], w_in[0].astype(BF16))
    attn = _attention(q.reshape(b, lp, -1), k.reshape(b, lp, -1), v.reshape(b, lp, -1), slopes,
                      lambda_q1[0][None], lambda_k1[0][None], lambda_q2[0][None],
                      lambda_k2[0][None], attn_subln_g[0][None], lam_init)
    conv = _conformer_conv(u.reshape(b, lp, -1), conv_w[0], conv_b[0][None], gavg,
                           conv_norm_g[0][None], conv_norm_b[0][None])
    h1, xn2, qp = _outproj(attn.reshape(t, -1), conv.reshape(t, -1), h2d,
                           w_out[0].astype(BF16), norm2_g[0][None], peer_wq[0].astype(BF16))
    keys = peer_subkeys[0].reshape(2 * PEER_HEADS, PEER_KEYS, PEER_KEYS).astype(BF16)
    off, shf, gate = _peer_topk(qp, keys)
    off = off.reshape(t * PEER_PAIRS)
    w = _peer_act(off, shf, xn2, gate, _pack_table(peer_u[0]), lp, s)
    out = _peer_out(off, shf, w, h1, final_norm_g[None], _pack_table(peer_v[0]), lp, s)
    return out.reshape(b, s, D_MODEL)
```

```python
import functools
import math

import jax
import jax.numpy as jnp
from jax import lax
from jax.experimental import pallas as pl
from jax.experimental.pallas import tpu as pltpu

F32 = jnp.float32
BF16 = jnp.bfloat16
I32 = jnp.int32
U32 = jnp.uint32

D_MODEL = 1024
N_META = 16
Q_BLOCK = 128
ATTN_WIDTH = 512
CONV_WIDTH = 512
N_HEADS = 4
HEAD_DIM = 64
V_DIM = 128
CONV_K = 31
CONV_GROUP = 64
PEER_HEADS = 8
PEER_KEYS = 128
PEER_TOPK = 16
PEER_PAIRS = PEER_HEADS * PEER_TOPK
N_EXPERTS = PEER_KEYS * PEER_KEYS
HALF_EXPERTS = N_EXPERTS // 2
EPS = 1e-6
NEG = -1e30

LANES = 128
SUBLANES = 8
ROW_TILES = D_MODEL // LANES

TM = 512
TQ = 384
CONV_ROWS = 64
CONV_PAD = 32
TOPK_TOKENS = 128
TOPK_HEADS_PER_TRIP = 2
PEER_TB = 64
ACT_TOKENS_PER_TRIP = 4
PAIR_GROUP = 64
PREP_UNROLL = 8
VMEM_TABLE_LIMIT = 52 * 1024 * 1024


def _inproj_kernel(h_ref, g_ref, w_ref, q_ref, k_ref, v_ref, u_ref):
    x = h_ref[...]
    ms = jnp.mean(x * x, axis=-1, keepdims=True)
    xn = (x * lax.rsqrt(ms + EPS) * g_ref[...]).astype(BF16)
    proj = jnp.dot(xn, w_ref[...], preferred_element_type=F32)
    q_ref[...] = (proj[:, 0:ATTN_WIDTH] * (HEAD_DIM ** -0.5)).astype(BF16)
    k_ref[...] = proj[:, ATTN_WIDTH:2 * ATTN_WIDTH].astype(BF16)
    v_ref[...] = proj[:, 2 * ATTN_WIDTH:3 * ATTN_WIDTH].astype(BF16)
    ga = proj[:, 3 * ATTN_WIDTH:3 * ATTN_WIDTH + CONV_WIDTH]
    gg = proj[:, 3 * ATTN_WIDTH + CONV_WIDTH:]
    u_ref[...] = ga * jax.nn.sigmoid(gg)


def _inproj(h2d, g, w_bf16):
    t = h2d.shape[0]
    n_cols = w_bf16.shape[1]
    return pl.pallas_call(
        _inproj_kernel,
        grid=(t // TM,),
        in_specs=[
            pl.BlockSpec((TM, D_MODEL), lambda i: (i, 0)),
            pl.BlockSpec((1, D_MODEL), lambda i: (0, 0)),
            pl.BlockSpec((D_MODEL, n_cols), lambda i: (0, 0)),
        ],
        out_specs=[
            pl.BlockSpec((TM, ATTN_WIDTH), lambda i: (i, 0)),
            pl.BlockSpec((TM, ATTN_WIDTH), lambda i: (i, 0)),
            pl.BlockSpec((TM, ATTN_WIDTH), lambda i: (i, 0)),
            pl.BlockSpec((TM, CONV_WIDTH), lambda i: (i, 0)),
        ],
        out_shape=[
            jax.ShapeDtypeStruct((t, ATTN_WIDTH), BF16),
            jax.ShapeDtypeStruct((t, ATTN_WIDTH), BF16),
            jax.ShapeDtypeStruct((t, ATTN_WIDTH), BF16),
            jax.ShapeDtypeStruct((t, CONV_WIDTH), F32),
        ],
        compiler_params=pltpu.CompilerParams(
            dimension_semantics=("arbitrary",), vmem_limit_bytes=48 * 1024 * 1024),
        name="inproj",
    )(h2d, g, w_bf16)


def _attn_kernel(slopes_ref, lq1_ref, lk1_ref, lq2_ref, lk2_ref, subg_ref,
                 q_ref, k_ref, v_ref, o_ref, *, lam_init):
    hd = pl.program_id(1)
    qi = pl.program_id(2)
    slope = slopes_ref[hd]
    lam = (jnp.exp(jnp.sum(lq1_ref[...] * lk1_ref[...], keepdims=True))
           - jnp.exp(jnp.sum(lq2_ref[...] * lk2_ref[...], keepdims=True)) + lam_init)

    q = q_ref[0]
    lane = lax.broadcasted_iota(I32, q.shape, 1)
    zero = jnp.zeros_like(q)
    qs = jnp.concatenate([jnp.where(lane < HEAD_DIM, q, zero),
                          jnp.where(lane >= HEAD_DIM, q, zero)], axis=0)

    q0 = qi * TQ
    col = lax.broadcasted_iota(I32, (1, TQ), 1)

    def step(j, carry, masked):
        m, l, acc = carry
        k0 = pl.multiple_of(j * TQ, TQ)
        kj = k_ref[0, pl.ds(k0, TQ), :]
        vj = v_ref[0, pl.ds(k0, TQ), :]
        s = lax.dot_general(qs, kj, (((1,), (1,)), ((), ())), preferred_element_type=F32)
        s = s + slope * (col + (k0 - q0)).astype(F32)
        if masked:
            row = lax.broadcasted_iota(I32, (2 * TQ, TQ), 0)
            row = jnp.where(row >= TQ, row - TQ, row)
            cc = lax.broadcasted_iota(I32, (2 * TQ, TQ), 1)
            s = jnp.where(cc <= row, s, NEG)
        m_new = jnp.maximum(m, jnp.max(s, axis=1, keepdims=True))
        alpha = jnp.exp(m - m_new)
        p = jnp.exp(s - m_new)
        l = alpha * l + jnp.sum(p, axis=1, keepdims=True)
        acc = alpha * acc + jnp.dot(p.astype(BF16), vj, preferred_element_type=F32)
        return m_new, l, acc

    init = (jnp.full((2 * TQ, 1), NEG, F32), jnp.zeros((2 * TQ, 1), F32),
            jnp.zeros((2 * TQ, V_DIM), F32))
    carry = lax.fori_loop(0, qi, lambda j, c: step(j, c, False), init)
    m, l, acc = step(qi, carry, True)
    o = acc / l
    a = o[:TQ] - lam * o[TQ:]
    ms = jnp.mean(a * a, axis=-1, keepdims=True)
    y = a * lax.rsqrt(ms + EPS) * subg_ref[...] * (1.0 - lam_init)
    o_ref[0] = y.astype(BF16)


def _attention(q, k, v, slopes, lq1, lk1, lq2, lk2, subg, lam_init):
    b, lp, _ = q.shape
    nq = lp // TQ
    vec = lambda n: pl.BlockSpec((1, n), lambda bi, hi, i: (0, 0))
    return pl.pallas_call(
        functools.partial(_attn_kernel, lam_init=lam_init),
        grid=(b, N_HEADS, nq),
        in_specs=[
            pl.BlockSpec(memory_space=pltpu.SMEM),
            vec(HEAD_DIM), vec(HEAD_DIM), vec(HEAD_DIM), vec(HEAD_DIM), vec(V_DIM),
            pl.BlockSpec((1, TQ, V_DIM), lambda bi, hi, i: (bi, i, hi)),
            pl.BlockSpec((1, lp, V_DIM), lambda bi, hi, i: (bi, 0, hi)),
            pl.BlockSpec((1, lp, V_DIM), lambda bi, hi, i: (bi, 0, hi)),
        ],
        out_specs=pl.BlockSpec((1, TQ, V_DIM), lambda bi, hi, i: (bi, i, hi)),
        out_shape=jax.ShapeDtypeStruct((b, lp, ATTN_WIDTH), BF16),
        compiler_params=pltpu.CompilerParams(
            dimension_semantics=("arbitrary", "arbitrary", "arbitrary"),
            vmem_limit_bytes=48 * 1024 * 1024),
        name="diff_attn",
    )(slopes, lq1, lk1, lq2, lk2, subg, q, k, v)


def _group_mean(v, gavg_bf16):
    hi = v.astype(BF16)
    lo = (v - hi.astype(F32)).astype(BF16)
    return (jnp.dot(hi, gavg_bf16, preferred_element_type=F32)
            + jnp.dot(lo, gavg_bf16, preferred_element_type=F32))


def _conv_kernel(u_ref, w_ref, b_ref, gavg_ref, g_ref, beta_ref, o_ref, upad_ref, y_ref):
    lp = u_ref.shape[1]
    upad_ref[0:CONV_PAD, :] = jnp.zeros((CONV_PAD, CONV_WIDTH), F32)
    upad_ref[CONV_PAD:, :] = u_ref[0]

    def chunk(c, _):
        base = pl.multiple_of(c * CONV_ROWS, CONV_ROWS)
        for lb in range(CONV_WIDTH // LANES):
            ls = slice(lb * LANES, (lb + 1) * LANES)
            acc = jnp.zeros((CONV_ROWS, LANES), F32) + b_ref[:, ls]
            win = upad_ref[pl.ds(base, CONV_ROWS + CONV_PAD), ls]
            shifted = [win] + [jnp.roll(win, -rho, axis=0) for rho in range(1, SUBLANES)]
            for t in range(CONV_K):
                off = CONV_PAD - (CONV_K - 1) + t
                rho = off % SUBLANES
                acc = acc + w_ref[t:t + 1, ls] * shifted[rho][off - rho:off - rho + CONV_ROWS, :]
            y_ref[:, ls] = acc
        y = y_ref[...]
        mu = _group_mean(y, gavg_ref[...])
        d = y - mu
        var = _group_mean(d * d, gavg_ref[...])
        yn = d * lax.rsqrt(var + EPS) * g_ref[...] + beta_ref[...]
        o_ref[0, pl.ds(base, CONV_ROWS), :] = (yn * jax.nn.sigmoid(yn)).astype(BF16)
        return 0

    lax.fori_loop(0, lp // CONV_ROWS, chunk, 0)


def _conformer_conv(u, conv_w, conv_b, gavg, gn_g, gn_b):
    b, lp, c = u.shape
    full = lambda shape: pl.BlockSpec(shape, lambda bi: (0,) * len(shape))
    return pl.pallas_call(
        _conv_kernel,
        grid=(b,),
        in_specs=[
            pl.BlockSpec((1, lp, c), lambda bi: (bi, 0, 0)),
            full((CONV_K, c)), full((1, c)), full((c, c)), full((1, c)), full((1, c)),
        ],
        out_specs=pl.BlockSpec((1, lp, c), lambda bi: (bi, 0, 0)),
        out_shape=jax.ShapeDtypeStruct((b, lp, c), BF16),
        scratch_shapes=[pltpu.VMEM((lp + CONV_PAD, c), F32), pltpu.VMEM((CONV_ROWS, c), F32)],
        compiler_params=pltpu.CompilerParams(
            dimension_semantics=("arbitrary",), vmem_limit_bytes=56 * 1024 * 1024),
        name="conformer_conv",
    )(u, conv_w, conv_b, gavg, gn_g, gn_b)


def _store_row_tiles(ref, val):
    for k in range(ROW_TILES):
        ref[:, k, :] = val[:, k * LANES:(k + 1) * LANES]


def _outproj_kernel(a_ref, c_ref, h_ref, wo_ref, g_ref, wq_ref, h1_ref, xn_ref, qp_ref):
    mix = (jnp.dot(a_ref[...], wo_ref[0:ATTN_WIDTH, :], preferred_element_type=F32)
           + jnp.dot(c_ref[...], wo_ref[ATTN_WIDTH:, :], preferred_element_type=F32))
    h1 = h_ref[...] + mix
    h1_ref[...] = h1
    ms = jnp.mean(h1 * h1, axis=-1, keepdims=True)
    xn = h1 * lax.rsqrt(ms + EPS) * g_ref[...]
    _store_row_tiles(xn_ref, xn)
    qp = jnp.dot(xn.astype(BF16), wq_ref[...], preferred_element_type=F32)
    for hp in range(2 * PEER_HEADS):
        qp_ref[hp] = qp[:, hp * PEER_KEYS:(hp + 1) * PEER_KEYS].astype(BF16)


def _outproj(attn2d, conv2d, h2d, wo_bf16, g2, wq_bf16):
    t = h2d.shape[0]
    nq = wq_bf16.shape[1]
    return pl.pallas_call(
        _outproj_kernel,
        grid=(t // TM,),
        in_specs=[
            pl.BlockSpec((TM, ATTN_WIDTH), lambda i: (i, 0)),
            pl.BlockSpec((TM, CONV_WIDTH), lambda i: (i, 0)),
            pl.BlockSpec((TM, D_MODEL), lambda i: (i, 0)),
            pl.BlockSpec((D_MODEL, D_MODEL), lambda i: (0, 0)),
            pl.BlockSpec((1, D_MODEL), lambda i: (0, 0)),
            pl.BlockSpec((D_MODEL, nq), lambda i: (0, 0)),
        ],
        out_specs=[
            pl.BlockSpec((TM, D_MODEL), lambda i: (i, 0)),
            pl.BlockSpec((TM, ROW_TILES, LANES), lambda i: (i, 0, 0)),
            pl.BlockSpec((2 * PEER_HEADS, TM, PEER_KEYS), lambda i: (0, i, 0)),
        ],
        out_shape=[
            jax.ShapeDtypeStruct((t, D_MODEL), F32),
            jax.ShapeDtypeStruct((t, ROW_TILES, LANES), F32),
            jax.ShapeDtypeStruct((2 * PEER_HEADS, t, PEER_KEYS), BF16),
        ],
        compiler_params=pltpu.CompilerParams(
            dimension_semantics=("arbitrary",), vmem_limit_bytes=48 * 1024 * 1024),
        name="outproj_peerq",
    )(attn2d, conv2d, h2d, wo_bf16, g2, wq_bf16)


def _extract_top(s, ids, val_ref, idx_ref):
    for r in range(PEER_TOPK):
        m = jnp.max(s, axis=0, keepdims=True)
        idx = jnp.min(jnp.where(s == m, ids, PEER_TOPK * PEER_TOPK), axis=0, keepdims=True)
        val_ref[r:r + 1, :] = m
        idx_ref[r:r + 1, :] = idx
        s = jnp.where(ids == idx, NEG, s)


def _candidates(s1, s2):
    tokens = s1.shape[1]
    sub = lax.broadcasted_iota(I32, (SUBLANES, tokens), 0)
    vals = [s1[0:1, :] + s2]
    ids = [sub, sub + SUBLANES]
    for a in range(1, SUBLANES):
        vals.append(s1[a:a + 1, :] + s2[0:SUBLANES, :])
        ids.append(sub + a * PEER_TOPK)
    vals.append(s1[SUBLANES:, :] + s2[0:1, :])
    ids.append((sub + SUBLANES) * PEER_TOPK)
    return jnp.concatenate(vals, axis=0), jnp.concatenate(ids, axis=0)


def _topk_kernel(qp_ref, keys_ref, r_ref, sh_ref, gate_ref,
                 s1_ref, i1_ref, s2_ref, i2_ref, ct_ref, ci_ref, e_ref, g_ref):
    def head(h, slot):
        s1_s, i1_s, s2_s, i2_s, ct_s, ci_s = (
            ref.at[slot] for ref in (s1_ref, i1_ref, s2_ref, i2_ref, ct_ref, ci_ref))
        for p, (sv, si) in enumerate(((s1_s, i1_s), (s2_s, i2_s))):
            hp = 2 * h + p
            st = lax.dot_general(keys_ref[hp], qp_ref[hp], (((1,), (1,)), ((), ())),
                                 preferred_element_type=F32)
            _extract_top(st, lax.broadcasted_iota(I32, st.shape, 0), sv, si)
        cand, cand_ids = _candidates(s1_s[...], s2_s[...])
        _extract_top(cand, cand_ids, ct_s, ci_s)
        ct = ct_s[...]
        ci = ci_s[...]
        hi = ci >> 4
        lo = ci & (PEER_TOPK - 1)
        i1 = i1_s[...]
        i2 = i2_s[...]
        e1 = jnp.zeros_like(ci)
        e2 = jnp.zeros_like(ci)
        for a in range(PEER_TOPK):
            e1 = jnp.where(hi == a, i1[a:a + 1, :], e1)
            e2 = jnp.where(lo == a, i2[a:a + 1, :], e2)
        e = e1 * PEER_KEYS + e2
        ex = jnp.exp(ct - jnp.max(ct, axis=0, keepdims=True))
        gate = ex / jnp.sum(ex, axis=0, keepdims=True)
        row0 = pl.multiple_of(h * PEER_TOPK, PEER_TOPK)
        e_ref[pl.ds(row0, PEER_TOPK), :] = e
        g_ref[pl.ds(row0, PEER_TOPK), :] = gate

    def heads(g, _):
        for slot in range(TOPK_HEADS_PER_TRIP):
            head(g * TOPK_HEADS_PER_TRIP + slot, slot)
        return 0

    lax.fori_loop(0, PEER_HEADS // TOPK_HEADS_PER_TRIP, heads, 0)
    e = e_ref[...].T
    r_ref[...] = (e & (HALF_EXPERTS - 1)) * ROW_TILES
    sh_ref[...] = ((e >> 13) << 4).astype(F32)
    gate_ref[...] = g_ref[...].T


def _peer_topk(qp, keys_bf16):
    t = qp.shape[1]
    tt = TOPK_TOKENS
    sc = lambda dt: pltpu.VMEM((TOPK_HEADS_PER_TRIP, PEER_TOPK, tt), dt)
    return pl.pallas_call(
        _topk_kernel,
        grid=(t // tt,),
        in_specs=[
            pl.BlockSpec((2 * PEER_HEADS, tt, PEER_KEYS), lambda i: (0, i, 0)),
            pl.BlockSpec((2 * PEER_HEADS, PEER_KEYS, PEER_KEYS), lambda i: (0, 0, 0)),
        ],
        out_specs=[
            pl.BlockSpec((tt, PEER_PAIRS), lambda i: (i, 0)),
            pl.BlockSpec((tt, PEER_PAIRS), lambda i: (i, 0)),
            pl.BlockSpec((tt, PEER_PAIRS), lambda i: (i, 0)),
        ],
        out_shape=[
            jax.ShapeDtypeStruct((t, PEER_PAIRS), I32),
            jax.ShapeDtypeStruct((t, PEER_PAIRS), F32),
            jax.ShapeDtypeStruct((t, PEER_PAIRS), F32),
        ],
        scratch_shapes=[sc(F32), sc(I32), sc(F32), sc(I32), sc(F32), sc(I32),
                        pltpu.VMEM((PEER_PAIRS, tt), I32), pltpu.VMEM((PEER_PAIRS, tt), F32)],
        compiler_params=pltpu.CompilerParams(dimension_semantics=("arbitrary",)),
        name="peer_topk",
    )(qp, keys_bf16)


def _pack_kernel(hi_ref, lo_ref, o_ref):
    hi = pltpu.bitcast(hi_ref[...].astype(BF16).astype(F32), U32)
    lo = pltpu.bitcast(lo_ref[...].astype(BF16).astype(F32), U32)
    o_ref[...] = hi | (lo >> 16)


def _pack_table(tab):
    rows = 512
    nb = HALF_EXPERTS // rows
    packed = pl.pallas_call(
        _pack_kernel,
        grid=(nb,),
        in_specs=[pl.BlockSpec((rows, D_MODEL), lambda i: (i, 0)),
                  pl.BlockSpec((rows, D_MODEL), lambda i: (i + nb, 0))],
        out_specs=pl.BlockSpec((rows, D_MODEL), lambda i: (i, 0)),
        out_shape=jax.ShapeDtypeStruct((HALF_EXPERTS, D_MODEL), U32),
        compiler_params=pltpu.CompilerParams(dimension_semantics=("arbitrary",)),
        name="pack_table",
    )(tab, tab)
    return packed.reshape(HALF_EXPERTS * ROW_TILES, LANES)


def _expert_row(tab_ref, off, shv):
    word = tab_ref[pl.ds(pl.multiple_of(off, SUBLANES), SUBLANES), :]
    return pltpu.bitcast(jnp.left_shift(word, shv) & jnp.uint32(0xFFFF0000), F32)


def _rows_to_lanes(row):
    return jnp.transpose(jnp.broadcast_to(row, (PEER_PAIRS, LANES)))


def _shift_rows(shf_row):
    ri = lax.broadcasted_iota(I32, (PEER_PAIRS, PEER_PAIRS), 0)
    ci = lax.broadcasted_iota(I32, (PEER_PAIRS, PEER_PAIRS), 1)
    diag = jnp.where(ri == ci, shf_row, 0.0).astype(BF16)
    rep = jnp.dot(diag, jnp.ones((PEER_PAIRS, LANES), BF16), preferred_element_type=F32)
    return pltpu.bitcast(rep, U32) >> 26


def _fold8(prods):
    sub = lax.broadcasted_iota(I32, (SUBLANES, LANES), 0)
    cur = prods
    for sh in (1, 2, 4):
        keep = (sub & sh) == 0
        nxt = []
        for k in range(0, len(cur), 2):
            a = jnp.where(keep, cur[k], cur[k + 1])
            b = jnp.where(keep, cur[k + 1], cur[k])
            nxt.append(a + pltpu.roll(b, sh, axis=0))
        cur = nxt
    return cur[0]


def _peer_act_kernel(off_ref, shf_ref, x_ref, gate_ref, tab_ref, w_ref, shb_ref, a_ref):
    def prep(t, slot):
        shb_ref[slot] = _shift_rows(shf_ref[pl.ds(jnp.minimum(t, PEER_TB - 1), 1), :])

    pair_slot = (lax.broadcasted_iota(I32, (SUBLANES, LANES), 1)
                 - lax.broadcasted_iota(I32, (SUBLANES, LANES), 0))

    def pairs(t, slot):
        xt = x_ref[t]
        t0 = pl.multiple_of(t * PEER_PAIRS, PEER_PAIRS)
        spread = jnp.zeros((SUBLANES, LANES), F32)
        for g in range(PEER_PAIRS // SUBLANES):
            prods = []
            for jj in range(g * SUBLANES, (g + 1) * SUBLANES):
                f = _expert_row(tab_ref, off_ref[t0 + jj], shb_ref[slot, jj:jj + 1, :])
                prods.append(f * xt)
            sums = jnp.sum(_fold8(prods), axis=1, keepdims=True)
            spread = jnp.where(pair_slot == g * SUBLANES, sums, spread)
        a_ref[pl.ds(t, 1), :] = jnp.sum(spread, axis=0, keepdims=True)

    prep(0, 0)

    def tokens(i, _):
        for k in range(ACT_TOKENS_PER_TRIP):
            t = i * ACT_TOKENS_PER_TRIP + k
            prep(t + 1, (k + 1) % 2)
            pairs(t, k % 2)
        return 0

    lax.fori_loop(0, PEER_TB // ACT_TOKENS_PER_TRIP, tokens, 0)
    a = a_ref[...]
    act = 0.5 * a * (1.0 + lax.erf(a * (2.0 ** -0.5)))
    w_ref[...] = gate_ref[...] * act


def _real_token_specs(lp, seq_real):
    tiles = seq_real // PEER_TB

    def start(i):
        return pl.multiple_of((i // tiles) * lp + N_META + (i % tiles) * PEER_TB, N_META)

    offsets = pl.BlockSpec((pl.Element(PEER_TB * PEER_PAIRS),),
                           lambda i: (start(i) * PEER_PAIRS,), memory_space=pltpu.SMEM)
    rows = lambda *tail: pl.BlockSpec((pl.Element(PEER_TB),) + tuple(pl.Element(d) for d in tail),
                                      lambda i: (start(i),) + (0,) * len(tail))
    return offsets, rows


def _peer_act(off, shf, x3, gate, tab, lp, seq_real):
    n_real = (shf.shape[0] // lp) * seq_real
    offsets, rows = _real_token_specs(lp, seq_real)
    return pl.pallas_call(
        _peer_act_kernel,
        grid=(n_real // PEER_TB,),
        in_specs=[
            offsets,
            rows(PEER_PAIRS),
            rows(ROW_TILES, LANES),
            rows(PEER_PAIRS),
            pl.BlockSpec((HALF_EXPERTS * ROW_TILES, LANES), lambda i: (0, 0),
                         pipeline_mode=pl.Buffered(1)),
        ],
        out_specs=pl.BlockSpec((PEER_TB, PEER_PAIRS), lambda i: (i, 0)),
        out_shape=jax.ShapeDtypeStruct((n_real, PEER_PAIRS), F32),
        scratch_shapes=[pltpu.VMEM((2, PEER_PAIRS, LANES), U32),
                        pltpu.VMEM((PEER_TB, PEER_PAIRS), F32)],
        compiler_params=pltpu.CompilerParams(
            dimension_semantics=("arbitrary",), vmem_limit_bytes=VMEM_TABLE_LIMIT),
        name="peer_act",
    )(off, shf, x3, gate, tab)


def _peer_out_kernel(off_ref, shf_ref, w_ref, h_ref, g_ref, tab_ref, o_ref,
                     shb_ref, wb_ref, ffn_ref):
    n_acc = 4

    def prep(t, _):
        r0 = pl.multiple_of(t * PEER_PAIRS, PEER_PAIRS)
        shb_ref[pl.ds(r0, PEER_PAIRS), :] = _shift_rows(shf_ref[pl.ds(t, 1), :])
        wb_ref[pl.ds(r0, PEER_PAIRS), :] = _rows_to_lanes(w_ref[pl.ds(t, 1), :])
        return 0

    lax.fori_loop(0, PEER_TB, prep, 0, unroll=PREP_UNROLL)

    def token(t, _):
        def group(g, accs):
            j0 = pl.multiple_of(t * PEER_PAIRS + g * PAIR_GROUP, PAIR_GROUP)
            accs = list(accs)
            offs = off_ref.at[pl.ds(j0, PAIR_GROUP)]
            for jj in range(PAIR_GROUP):
                f = _expert_row(tab_ref, offs[jj], shb_ref[pl.ds(j0 + jj, 1), :])
                accs[jj % n_acc] = accs[jj % n_acc] + wb_ref[pl.ds(j0 + jj, 1), :] * f
            return tuple(accs)

        zero = jnp.zeros((SUBLANES, LANES), F32)
        accs = lax.fori_loop(0, PEER_PAIRS // PAIR_GROUP, group, (zero,) * n_acc)
        ffn_ref[t] = (accs[0] + accs[1]) + (accs[2] + accs[3])
        return 0

    lax.fori_loop(0, PEER_TB, token, 0)
    ffn = jnp.concatenate([ffn_ref[:, k, :] for k in range(ROW_TILES)], axis=1)
    y = h_ref[...] + ffn
    ms = jnp.mean(y * y, axis=-1, keepdims=True)
    o_ref[...] = y * lax.rsqrt(ms + EPS) * g_ref[...]


def _peer_out(off, shf, w, h2d, g, tab, lp, seq_real):
    n_real = w.shape[0]
    offsets, rows = _real_token_specs(lp, seq_real)
    return pl.pallas_call(
        _peer_out_kernel,
        grid=(n_real // PEER_TB,),
        in_specs=[
            offsets,
            rows(PEER_PAIRS),
            pl.BlockSpec((PEER_TB, PEER_PAIRS), lambda i: (i, 0)),
            rows(D_MODEL),
            pl.BlockSpec((1, D_MODEL), lambda i: (0, 0)),
            pl.BlockSpec((HALF_EXPERTS * ROW_TILES, LANES), lambda i: (0, 0),
                         pipeline_mode=pl.Buffered(1)),
        ],
        out_specs=pl.BlockSpec((PEER_TB, D_MODEL), lambda i: (i, 0)),
        out_shape=jax.ShapeDtypeStruct((n_real, D_MODEL), F32),
        scratch_shapes=[pltpu.VMEM((PEER_TB * PEER_PAIRS, LANES), U32),
                        pltpu.VMEM((PEER_TB * PEER_PAIRS, LANES), F32),
                        pltpu.VMEM((PEER_TB, ROW_TILES, LANES), F32)],
        compiler_params=pltpu.CompilerParams(
            dimension_semantics=("arbitrary",), vmem_limit_bytes=VMEM_TABLE_LIMIT),
        name="peer_out",
    )(off, shf, w, h2d, g, tab)


def kernel(x, meta_tokens, norm1_g, w_in, lambda_q1, lambda_k1, lambda_q2, lambda_k2,
           attn_subln_g, conv_w, conv_b, conv_norm_g, conv_norm_b, w_out, norm2_g,
           peer_wq, peer_subkeys, peer_u, peer_v, final_norm_g):
    b, s, _ = x.shape
    seq = N_META + s
    lp = ((seq + Q_BLOCK - 1) // Q_BLOCK) * Q_BLOCK
    t = b * lp
    assert lp % TQ == 0 and lp % CONV_ROWS == 0 and t % TM == 0 and s % PEER_TB == 0

    meta = jnp.broadcast_to(meta_tokens[None].astype(x.dtype), (b, N_META, D_MODEL))
    h = jnp.concatenate([meta, x, jnp.zeros((b, lp - seq, D_MODEL), x.dtype)], axis=1)
    h2d = h.reshape(t, D_MODEL)

    lam_init = 0.8 - 0.6 * math.exp(-0.3 * 0)
    slopes = jnp.asarray([2.0 ** (-8.0 * (i + 1) / N_HEADS) for i in range(N_HEADS)], F32)
    group = jnp.arange(CONV_WIDTH) // CONV_GROUP
    gavg = ((group[:, None] == group[None, :]).astype(F32) * (1.0 / CONV_GROUP)).astype(BF16)

    q, k, v, u = _inproj(h2d, norm1_g[0][None], w_in[0].astype(BF16))
    attn = _attention(q.reshape(b, lp, -1), k.reshape(b, lp, -1), v.reshape(b, lp, -1), slopes,
                      lambda_q1[0][None], lambda_k1[0][None], lambda_q2[0][None],
                      lambda_k2[0][None], attn_subln_g[0][None], lam_init)
    conv = _conformer_conv(u.reshape(b, lp, -1), conv_w[0], conv_b[0][None], gavg,
                           conv_norm_g[0][None], conv_norm_b[0][None])
    h1, xn2, qp = _outproj(attn.reshape(t, -1), conv.reshape(t, -1), h2d,
                           w_out[0].astype(BF16), norm2_g[0][None], peer_wq[0].astype(BF16))
    keys = peer_subkeys[0].reshape(2 * PEER_HEADS, PEER_KEYS, PEER_KEYS).astype(BF16)
    off, shf, gate = _peer_topk(qp, keys)
    off = off.reshape(t * PEER_PAIRS)
    w = _peer_act(off, shf, xn2, gate, _pack_table(peer_u[0]), lp, s)
    out = _peer_out(off, shf, w, h1, final_norm_g[None], _pack_table(peer_v[0]), lp, s)
    return out.reshape(b, s, D_MODEL)
```

```python
import functools
import math

import jax
import jax.numpy as jnp
from jax import lax
from jax.experimental import pallas as pl
from jax.experimental.pallas import tpu as pltpu

F32 = jnp.float32
BF16 = jnp.bfloat16
I32 = jnp.int32
U32 = jnp.uint32

D_MODEL = 1024
N_META = 16
Q_BLOCK = 128
ATTN_WIDTH = 512
CONV_WIDTH = 512
N_HEADS = 4
HEAD_DIM = 64
V_DIM = 128
CONV_K = 31
CONV_GROUP = 64
PEER_HEADS = 8
PEER_KEYS = 128
PEER_TOPK = 16
PEER_PAIRS = PEER_HEADS * PEER_TOPK
N_EXPERTS = PEER_KEYS * PEER_KEYS
HALF_EXPERTS = N_EXPERTS // 2
EPS = 1e-6
NEG = -1e30

LANES = 128
SUBLANES = 8
ROW_TILES = D_MODEL // LANES

TM = 512
TQ = 384
CONV_ROWS = 64
CONV_PAD = 32
TOPK_TOKENS = 128
TOPK_HEADS_PER_TRIP = 2
PEER_TB = 64
ACT_TOKENS_PER_TRIP = 4
OFF_STREAMS = 8
PREP_UNROLL = 8
VMEM_TABLE_LIMIT = 52 * 1024 * 1024


def _inproj_kernel(h_ref, g_ref, w_ref, q_ref, k_ref, v_ref, u_ref):
    x = h_ref[...]
    ms = jnp.mean(x * x, axis=-1, keepdims=True)
    xn = (x * lax.rsqrt(ms + EPS) * g_ref[...]).astype(BF16)
    proj = jnp.dot(xn, w_ref[...], preferred_element_type=F32)
    q_ref[...] = (proj[:, 0:ATTN_WIDTH] * (HEAD_DIM ** -0.5)).astype(BF16)
    k_ref[...] = proj[:, ATTN_WIDTH:2 * ATTN_WIDTH].astype(BF16)
    v_ref[...] = proj[:, 2 * ATTN_WIDTH:3 * ATTN_WIDTH].astype(BF16)
    ga = proj[:, 3 * ATTN_WIDTH:3 * ATTN_WIDTH + CONV_WIDTH]
    gg = proj[:, 3 * ATTN_WIDTH + CONV_WIDTH:]
    u_ref[...] = ga * jax.nn.sigmoid(gg)


def _inproj(h2d, g, w_bf16):
    t = h2d.shape[0]
    n_cols = w_bf16.shape[1]
    return pl.pallas_call(
        _inproj_kernel,
        grid=(t // TM,),
        in_specs=[
            pl.BlockSpec((TM, D_MODEL), lambda i: (i, 0)),
            pl.BlockSpec((1, D_MODEL), lambda i: (0, 0)),
            pl.BlockSpec((D_MODEL, n_cols), lambda i: (0, 0)),
        ],
        out_specs=[
            pl.BlockSpec((TM, ATTN_WIDTH), lambda i: (i, 0)),
            pl.BlockSpec((TM, ATTN_WIDTH), lambda i: (i, 0)),
            pl.BlockSpec((TM, ATTN_WIDTH), lambda i: (i, 0)),
            pl.BlockSpec((TM, CONV_WIDTH), lambda i: (i, 0)),
        ],
        out_shape=[
            jax.ShapeDtypeStruct((t, ATTN_WIDTH), BF16),
            jax.ShapeDtypeStruct((t, ATTN_WIDTH), BF16),
            jax.ShapeDtypeStruct((t, ATTN_WIDTH), BF16),
            jax.ShapeDtypeStruct((t, CONV_WIDTH), F32),
        ],
        compiler_params=pltpu.CompilerParams(
            dimension_semantics=("arbitrary",), vmem_limit_bytes=48 * 1024 * 1024),
        name="inproj",
    )(h2d, g, w_bf16)


def _attn_kernel(slopes_ref, lq1_ref, lk1_ref, lq2_ref, lk2_ref, subg_ref,
                 q_ref, k_ref, v_ref, o_ref, *, lam_init):
    hd = pl.program_id(1)
    qi = pl.program_id(2)
    slope = slopes_ref[hd]
    lam = (jnp.exp(jnp.sum(lq1_ref[...] * lk1_ref[...], keepdims=True))
           - jnp.exp(jnp.sum(lq2_ref[...] * lk2_ref[...], keepdims=True)) + lam_init)

    q = q_ref[0]
    lane = lax.broadcasted_iota(I32, q.shape, 1)
    zero = jnp.zeros_like(q)
    qs = jnp.concatenate([jnp.where(lane < HEAD_DIM, q, zero),
                          jnp.where(lane >= HEAD_DIM, q, zero)], axis=0)

    q0 = qi * TQ
    col = lax.broadcasted_iota(I32, (1, TQ), 1)

    def step(j, carry, masked):
        m, l, acc = carry
        k0 = pl.multiple_of(j * TQ, TQ)
        kj = k_ref[0, pl.ds(k0, TQ), :]
        vj = v_ref[0, pl.ds(k0, TQ), :]
        s = lax.dot_general(qs, kj, (((1,), (1,)), ((), ())), preferred_element_type=F32)
        s = s + slope * (col + (k0 - q0)).astype(F32)
        if masked:
            row = lax.broadcasted_iota(I32, (2 * TQ, TQ), 0)
            row = jnp.where(row >= TQ, row - TQ, row)
            cc = lax.broadcasted_iota(I32, (2 * TQ, TQ), 1)
            s = jnp.where(cc <= row, s, NEG)
        m_new = jnp.maximum(m, jnp.max(s, axis=1, keepdims=True))
        alpha = jnp.exp(m - m_new)
        p = jnp.exp(s - m_new)
        l = alpha * l + jnp.sum(p, axis=1, keepdims=True)
        acc = alpha * acc + jnp.dot(p.astype(BF16), vj, preferred_element_type=F32)
        return m_new, l, acc

    init = (jnp.full((2 * TQ, 1), NEG, F32), jnp.zeros((2 * TQ, 1), F32),
            jnp.zeros((2 * TQ, V_DIM), F32))
    carry = lax.fori_loop(0, qi, lambda j, c: step(j, c, False), init)
    m, l, acc = step(qi, carry, True)
    o = acc / l
    a = o[:TQ] - lam * o[TQ:]
    ms = jnp.mean(a * a, axis=-1, keepdims=True)
    y = a * lax.rsqrt(ms + EPS) * subg_ref[...] * (1.0 - lam_init)
    o_ref[0] = y.astype(BF16)


def _attention(q, k, v, slopes, lq1, lk1, lq2, lk2, subg, lam_init):
    b, lp, _ = q.shape
    nq = lp // TQ
    vec = lambda n: pl.BlockSpec((1, n), lambda bi, hi, i: (0, 0))
    return pl.pallas_call(
        functools.partial(_attn_kernel, lam_init=lam_init),
        grid=(b, N_HEADS, nq),
        in_specs=[
            pl.BlockSpec(memory_space=pltpu.SMEM),
            vec(HEAD_DIM), vec(HEAD_DIM), vec(HEAD_DIM), vec(HEAD_DIM), vec(V_DIM),
            pl.BlockSpec((1, TQ, V_DIM), lambda bi, hi, i: (bi, i, hi)),
            pl.BlockSpec((1, lp, V_DIM), lambda bi, hi, i: (bi, 0, hi)),
            pl.BlockSpec((1, lp, V_DIM), lambda bi, hi, i: (bi, 0, hi)),
        ],
        out_specs=pl.BlockSpec((1, TQ, V_DIM), lambda bi, hi, i: (bi, i, hi)),
        out_shape=jax.ShapeDtypeStruct((b, lp, ATTN_WIDTH), BF16),
        compiler_params=pltpu.CompilerParams(
            dimension_semantics=("arbitrary", "arbitrary", "arbitrary"),
            vmem_limit_bytes=48 * 1024 * 1024),
        name="diff_attn",
    )(slopes, lq1, lk1, lq2, lk2, subg, q, k, v)


def _group_mean(v, gavg_bf16):
    hi = v.astype(BF16)
    lo = (v - hi.astype(F32)).astype(BF16)
    return (jnp.dot(hi, gavg_bf16, preferred_element_type=F32)
            + jnp.dot(lo, gavg_bf16, preferred_element_type=F32))


def _conv_kernel(u_ref, w_ref, b_ref, gavg_ref, g_ref, beta_ref, o_ref, upad_ref, y_ref):
    lp = u_ref.shape[1]
    upad_ref[0:CONV_PAD, :] = jnp.zeros((CONV_PAD, CONV_WIDTH), F32)
    upad_ref[CONV_PAD:, :] = u_ref[0]

    def chunk(c, _):
        base = pl.multiple_of(c * CONV_ROWS, CONV_ROWS)
        for lb in range(CONV_WIDTH // LANES):
            ls = slice(lb * LANES, (lb + 1) * LANES)
            acc = jnp.zeros((CONV_ROWS, LANES), F32) + b_ref[:, ls]
            win = upad_ref[pl.ds(base, CONV_ROWS + CONV_PAD), ls]
            shifted = [win] + [jnp.roll(win, -rho, axis=0) for rho in range(1, SUBLANES)]
            for t in range(CONV_K):
                off = CONV_PAD - (CONV_K - 1) + t
                rho = off % SUBLANES
                acc = acc + w_ref[t:t + 1, ls] * shifted[rho][off - rho:off - rho + CONV_ROWS, :]
            y_ref[:, ls] = acc
        y = y_ref[...]
        mu = _group_mean(y, gavg_ref[...])
        d = y - mu
        var = _group_mean(d * d, gavg_ref[...])
        yn = d * lax.rsqrt(var + EPS) * g_ref[...] + beta_ref[...]
        o_ref[0, pl.ds(base, CONV_ROWS), :] = (yn * jax.nn.sigmoid(yn)).astype(BF16)
        return 0

    lax.fori_loop(0, lp // CONV_ROWS, chunk, 0)


def _conformer_conv(u, conv_w, conv_b, gavg, gn_g, gn_b):
    b, lp, c = u.shape
    full = lambda shape: pl.BlockSpec(shape, lambda bi: (0,) * len(shape))
    return pl.pallas_call(
        _conv_kernel,
        grid=(b,),
        in_specs=[
            pl.BlockSpec((1, lp, c), lambda bi: (bi, 0, 0)),
            full((CONV_K, c)), full((1, c)), full((c, c)), full((1, c)), full((1, c)),
        ],
        out_specs=pl.BlockSpec((1, lp, c), lambda bi: (bi, 0, 0)),
        out_shape=jax.ShapeDtypeStruct((b, lp, c), BF16),
        scratch_shapes=[pltpu.VMEM((lp + CONV_PAD, c), F32), pltpu.VMEM((CONV_ROWS, c), F32)],
        compiler_params=pltpu.CompilerParams(
            dimension_semantics=("arbitrary",), vmem_limit_bytes=56 * 1024 * 1024),
        name="conformer_conv",
    )(u, conv_w, conv_b, gavg, gn_g, gn_b)


def _store_row_tiles(ref, val):
    for k in range(ROW_TILES):
        ref[:, k, :] = val[:, k * LANES:(k + 1) * LANES]


def _outproj_kernel(a_ref, c_ref, h_ref, wo_ref, g_ref, wq_ref, h1_ref, xn_ref, qp_ref):
    mix = (jnp.dot(a_ref[...], wo_ref[0:ATTN_WIDTH, :], preferred_element_type=F32)
           + jnp.dot(c_ref[...], wo_ref[ATTN_WIDTH:, :], preferred_element_type=F32))
    h1 = h_ref[...] + mix
    h1_ref[...] = h1
    ms = jnp.mean(h1 * h1, axis=-1, keepdims=True)
    xn = h1 * lax.rsqrt(ms + EPS) * g_ref[...]
    _store_row_tiles(xn_ref, xn)
    qp = jnp.dot(xn.astype(BF16), wq_ref[...], preferred_element_type=F32)
    for hp in range(2 * PEER_HEADS):
        qp_ref[hp] = qp[:, hp * PEER_KEYS:(hp + 1) * PEER_KEYS].astype(BF16)


def _outproj(attn2d, conv2d, h2d, wo_bf16, g2, wq_bf16):
    t = h2d.shape[0]
    nq = wq_bf16.shape[1]
    return pl.pallas_call(
        _outproj_kernel,
        grid=(t // TM,),
        in_specs=[
            pl.BlockSpec((TM, ATTN_WIDTH), lambda i: (i, 0)),
            pl.BlockSpec((TM, CONV_WIDTH), lambda i: (i, 0)),
            pl.BlockSpec((TM, D_MODEL), lambda i: (i, 0)),
            pl.BlockSpec((D_MODEL, D_MODEL), lambda i: (0, 0)),
            pl.BlockSpec((1, D_MODEL), lambda i: (0, 0)),
            pl.BlockSpec((D_MODEL, nq), lambda i: (0, 0)),
        ],
        out_specs=[
            pl.BlockSpec((TM, D_MODEL), lambda i: (i, 0)),
            pl.BlockSpec((TM, ROW_TILES, LANES), lambda i: (i, 0, 0)),
            pl.BlockSpec((2 * PEER_HEADS, TM, PEER_KEYS), lambda i: (0, i, 0)),
        ],
        out_shape=[
            jax.ShapeDtypeStruct((t, D_MODEL), F32),
            jax.ShapeDtypeStruct((t, ROW_TILES, LANES), F32),
            jax.ShapeDtypeStruct((2 * PEER_HEADS, t, PEER_KEYS), BF16),
        ],
        compiler_params=pltpu.CompilerParams(
            dimension_semantics=("arbitrary",), vmem_limit_bytes=48 * 1024 * 1024),
        name="outproj_peerq",
    )(attn2d, conv2d, h2d, wo_bf16, g2, wq_bf16)


def _merge_network(n):
    pairs = []

    def merge(lo, hi, r):
        step = r * 2
        if step < hi - lo:
            merge(lo, hi, step)
            merge(lo + r, hi, step)
            pairs.extend((i, i + r) for i in range(lo + r, hi - r, step))
        else:
            pairs.append((lo, lo + r))

    def sort(lo, hi):
        if hi - lo >= 1:
            mid = lo + (hi - lo) // 2
            sort(lo, mid)
            sort(mid + 1, hi)
            merge(lo, hi, 1)

    sort(0, n - 1)
    return tuple(pairs)


SORT16 = _merge_network(PEER_KEYS // SUBLANES)


def _sort_lists(vals, ids):
    v, d = list(vals), list(ids)
    for i, j in SORT16:
        swap = (v[j] > v[i]) | ((v[j] == v[i]) & (d[j] < d[i]))
        v[i], v[j] = jnp.where(swap, v[j], v[i]), jnp.where(swap, v[i], v[j])
        d[i], d[j] = jnp.where(swap, d[j], d[i]), jnp.where(swap, d[i], d[j])
    return v, d


def _row_writer(val_ref, idx_ref):
    def emit(r, val, idx):
        val_ref[r:r + 1, :] = val
        idx_ref[r:r + 1, :] = idx
    return emit


def _top_keys(st, emit):
    tokens = st.shape[1]
    sub = lax.broadcasted_iota(I32, (SUBLANES, tokens), 0)
    n = st.shape[0] // SUBLANES
    v, d = _sort_lists([st[k * SUBLANES:(k + 1) * SUBLANES, :] for k in range(n)],
                       [sub + k * SUBLANES for k in range(n)])
    for r in range(PEER_TOPK):
        m = jnp.max(v[0], axis=0, keepdims=True)
        imin = jnp.min(jnp.where(v[0] == m, d[0], st.shape[0]), axis=0, keepdims=True)
        emit(r, m, imin)
        sel = d[0] == imin
        for k in range(PEER_TOPK - 1 - r):
            v[k] = jnp.where(sel, v[k + 1], v[k])
            d[k] = jnp.where(sel, d[k + 1], d[k])


def _top_sums(s1, s2, emit):
    tokens = s1.shape[1]
    sub = lax.broadcasted_iota(I32, (SUBLANES, tokens), 0)
    big = PEER_TOPK * PEER_TOPK
    v = [s1[0:SUBLANES, :] + s2[b:b + 1, :] for b in range(PEER_TOPK)]
    ptr = jnp.zeros((SUBLANES, tokens), I32)
    e = s1[SUBLANES:, :] + s2[0:1, :]
    eid = (sub + SUBLANES) * PEER_TOPK
    for r in range(PEER_TOPK):
        hid = sub * PEER_TOPK + ptr
        m = jnp.max(jnp.maximum(v[0], e), axis=0, keepdims=True)
        cand = jnp.minimum(jnp.where(v[0] == m, hid, big), jnp.where(e == m, eid, big))
        imin = jnp.min(cand, axis=0, keepdims=True)
        emit(r, m, imin)
        sel = hid == imin
        e = jnp.where(eid == imin, NEG, e)
        for k in range(PEER_TOPK - 1 - r):
            v[k] = jnp.where(sel, v[k + 1], v[k])
        ptr = jnp.where(sel, ptr + 1, ptr)


def _topk_kernel(qp_ref, keys_ref, r_ref, sh_ref, gate_ref,
                 s1_ref, i1_ref, s2_ref, i2_ref, ct_ref, ci_ref, e_ref, g_ref):
    def head(h, slot):
        s1_s, i1_s, s2_s, i2_s, ct_s, ci_s = (
            ref.at[slot] for ref in (s1_ref, i1_ref, s2_ref, i2_ref, ct_ref, ci_ref))
        for p, (sv, si) in enumerate(((s1_s, i1_s), (s2_s, i2_s))):
            hp = 2 * h + p
            st = lax.dot_general(keys_ref[hp], qp_ref[hp], (((1,), (1,)), ((), ())),
                                 preferred_element_type=F32)
            _top_keys(st, _row_writer(sv, si))
        _top_sums(s1_s[...], s2_s[...], _row_writer(ct_s, ci_s))
        ct = ct_s[...]
        ci = ci_s[...]
        hi = ci >> 4
        lo = ci & (PEER_TOPK - 1)
        i1 = i1_s[...]
        i2 = i2_s[...]
        e1 = jnp.zeros_like(ci)
        e2 = jnp.zeros_like(ci)
        for a in range(PEER_TOPK):
            e1 = jnp.where(hi == a, i1[a:a + 1, :], e1)
            e2 = jnp.where(lo == a, i2[a:a + 1, :], e2)
        e = e1 * PEER_KEYS + e2
        ex = jnp.exp(ct - jnp.max(ct, axis=0, keepdims=True))
        gate = ex / jnp.sum(ex, axis=0, keepdims=True)
        row0 = pl.multiple_of(h * PEER_TOPK, PEER_TOPK)
        e_ref[pl.ds(row0, PEER_TOPK), :] = e
        g_ref[pl.ds(row0, PEER_TOPK), :] = gate

    def heads(g, _):
        for slot in range(TOPK_HEADS_PER_TRIP):
            head(g * TOPK_HEADS_PER_TRIP + slot, slot)
        return 0

    lax.fori_loop(0, PEER_HEADS // TOPK_HEADS_PER_TRIP, heads, 0)
    e = e_ref[...].T
    r_ref[...] = (e & (HALF_EXPERTS - 1)) * ROW_TILES
    sh_ref[...] = ((e >> 13) << 4).astype(F32)
    gate_ref[...] = g_ref[...].T


def _peer_topk(qp, keys_bf16):
    t = qp.shape[1]
    tt = TOPK_TOKENS
    sc = lambda dt: pltpu.VMEM((TOPK_HEADS_PER_TRIP, PEER_TOPK, tt), dt)
    return pl.pallas_call(
        _topk_kernel,
        grid=(t // tt,),
        in_specs=[
            pl.BlockSpec((2 * PEER_HEADS, tt, PEER_KEYS), lambda i: (0, i, 0)),
            pl.BlockSpec((2 * PEER_HEADS, PEER_KEYS, PEER_KEYS), lambda i: (0, 0, 0)),
        ],
        out_specs=[
            pl.BlockSpec((tt, PEER_PAIRS), lambda i: (i, 0)),
            pl.BlockSpec((tt, PEER_PAIRS), lambda i: (i, 0)),
            pl.BlockSpec((tt, PEER_PAIRS), lambda i: (i, 0)),
        ],
        out_shape=[
            jax.ShapeDtypeStruct((t, PEER_PAIRS), I32),
            jax.ShapeDtypeStruct((t, PEER_PAIRS), F32),
            jax.ShapeDtypeStruct((t, PEER_PAIRS), F32),
        ],
        scratch_shapes=[sc(F32), sc(I32), sc(F32), sc(I32), sc(F32), sc(I32),
                        pltpu.VMEM((PEER_PAIRS, tt), I32), pltpu.VMEM((PEER_PAIRS, tt), F32)],
        compiler_params=pltpu.CompilerParams(dimension_semantics=("arbitrary",)),
        name="peer_topk",
    )(qp, keys_bf16)


def _pack_kernel(hi_ref, lo_ref, o_ref):
    hi = pltpu.bitcast(hi_ref[...].astype(BF16).astype(F32), U32)
    lo = pltpu.bitcast(lo_ref[...].astype(BF16).astype(F32), U32)
    o_ref[...] = hi | (lo >> 16)


def _pack_table(tab):
    rows = 512
    nb = HALF_EXPERTS // rows
    packed = pl.pallas_call(
        _pack_kernel,
        grid=(nb,),
        in_specs=[pl.BlockSpec((rows, D_MODEL), lambda i: (i, 0)),
                  pl.BlockSpec((rows, D_MODEL), lambda i: (i + nb, 0))],
        out_specs=pl.BlockSpec((rows, D_MODEL), lambda i: (i, 0)),
        out_shape=jax.ShapeDtypeStruct((HALF_EXPERTS, D_MODEL), U32),
        compiler_params=pltpu.CompilerParams(dimension_semantics=("arbitrary",)),
        name="pack_table",
    )(tab, tab)
    return packed.reshape(HALF_EXPERTS * ROW_TILES, LANES)


def _expert_row(tab_ref, off, shv):
    word = tab_ref[pl.ds(pl.multiple_of(off, SUBLANES), SUBLANES), :]
    return pltpu.bitcast(jnp.left_shift(word, shv) & jnp.uint32(0xFFFF0000), F32)


def _rows_to_lanes(row):
    return jnp.transpose(jnp.broadcast_to(row, (PEER_PAIRS, LANES)))


def _shift_rows(shf_row):
    ri = lax.broadcasted_iota(I32, (PEER_PAIRS, PEER_PAIRS), 0)
    ci = lax.broadcasted_iota(I32, (PEER_PAIRS, PEER_PAIRS), 1)
    diag = jnp.where(ri == ci, shf_row, 0.0).astype(BF16)
    rep = jnp.dot(diag, jnp.ones((PEER_PAIRS, LANES), BF16), preferred_element_type=F32)
    return pltpu.bitcast(rep, U32) >> 26


def _fold8(prods):
    sub = lax.broadcasted_iota(I32, (SUBLANES, LANES), 0)
    cur = prods
    for sh in (1, 2, 4):
        keep = (sub & sh) == 0
        nxt = []
        for k in range(0, len(cur), 2):
            a = jnp.where(keep, cur[k], cur[k + 1])
            b = jnp.where(keep, cur[k + 1], cur[k])
            nxt.append(a + pltpu.roll(b, sh, axis=0))
        cur = nxt
    return cur[0]


def _peer_act_kernel(*refs):
    off_refs = refs[:OFF_STREAMS]
    shf_ref, x_ref, gate_ref, tab_ref, w_ref, shb_ref, a_ref = refs[OFF_STREAMS:]

    def prep(t, slot):
        shb_ref[slot] = _shift_rows(shf_ref[pl.ds(jnp.minimum(t, PEER_TB - 1), 1), :])

    pair_slot = (lax.broadcasted_iota(I32, (SUBLANES, LANES), 1)
                 - lax.broadcasted_iota(I32, (SUBLANES, LANES), 0))

    def pairs(t, slot):
        xt = x_ref[t]
        t0 = t * (PEER_PAIRS // OFF_STREAMS)
        spread = jnp.zeros((SUBLANES, LANES), F32)
        for g in range(PEER_PAIRS // SUBLANES):
            prods = []
            for jj in range(g * SUBLANES, (g + 1) * SUBLANES):
                off = off_refs[jj % OFF_STREAMS][t0 + jj // OFF_STREAMS]
                f = _expert_row(tab_ref, off, shb_ref[slot, jj:jj + 1, :])
                prods.append(f * xt)
            sums = jnp.sum(_fold8(prods), axis=1, keepdims=True)
            spread = jnp.where(pair_slot == g * SUBLANES, sums, spread)
        a_ref[pl.ds(t, 1), :] = jnp.sum(spread, axis=0, keepdims=True)

    prep(0, 0)

    def tokens(i, _):
        for k in range(ACT_TOKENS_PER_TRIP):
            t = i * ACT_TOKENS_PER_TRIP + k
            prep(t + 1, (k + 1) % 2)
            pairs(t, k % 2)
        return 0

    lax.fori_loop(0, PEER_TB // ACT_TOKENS_PER_TRIP, tokens, 0)
    a = a_ref[...]
    act = 0.5 * a * (1.0 + lax.erf(a * (2.0 ** -0.5)))
    w_ref[...] = gate_ref[...] * act


def _real_token_specs(lp, seq_real):
    tiles = seq_real // PEER_TB

    def start(i):
        return pl.multiple_of((i // tiles) * lp + N_META + (i % tiles) * PEER_TB, N_META)

    per_stream = PEER_PAIRS // OFF_STREAMS
    offsets = [pl.BlockSpec((pl.Element(PEER_TB * per_stream),),
                            lambda i: (pl.multiple_of(start(i) * per_stream, N_META * per_stream),),
                            memory_space=pltpu.SMEM)
               for _ in range(OFF_STREAMS)]
    rows = lambda *tail: pl.BlockSpec((pl.Element(PEER_TB),) + tuple(pl.Element(d) for d in tail),
                                      lambda i: (start(i),) + (0,) * len(tail))
    return offsets, rows


def _peer_act(off, shf, x3, gate, tab, lp, seq_real):
    n_real = (shf.shape[0] // lp) * seq_real
    offsets, rows = _real_token_specs(lp, seq_real)
    return pl.pallas_call(
        _peer_act_kernel,
        grid=(n_real // PEER_TB,),
        in_specs=offsets + [
            rows(PEER_PAIRS),
            rows(ROW_TILES, LANES),
            rows(PEER_PAIRS),
            pl.BlockSpec((HALF_EXPERTS * ROW_TILES, LANES), lambda i: (0, 0),
                         pipeline_mode=pl.Buffered(1)),
        ],
        out_specs=pl.BlockSpec((PEER_TB, PEER_PAIRS), lambda i: (i, 0)),
        out_shape=jax.ShapeDtypeStruct((n_real, PEER_PAIRS), F32),
        scratch_shapes=[pltpu.VMEM((2, PEER_PAIRS, LANES), U32),
                        pltpu.VMEM((PEER_TB, PEER_PAIRS), F32)],
        compiler_params=pltpu.CompilerParams(
            dimension_semantics=("arbitrary",), vmem_limit_bytes=VMEM_TABLE_LIMIT),
        name="peer_act",
    )(*off, shf, x3, gate, tab)


def _peer_out_kernel(*refs):
    off_refs = refs[:OFF_STREAMS]
    shf_ref, w_ref, h_ref, g_ref, tab_ref, o_ref, shb_ref, wb_ref, ffn_ref = refs[OFF_STREAMS:]
    n_acc = 4

    def prep(t, _):
        r0 = pl.multiple_of(t * PEER_PAIRS, PEER_PAIRS)
        shb_ref[pl.ds(r0, PEER_PAIRS), :] = _shift_rows(shf_ref[pl.ds(t, 1), :])
        wb_ref[pl.ds(r0, PEER_PAIRS), :] = _rows_to_lanes(w_ref[pl.ds(t, 1), :])
        return 0

    lax.fori_loop(0, PEER_TB, prep, 0, unroll=PREP_UNROLL)

    def token(t, _):
        j0 = pl.multiple_of(t * PEER_PAIRS, PEER_PAIRS)
        m0 = t * (PEER_PAIRS // OFF_STREAMS)
        accs = [jnp.zeros((SUBLANES, LANES), F32) for _ in range(n_acc)]
        for jj in range(PEER_PAIRS):
            off = off_refs[jj % OFF_STREAMS][m0 + jj // OFF_STREAMS]
            f = _expert_row(tab_ref, off, shb_ref[pl.ds(j0 + jj, 1), :])
            accs[jj % n_acc] = accs[jj % n_acc] + wb_ref[pl.ds(j0 + jj, 1), :] * f
        ffn_ref[t] = (accs[0] + accs[1]) + (accs[2] + accs[3])
        return 0

    lax.fori_loop(0, PEER_TB, token, 0)
    ffn = jnp.concatenate([ffn_ref[:, k, :] for k in range(ROW_TILES)], axis=1)
    y = h_ref[...] + ffn
    ms = jnp.mean(y * y, axis=-1, keepdims=True)
    o_ref[...] = y * lax.rsqrt(ms + EPS) * g_ref[...]


def _peer_out(off, shf, w, h2d, g, tab, lp, seq_real):
    n_real = w.shape[0]
    offsets, rows = _real_token_specs(lp, seq_real)
    return pl.pallas_call(
        _peer_out_kernel,
        grid=(n_real // PEER_TB,),
        in_specs=offsets + [
            rows(PEER_PAIRS),
            pl.BlockSpec((PEER_TB, PEER_PAIRS), lambda i: (i, 0)),
            rows(D_MODEL),
            pl.BlockSpec((1, D_MODEL), lambda i: (0, 0)),
            pl.BlockSpec((HALF_EXPERTS * ROW_TILES, LANES), lambda i: (0, 0),
                         pipeline_mode=pl.Buffered(1)),
        ],
        out_specs=pl.BlockSpec((PEER_TB, D_MODEL), lambda i: (i, 0)),
        out_shape=jax.ShapeDtypeStruct((n_real, D_MODEL), F32),
        scratch_shapes=[pltpu.VMEM((PEER_TB * PEER_PAIRS, LANES), U32),
                        pltpu.VMEM((PEER_TB * PEER_PAIRS, LANES), F32),
                        pltpu.VMEM((PEER_TB, ROW_TILES, LANES), F32)],
        compiler_params=pltpu.CompilerParams(
            dimension_semantics=("arbitrary",), vmem_limit_bytes=VMEM_TABLE_LIMIT),
        name="peer_out",
    )(*off, shf, w, h2d, g, tab)


def kernel(x, meta_tokens, norm1_g, w_in, lambda_q1, lambda_k1, lambda_q2, lambda_k2,
           attn_subln_g, conv_w, conv_b, conv_norm_g, conv_norm_b, w_out, norm2_g,
           peer_wq, peer_subkeys, peer_u, peer_v, final_norm_g):
    b, s, _ = x.shape
    seq = N_META + s
    lp = ((seq + Q_BLOCK - 1) // Q_BLOCK) * Q_BLOCK
    t = b * lp
    assert lp % TQ == 0 and lp % CONV_ROWS == 0 and t % TM == 0 and s % PEER_TB == 0

    meta = jnp.broadcast_to(meta_tokens[None].astype(x.dtype), (b, N_META, D_MODEL))
    h = jnp.concatenate([meta, x, jnp.zeros((b, lp - seq, D_MODEL), x.dtype)], axis=1)
    h2d = h.reshape(t, D_MODEL)

    lam_init = 0.8 - 0.6 * math.exp(-0.3 * 0)
    slopes = jnp.asarray([2.0 ** (-8.0 * (i + 1) / N_HEADS) for i in range(N_HEADS)], F32)
    group = jnp.arange(CONV_WIDTH) // CONV_GROUP
    gavg = ((group[:, None] == group[None, :]).astype(F32) * (1.0 / CONV_GROUP)).astype(BF16)

    q, k, v, u = _inproj(h2d, norm1_g[0][None], w_in[0].astype(BF16))
    attn = _attention(q.reshape(b, lp, -1), k.reshape(b, lp, -1), v.reshape(b, lp, -1), slopes,
                      lambda_q1[0][None], lambda_k1[0][None], lambda_q2[0][None],
                      lambda_k2[0][None], attn_subln_g[0][None], lam_init)
    conv = _conformer_conv(u.reshape(b, lp, -1), conv_w[0], conv_b[0][None], gavg,
                           conv_norm_g[0][None], conv_norm_b[0][None])
    h1, xn2, qp = _outproj(attn.reshape(t, -1), conv.reshape(t, -1), h2d,
                           w_out[0].astype(BF16), norm2_g[0][None], peer_wq[0].astype(BF16))
    keys = peer_subkeys[0].reshape(2 * PEER_HEADS, PEER_KEYS, PEER_KEYS).astype(BF16)
    off, shf, gate = _peer_topk(qp, keys)
    off = [off[:, k::OFF_STREAMS].reshape(-1) for k in range(OFF_STREAMS)]
    w = _peer_act(off, shf, xn2, gate, _pack_table(peer_u[0]), lp, s)
    out = _peer_out(off, shf, w, h1, final_norm_g[None], _pack_table(peer_v[0]), lp, s)
    return out.reshape(b, s, D_MODEL)
```

```python
import functools
import math

import jax
import jax.numpy as jnp
from jax import lax
from jax.experimental import pallas as pl
from jax.experimental.pallas import tpu as pltpu

F32 = jnp.float32
BF16 = jnp.bfloat16
I32 = jnp.int32
U32 = jnp.uint32

D_MODEL = 1024
N_META = 16
Q_BLOCK = 128
ATTN_WIDTH = 512
CONV_WIDTH = 512
N_HEADS = 4
HEAD_DIM = 64
V_DIM = 128
CONV_K = 31
CONV_GROUP = 64
PEER_HEADS = 8
PEER_KEYS = 128
PEER_TOPK = 16
PEER_PAIRS = PEER_HEADS * PEER_TOPK
N_EXPERTS = PEER_KEYS * PEER_KEYS
HALF_EXPERTS = N_EXPERTS // 2
EPS = 1e-6
NEG = -1e30

LANES = 128
SUBLANES = 8
ROW_TILES = D_MODEL // LANES

TM = 512
TQ = 384
CONV_ROWS = 64
CONV_PAD = 32
TOPK_TOKENS = 128
TOPK_HEADS_PER_TRIP = 2
PEER_TB = 64
ACT_TOKENS_PER_TRIP = 4
OFF_STREAMS = 8
PREP_UNROLL = 8
VMEM_TABLE_LIMIT = 52 * 1024 * 1024


def _inproj_kernel(h_ref, g_ref, w_ref, q_ref, k_ref, v_ref, u_ref):
    x = h_ref[...]
    ms = jnp.mean(x * x, axis=-1, keepdims=True)
    xn = (x * lax.rsqrt(ms + EPS) * g_ref[...]).astype(BF16)
    proj = jnp.dot(xn, w_ref[...], preferred_element_type=F32)
    q_ref[...] = (proj[:, 0:ATTN_WIDTH] * (HEAD_DIM ** -0.5)).astype(BF16)
    k_ref[...] = proj[:, ATTN_WIDTH:2 * ATTN_WIDTH].astype(BF16)
    v_ref[...] = proj[:, 2 * ATTN_WIDTH:3 * ATTN_WIDTH].astype(BF16)
    ga = proj[:, 3 * ATTN_WIDTH:3 * ATTN_WIDTH + CONV_WIDTH]
    gg = proj[:, 3 * ATTN_WIDTH + CONV_WIDTH:]
    u_ref[...] = ga * jax.nn.sigmoid(gg)


def _inproj(h2d, g, w_bf16):
    t = h2d.shape[0]
    n_cols = w_bf16.shape[1]
    return pl.pallas_call(
        _inproj_kernel,
        grid=(t // TM,),
        in_specs=[
            pl.BlockSpec((TM, D_MODEL), lambda i: (i, 0)),
            pl.BlockSpec((1, D_MODEL), lambda i: (0, 0)),
            pl.BlockSpec((D_MODEL, n_cols), lambda i: (0, 0)),
        ],
        out_specs=[
            pl.BlockSpec((TM, ATTN_WIDTH), lambda i: (i, 0)),
            pl.BlockSpec((TM, ATTN_WIDTH), lambda i: (i, 0)),
            pl.BlockSpec((TM, ATTN_WIDTH), lambda i: (i, 0)),
            pl.BlockSpec((TM, CONV_WIDTH), lambda i: (i, 0)),
        ],
        out_shape=[
            jax.ShapeDtypeStruct((t, ATTN_WIDTH), BF16),
            jax.ShapeDtypeStruct((t, ATTN_WIDTH), BF16),
            jax.ShapeDtypeStruct((t, ATTN_WIDTH), BF16),
            jax.ShapeDtypeStruct((t, CONV_WIDTH), F32),
        ],
        compiler_params=pltpu.CompilerParams(
            dimension_semantics=("arbitrary",), vmem_limit_bytes=48 * 1024 * 1024),
        name="inproj",
    )(h2d, g, w_bf16)


def _attn_kernel(slopes_ref, lq1_ref, lk1_ref, lq2_ref, lk2_ref, subg_ref,
                 q_ref, k_ref, v_ref, o_ref, *, lam_init):
    hd = pl.program_id(1)
    qi = pl.program_id(2)
    slope = slopes_ref[hd]
    lam = (jnp.exp(jnp.sum(lq1_ref[...] * lk1_ref[...], keepdims=True))
           - jnp.exp(jnp.sum(lq2_ref[...] * lk2_ref[...], keepdims=True)) + lam_init)

    q = q_ref[0]
    lane = lax.broadcasted_iota(I32, q.shape, 1)
    zero = jnp.zeros_like(q)
    qs = jnp.concatenate([jnp.where(lane < HEAD_DIM, q, zero),
                          jnp.where(lane >= HEAD_DIM, q, zero)], axis=0)

    q0 = qi * TQ
    col = lax.broadcasted_iota(I32, (1, TQ), 1)

    def step(j, carry, masked):
        m, l, acc = carry
        k0 = pl.multiple_of(j * TQ, TQ)
        kj = k_ref[0, pl.ds(k0, TQ), :]
        vj = v_ref[0, pl.ds(k0, TQ), :]
        s = lax.dot_general(qs, kj, (((1,), (1,)), ((), ())), preferred_element_type=F32)
        s = s + slope * (col + (k0 - q0)).astype(F32)
        if masked:
            row = lax.broadcasted_iota(I32, (2 * TQ, TQ), 0)
            row = jnp.where(row >= TQ, row - TQ, row)
            cc = lax.broadcasted_iota(I32, (2 * TQ, TQ), 1)
            s = jnp.where(cc <= row, s, NEG)
        m_new = jnp.maximum(m, jnp.max(s, axis=1, keepdims=True))
        alpha = jnp.exp(m - m_new)
        p = jnp.exp(s - m_new)
        l = alpha * l + jnp.sum(p, axis=1, keepdims=True)
        acc = alpha * acc + jnp.dot(p.astype(BF16), vj, preferred_element_type=F32)
        return m_new, l, acc

    init = (jnp.full((2 * TQ, 1), NEG, F32), jnp.zeros((2 * TQ, 1), F32),
            jnp.zeros((2 * TQ, V_DIM), F32))
    carry = lax.fori_loop(0, qi, lambda j, c: step(j, c, False), init)
    m, l, acc = step(qi, carry, True)
    o = acc / l
    a = o[:TQ] - lam * o[TQ:]
    ms = jnp.mean(a * a, axis=-1, keepdims=True)
    y = a * lax.rsqrt(ms + EPS) * subg_ref[...] * (1.0 - lam_init)
    o_ref[0] = y.astype(BF16)


def _attention(q, k, v, slopes, lq1, lk1, lq2, lk2, subg, lam_init):
    b, lp, _ = q.shape
    nq = lp // TQ
    vec = lambda n: pl.BlockSpec((1, n), lambda bi, hi, i: (0, 0))
    return pl.pallas_call(
        functools.partial(_attn_kernel, lam_init=lam_init),
        grid=(b, N_HEADS, nq),
        in_specs=[
            pl.BlockSpec(memory_space=pltpu.SMEM),
            vec(HEAD_DIM), vec(HEAD_DIM), vec(HEAD_DIM), vec(HEAD_DIM), vec(V_DIM),
            pl.BlockSpec((1, TQ, V_DIM), lambda bi, hi, i: (bi, i, hi)),
            pl.BlockSpec((1, lp, V_DIM), lambda bi, hi, i: (bi, 0, hi)),
            pl.BlockSpec((1, lp, V_DIM), lambda bi, hi, i: (bi, 0, hi)),
        ],
        out_specs=pl.BlockSpec((1, TQ, V_DIM), lambda bi, hi, i: (bi, i, hi)),
        out_shape=jax.ShapeDtypeStruct((b, lp, ATTN_WIDTH), BF16),
        compiler_params=pltpu.CompilerParams(
            dimension_semantics=("arbitrary", "arbitrary", "arbitrary"),
            vmem_limit_bytes=48 * 1024 * 1024),
        name="diff_attn",
    )(slopes, lq1, lk1, lq2, lk2, subg, q, k, v)


def _group_mean(v, gavg_bf16):
    hi = v.astype(BF16)
    lo = (v - hi.astype(F32)).astype(BF16)
    return (jnp.dot(hi, gavg_bf16, preferred_element_type=F32)
            + jnp.dot(lo, gavg_bf16, preferred_element_type=F32))


def _conv_kernel(u_ref, w_ref, b_ref, gavg_ref, g_ref, beta_ref, o_ref, upad_ref, y_ref):
    lp = u_ref.shape[1]
    upad_ref[0:CONV_PAD, :] = jnp.zeros((CONV_PAD, CONV_WIDTH), F32)
    upad_ref[CONV_PAD:, :] = u_ref[0]

    def chunk(c, _):
        base = pl.multiple_of(c * CONV_ROWS, CONV_ROWS)
        for lb in range(CONV_WIDTH // LANES):
            ls = slice(lb * LANES, (lb + 1) * LANES)
            acc = jnp.zeros((CONV_ROWS, LANES), F32) + b_ref[:, ls]
            win = upad_ref[pl.ds(base, CONV_ROWS + CONV_PAD), ls]
            shifted = [win] + [jnp.roll(win, -rho, axis=0) for rho in range(1, SUBLANES)]
            for t in range(CONV_K):
                off = CONV_PAD - (CONV_K - 1) + t
                rho = off % SUBLANES
                acc = acc + w_ref[t:t + 1, ls] * shifted[rho][off - rho:off - rho + CONV_ROWS, :]
            y_ref[:, ls] = acc
        y = y_ref[...]
        mu = _group_mean(y, gavg_ref[...])
        d = y - mu
        var = _group_mean(d * d, gavg_ref[...])
        yn = d * lax.rsqrt(var + EPS) * g_ref[...] + beta_ref[...]
        o_ref[0, pl.ds(base, CONV_ROWS), :] = (yn * jax.nn.sigmoid(yn)).astype(BF16)
        return 0

    lax.fori_loop(0, lp // CONV_ROWS, chunk, 0)


def _conformer_conv(u, conv_w, conv_b, gavg, gn_g, gn_b):
    b, lp, c = u.shape
    full = lambda shape: pl.BlockSpec(shape, lambda bi: (0,) * len(shape))
    return pl.pallas_call(
        _conv_kernel,
        grid=(b,),
        in_specs=[
            pl.BlockSpec((1, lp, c), lambda bi: (bi, 0, 0)),
            full((CONV_K, c)), full((1, c)), full((c, c)), full((1, c)), full((1, c)),
        ],
        out_specs=pl.BlockSpec((1, lp, c), lambda bi: (bi, 0, 0)),
        out_shape=jax.ShapeDtypeStruct((b, lp, c), BF16),
        scratch_shapes=[pltpu.VMEM((lp + CONV_PAD, c), F32), pltpu.VMEM((CONV_ROWS, c), F32)],
        compiler_params=pltpu.CompilerParams(
            dimension_semantics=("arbitrary",), vmem_limit_bytes=56 * 1024 * 1024),
        name="conformer_conv",
    )(u, conv_w, conv_b, gavg, gn_g, gn_b)


def _store_row_tiles(ref, val):
    for k in range(ROW_TILES):
        ref[:, k, :] = val[:, k * LANES:(k + 1) * LANES]


def _outproj_kernel(a_ref, c_ref, h_ref, wo_ref, g_ref, wq_ref, h1_ref, xn_ref, qp_ref):
    mix = (jnp.dot(a_ref[...], wo_ref[0:ATTN_WIDTH, :], preferred_element_type=F32)
           + jnp.dot(c_ref[...], wo_ref[ATTN_WIDTH:, :], preferred_element_type=F32))
    h1 = h_ref[...] + mix
    h1_ref[...] = h1
    ms = jnp.mean(h1 * h1, axis=-1, keepdims=True)
    xn = h1 * lax.rsqrt(ms + EPS) * g_ref[...]
    _store_row_tiles(xn_ref, xn)
    qp = jnp.dot(xn.astype(BF16), wq_ref[...], preferred_element_type=F32)
    for hp in range(2 * PEER_HEADS):
        qp_ref[hp] = qp[:, hp * PEER_KEYS:(hp + 1) * PEER_KEYS].astype(BF16)


def _outproj(attn2d, conv2d, h2d, wo_bf16, g2, wq_bf16):
    t = h2d.shape[0]
    nq = wq_bf16.shape[1]
    return pl.pallas_call(
        _outproj_kernel,
        grid=(t // TM,),
        in_specs=[
            pl.BlockSpec((TM, ATTN_WIDTH), lambda i: (i, 0)),
            pl.BlockSpec((TM, CONV_WIDTH), lambda i: (i, 0)),
            pl.BlockSpec((TM, D_MODEL), lambda i: (i, 0)),
            pl.BlockSpec((D_MODEL, D_MODEL), lambda i: (0, 0)),
            pl.BlockSpec((1, D_MODEL), lambda i: (0, 0)),
            pl.BlockSpec((D_MODEL, nq), lambda i: (0, 0)),
        ],
        out_specs=[
            pl.BlockSpec((TM, D_MODEL), lambda i: (i, 0)),
            pl.BlockSpec((TM, ROW_TILES, LANES), lambda i: (i, 0, 0)),
            pl.BlockSpec((2 * PEER_HEADS, TM, PEER_KEYS), lambda i: (0, i, 0)),
        ],
        out_shape=[
            jax.ShapeDtypeStruct((t, D_MODEL), F32),
            jax.ShapeDtypeStruct((t, ROW_TILES, LANES), F32),
            jax.ShapeDtypeStruct((2 * PEER_HEADS, t, PEER_KEYS), BF16),
        ],
        compiler_params=pltpu.CompilerParams(
            dimension_semantics=("arbitrary",), vmem_limit_bytes=48 * 1024 * 1024),
        name="outproj_peerq",
    )(attn2d, conv2d, h2d, wo_bf16, g2, wq_bf16)


def _merge_network(n):
    pairs = []

    def merge(lo, hi, r):
        step = r * 2
        if step < hi - lo:
            merge(lo, hi, step)
            merge(lo + r, hi, step)
            pairs.extend((i, i + r) for i in range(lo + r, hi - r, step))
        else:
            pairs.append((lo, lo + r))

    def sort(lo, hi):
        if hi - lo >= 1:
            mid = lo + (hi - lo) // 2
            sort(lo, mid)
            sort(mid + 1, hi)
            merge(lo, hi, 1)

    sort(0, n - 1)
    return tuple(pairs)


SORT16 = _merge_network(PEER_KEYS // SUBLANES)


def _sort_lists(vals, ids):
    v, d = list(vals), list(ids)
    for i, j in SORT16:
        swap = (v[j] > v[i]) | ((v[j] == v[i]) & (d[j] < d[i]))
        v[i], v[j] = jnp.where(swap, v[j], v[i]), jnp.where(swap, v[i], v[j])
        d[i], d[j] = jnp.where(swap, d[j], d[i]), jnp.where(swap, d[i], d[j])
    return v, d


def _row_writer(val_ref, idx_ref):
    def emit(r, val, idx):
        val_ref[r:r + 1, :] = val
        idx_ref[r:r + 1, :] = idx
    return emit


def _top_keys(st, emit):
    tokens = st.shape[1]
    sub = lax.broadcasted_iota(I32, (SUBLANES, tokens), 0)
    n = st.shape[0] // SUBLANES
    v, d = _sort_lists([st[k * SUBLANES:(k + 1) * SUBLANES, :] for k in range(n)],
                       [sub + k * SUBLANES for k in range(n)])
    for r in range(PEER_TOPK):
        m = jnp.max(v[0], axis=0, keepdims=True)
        imin = jnp.min(jnp.where(v[0] == m, d[0], st.shape[0]), axis=0, keepdims=True)
        emit(r, m, imin)
        sel = d[0] == imin
        for k in range(PEER_TOPK - 1 - r):
            v[k] = jnp.where(sel, v[k + 1], v[k])
            d[k] = jnp.where(sel, d[k + 1], d[k])


def _top_sums(s1, s2, emit):
    tokens = s1.shape[1]
    sub = lax.broadcasted_iota(I32, (SUBLANES, tokens), 0)
    big = PEER_TOPK * PEER_TOPK
    v = [s1[0:SUBLANES, :] + s2[b:b + 1, :] for b in range(PEER_TOPK)]
    ptr = jnp.zeros((SUBLANES, tokens), I32)
    e = s1[SUBLANES:, :] + s2[0:1, :]
    eid = (sub + SUBLANES) * PEER_TOPK
    for r in range(PEER_TOPK):
        hid = sub * PEER_TOPK + ptr
        m = jnp.max(jnp.maximum(v[0], e), axis=0, keepdims=True)
        cand = jnp.minimum(jnp.where(v[0] == m, hid, big), jnp.where(e == m, eid, big))
        imin = jnp.min(cand, axis=0, keepdims=True)
        emit(r, m, imin)
        sel = hid == imin
        e = jnp.where(eid == imin, NEG, e)
        for k in range(PEER_TOPK - 1 - r):
            v[k] = jnp.where(sel, v[k + 1], v[k])
        ptr = jnp.where(sel, ptr + 1, ptr)


def _topk_kernel(qp_ref, keys_ref, r_ref, sh_ref, gate_ref,
                 s1_ref, i1_ref, s2_ref, i2_ref, ct_ref, ci_ref, e_ref, g_ref):
    def head(h, slot):
        s1_s, i1_s, s2_s, i2_s, ct_s, ci_s = (
            ref.at[slot] for ref in (s1_ref, i1_ref, s2_ref, i2_ref, ct_ref, ci_ref))
        for p, (sv, si) in enumerate(((s1_s, i1_s), (s2_s, i2_s))):
            hp = 2 * h + p
            st = lax.dot_general(keys_ref[hp], qp_ref[hp], (((1,), (1,)), ((), ())),
                                 preferred_element_type=F32)
            _top_keys(st, _row_writer(sv, si))
        _top_sums(s1_s[...], s2_s[...], _row_writer(ct_s, ci_s))
        ct = ct_s[...]
        ci = ci_s[...]
        hi = ci >> 4
        lo = ci & (PEER_TOPK - 1)
        i1 = i1_s[...]
        i2 = i2_s[...]
        e1 = jnp.zeros_like(ci)
        e2 = jnp.zeros_like(ci)
        for a in range(PEER_TOPK):
            e1 = jnp.where(hi == a, i1[a:a + 1, :], e1)
            e2 = jnp.where(lo == a, i2[a:a + 1, :], e2)
        e = e1 * PEER_KEYS + e2
        ex = jnp.exp(ct - jnp.max(ct, axis=0, keepdims=True))
        gate = ex / jnp.sum(ex, axis=0, keepdims=True)
        row0 = pl.multiple_of(h * PEER_TOPK, PEER_TOPK)
        e_ref[pl.ds(row0, PEER_TOPK), :] = e
        g_ref[pl.ds(row0, PEER_TOPK), :] = gate

    def heads(g, _):
        for slot in range(TOPK_HEADS_PER_TRIP):
            head(g * TOPK_HEADS_PER_TRIP + slot, slot)
        return 0

    lax.fori_loop(0, PEER_HEADS // TOPK_HEADS_PER_TRIP, heads, 0)
    e = e_ref[...]
    sh_ref[...] = ((e >> 13) << 4).astype(F32).T
    gate_ref[...] = g_ref[...].T
    e_ref[...] = (e & (HALF_EXPERTS - 1)) * ROW_TILES
    for k in range(OFF_STREAMS):
        r_ref[0, k] = e_ref[pl.ds(k, PEER_PAIRS // OFF_STREAMS, stride=OFF_STREAMS), :]


def _real_tile_start(i, tile, lp, seq_real):
    per_batch = seq_real // tile
    return pl.multiple_of((i // per_batch) * lp + N_META + (i % per_batch) * tile, N_META)


def _peer_topk(qp, keys_bf16, lp, seq_real):
    n_real = (qp.shape[1] // lp) * seq_real
    tt = TOPK_TOKENS
    sc = lambda dt: pltpu.VMEM((TOPK_HEADS_PER_TRIP, PEER_TOPK, tt), dt)
    qp_block = tuple(pl.Element(d) for d in (2 * PEER_HEADS, tt, PEER_KEYS))
    return pl.pallas_call(
        _topk_kernel,
        grid=(n_real // tt,),
        in_specs=[
            pl.BlockSpec(qp_block, lambda i: (0, _real_tile_start(i, tt, lp, seq_real), 0)),
            pl.BlockSpec((2 * PEER_HEADS, PEER_KEYS, PEER_KEYS), lambda i: (0, 0, 0)),
        ],
        out_specs=[
            pl.BlockSpec((1, OFF_STREAMS, PEER_PAIRS // OFF_STREAMS, tt), lambda i: (i, 0, 0, 0)),
            pl.BlockSpec((tt, PEER_PAIRS), lambda i: (i, 0)),
            pl.BlockSpec((tt, PEER_PAIRS), lambda i: (i, 0)),
        ],
        out_shape=[
            jax.ShapeDtypeStruct((n_real // tt, OFF_STREAMS, PEER_PAIRS // OFF_STREAMS, tt), I32),
            jax.ShapeDtypeStruct((n_real, PEER_PAIRS), F32),
            jax.ShapeDtypeStruct((n_real, PEER_PAIRS), F32),
        ],
        scratch_shapes=[sc(F32), sc(I32), sc(F32), sc(I32), sc(F32), sc(I32),
                        pltpu.VMEM((PEER_PAIRS, tt), I32), pltpu.VMEM((PEER_PAIRS, tt), F32)],
        compiler_params=pltpu.CompilerParams(dimension_semantics=("arbitrary",)),
        name="peer_topk",
    )(qp, keys_bf16)


def _pack_kernel(hi_ref, lo_ref, o_ref):
    hi = pltpu.bitcast(hi_ref[...].astype(BF16).astype(F32), U32)
    lo = pltpu.bitcast(lo_ref[...].astype(BF16).astype(F32), U32)
    o_ref[...] = hi | (lo >> 16)


def _pack_table(tab):
    rows = 512
    nb = HALF_EXPERTS // rows
    packed = pl.pallas_call(
        _pack_kernel,
        grid=(nb,),
        in_specs=[pl.BlockSpec((rows, D_MODEL), lambda i: (i, 0)),
                  pl.BlockSpec((rows, D_MODEL), lambda i: (i + nb, 0))],
        out_specs=pl.BlockSpec((rows, D_MODEL), lambda i: (i, 0)),
        out_shape=jax.ShapeDtypeStruct((HALF_EXPERTS, D_MODEL), U32),
        compiler_params=pltpu.CompilerParams(dimension_semantics=("arbitrary",)),
        name="pack_table",
    )(tab, tab)
    return packed.reshape(HALF_EXPERTS * ROW_TILES, LANES)


def _expert_row(tab_ref, off, shv):
    word = tab_ref[pl.ds(pl.multiple_of(off, SUBLANES), SUBLANES), :]
    return pltpu.bitcast(jnp.left_shift(word, shv) & jnp.uint32(0xFFFF0000), F32)


def _rows_to_lanes(row):
    return jnp.transpose(jnp.broadcast_to(row, (PEER_PAIRS, LANES)))


def _shift_rows(shf_row):
    ri = lax.broadcasted_iota(I32, (PEER_PAIRS, PEER_PAIRS), 0)
    ci = lax.broadcasted_iota(I32, (PEER_PAIRS, PEER_PAIRS), 1)
    diag = jnp.where(ri == ci, shf_row, 0.0).astype(BF16)
    rep = jnp.dot(diag, jnp.ones((PEER_PAIRS, LANES), BF16), preferred_element_type=F32)
    return pltpu.bitcast(rep, U32) >> 26


def _fold8(prods):
    sub = lax.broadcasted_iota(I32, (SUBLANES, LANES), 0)
    cur = prods
    for sh in (1, 2, 4):
        keep = (sub & sh) == 0
        nxt = []
        for k in range(0, len(cur), 2):
            a = jnp.where(keep, cur[k], cur[k + 1])
            b = jnp.where(keep, cur[k + 1], cur[k])
            nxt.append(a + pltpu.roll(b, sh, axis=0))
        cur = nxt
    return cur[0]


def _peer_act_kernel(*refs):
    off_refs = refs[:OFF_STREAMS]
    shf_ref, x_ref, gate_ref, tab_ref, w_ref, shb_ref, a_ref = refs[OFF_STREAMS:]

    def prep(t, slot):
        shb_ref[slot] = _shift_rows(shf_ref[pl.ds(jnp.minimum(t, PEER_TB - 1), 1), :])

    pair_slot = (lax.broadcasted_iota(I32, (SUBLANES, LANES), 1)
                 - lax.broadcasted_iota(I32, (SUBLANES, LANES), 0))

    def pairs(t, slot):
        xt = x_ref[t]
        t0 = _offset_index(t)
        spread = jnp.zeros((SUBLANES, LANES), F32)
        for g in range(PEER_PAIRS // SUBLANES):
            prods = []
            for jj in range(g * SUBLANES, (g + 1) * SUBLANES):
                off = off_refs[jj % OFF_STREAMS][t0 + (jj // OFF_STREAMS) * TOPK_TOKENS]
                f = _expert_row(tab_ref, off, shb_ref[slot, jj:jj + 1, :])
                prods.append(f * xt)
            sums = jnp.sum(_fold8(prods), axis=1, keepdims=True)
            spread = jnp.where(pair_slot == g * SUBLANES, sums, spread)
        a_ref[pl.ds(t, 1), :] = jnp.sum(spread, axis=0, keepdims=True)

    prep(0, 0)

    def tokens(i, _):
        for k in range(ACT_TOKENS_PER_TRIP):
            t = i * ACT_TOKENS_PER_TRIP + k
            prep(t + 1, (k + 1) % 2)
            pairs(t, k % 2)
        return 0

    lax.fori_loop(0, PEER_TB // ACT_TOKENS_PER_TRIP, tokens, 0)
    a = a_ref[...]
    act = 0.5 * a * (1.0 + lax.erf(a * (2.0 ** -0.5)))
    w_ref[...] = gate_ref[...] * act


def _peer_specs(lp, seq_real):
    per_tile = TOPK_TOKENS // PEER_TB
    stream_len = (PEER_PAIRS // OFF_STREAMS) * TOPK_TOKENS
    offsets = [pl.BlockSpec((stream_len,), lambda i, k=k: ((i // per_tile) * OFF_STREAMS + k,),
                            memory_space=pltpu.SMEM) for k in range(OFF_STREAMS)]
    rows = lambda *tail: pl.BlockSpec(
        (pl.Element(PEER_TB),) + tuple(pl.Element(d) for d in tail),
        lambda i: (_real_tile_start(i, PEER_TB, lp, seq_real),) + (0,) * len(tail))
    return offsets, rows


def _offset_index(t):
    return (pl.program_id(0) % (TOPK_TOKENS // PEER_TB)) * PEER_TB + t


def _peer_act(off, shf, x3, gate, tab, lp, seq_real):
    n_real = shf.shape[0]
    tile = lambda: pl.BlockSpec((PEER_TB, PEER_PAIRS), lambda i: (i, 0))
    offsets, rows = _peer_specs(lp, seq_real)
    return pl.pallas_call(
        _peer_act_kernel,
        grid=(n_real // PEER_TB,),
        in_specs=offsets + [
            tile(),
            rows(ROW_TILES, LANES),
            tile(),
            pl.BlockSpec((HALF_EXPERTS * ROW_TILES, LANES), lambda i: (0, 0),
                         pipeline_mode=pl.Buffered(1)),
        ],
        out_specs=pl.BlockSpec((PEER_TB, PEER_PAIRS), lambda i: (i, 0)),
        out_shape=jax.ShapeDtypeStruct((n_real, PEER_PAIRS), F32),
        scratch_shapes=[pltpu.VMEM((2, PEER_PAIRS, LANES), U32),
                        pltpu.VMEM((PEER_TB, PEER_PAIRS), F32)],
        compiler_params=pltpu.CompilerParams(
            dimension_semantics=("arbitrary",), vmem_limit_bytes=VMEM_TABLE_LIMIT),
        name="peer_act",
    )(*off, shf, x3, gate, tab)


def _peer_out_kernel(*refs):
    off_refs = refs[:OFF_STREAMS]
    shf_ref, w_ref, h_ref, g_ref, tab_ref, o_ref, shb_ref, wb_ref, ffn_ref = refs[OFF_STREAMS:]
    n_acc = 4

    def prep(t, _):
        r0 = pl.multiple_of(t * PEER_PAIRS, PEER_PAIRS)
        shb_ref[pl.ds(r0, PEER_PAIRS), :] = _shift_rows(shf_ref[pl.ds(t, 1), :])
        wb_ref[pl.ds(r0, PEER_PAIRS), :] = _rows_to_lanes(w_ref[pl.ds(t, 1), :])
        return 0

    lax.fori_loop(0, PEER_TB, prep, 0, unroll=PREP_UNROLL)

    def token(t, _):
        j0 = pl.multiple_of(t * PEER_PAIRS, PEER_PAIRS)
        m0 = _offset_index(t)
        accs = [jnp.zeros((SUBLANES, LANES), F32) for _ in range(n_acc)]
        for jj in range(PEER_PAIRS):
            off = off_refs[jj % OFF_STREAMS][m0 + (jj // OFF_STREAMS) * TOPK_TOKENS]
            f = _expert_row(tab_ref, off, shb_ref[pl.ds(j0 + jj, 1), :])
            accs[jj % n_acc] = accs[jj % n_acc] + wb_ref[pl.ds(j0 + jj, 1), :] * f
        ffn_ref[t] = (accs[0] + accs[1]) + (accs[2] + accs[3])
        return 0

    lax.fori_loop(0, PEER_TB, token, 0)
    ffn = jnp.concatenate([ffn_ref[:, k, :] for k in range(ROW_TILES)], axis=1)
    y = h_ref[...] + ffn
    ms = jnp.mean(y * y, axis=-1, keepdims=True)
    o_ref[...] = y * lax.rsqrt(ms + EPS) * g_ref[...]


def _peer_out(off, shf, w, h2d, g, tab, lp, seq_real):
    n_real = w.shape[0]
    offsets, rows = _peer_specs(lp, seq_real)
    return pl.pallas_call(
        _peer_out_kernel,
        grid=(n_real // PEER_TB,),
        in_specs=offsets + [
            pl.BlockSpec((PEER_TB, PEER_PAIRS), lambda i: (i, 0)),
            pl.BlockSpec((PEER_TB, PEER_PAIRS), lambda i: (i, 0)),
            rows(D_MODEL),
            pl.BlockSpec((1, D_MODEL), lambda i: (0, 0)),
            pl.BlockSpec((HALF_EXPERTS * ROW_TILES, LANES), lambda i: (0, 0),
                         pipeline_mode=pl.Buffered(1)),
        ],
        out_specs=pl.BlockSpec((PEER_TB, D_MODEL), lambda i: (i, 0)),
        out_shape=jax.ShapeDtypeStruct((n_real, D_MODEL), F32),
        scratch_shapes=[pltpu.VMEM((PEER_TB * PEER_PAIRS, LANES), U32),
                        pltpu.VMEM((PEER_TB * PEER_PAIRS, LANES), F32),
                        pltpu.VMEM((PEER_TB, ROW_TILES, LANES), F32)],
        compiler_params=pltpu.CompilerParams(
            dimension_semantics=("arbitrary",), vmem_limit_bytes=VMEM_TABLE_LIMIT),
        name="peer_out",
    )(*off, shf, w, h2d, g, tab)


def kernel(x, meta_tokens, norm1_g, w_in, lambda_q1, lambda_k1, lambda_q2, lambda_k2,
           attn_subln_g, conv_w, conv_b, conv_norm_g, conv_norm_b, w_out, norm2_g,
           peer_wq, peer_subkeys, peer_u, peer_v, final_norm_g):
    b, s, _ = x.shape
    seq = N_META + s
    lp = ((seq + Q_BLOCK - 1) // Q_BLOCK) * Q_BLOCK
    t = b * lp
    assert lp % TQ == 0 and lp % CONV_ROWS == 0 and t % TM == 0
    assert s % TOPK_TOKENS == 0 and TOPK_TOKENS % PEER_TB == 0

    meta = jnp.broadcast_to(meta_tokens[None].astype(x.dtype), (b, N_META, D_MODEL))
    h = jnp.concatenate([meta, x, jnp.zeros((b, lp - seq, D_MODEL), x.dtype)], axis=1)
    h2d = h.reshape(t, D_MODEL)

    lam_init = 0.8 - 0.6 * math.exp(-0.3 * 0)
    slopes = jnp.asarray([2.0 ** (-8.0 * (i + 1) / N_HEADS) for i in range(N_HEADS)], F32)
    group = jnp.arange(CONV_WIDTH) // CONV_GROUP
    gavg = ((group[:, None] == group[None, :]).astype(F32) * (1.0 / CONV_GROUP)).astype(BF16)

    q, k, v, u = _inproj(h2d, norm1_g[0][None], w_in[0].astype(BF16))
    attn = _attention(q.reshape(b, lp, -1), k.reshape(b, lp, -1), v.reshape(b, lp, -1), slopes,
                      lambda_q1[0][None], lambda_k1[0][None], lambda_q2[0][None],
                      lambda_k2[0][None], attn_subln_g[0][None], lam_init)
    conv = _conformer_conv(u.reshape(b, lp, -1), conv_w[0], conv_b[0][None], gavg,
                           conv_norm_g[0][None], conv_norm_b[0][None])
    h1, xn2, qp = _outproj(attn.reshape(t, -1), conv.reshape(t, -1), h2d,
                           w_out[0].astype(BF16), norm2_g[0][None], peer_wq[0].astype(BF16))
    keys = peer_subkeys[0].reshape(2 * PEER_HEADS, PEER_KEYS, PEER_KEYS).astype(BF16)
    off, shf, gate = _peer_topk(qp, keys, lp, s)
    off = [off.reshape(-1)] * OFF_STREAMS
    w = _peer_act(off, shf, xn2, gate, _pack_table(peer_u[0]), lp, s)
    out = _peer_out(off, shf, w, h1, final_norm_g[None], _pack_table(peer_v[0]), lp, s)
    return out.reshape(b, s, D_MODEL)
```

```python
import functools
import math

import jax
import jax.numpy as jnp
from jax import lax
from jax.experimental import pallas as pl
from jax.experimental.pallas import tpu as pltpu

F32 = jnp.float32
BF16 = jnp.bfloat16
I32 = jnp.int32
U32 = jnp.uint32

D_MODEL = 1024
N_META = 16
Q_BLOCK = 128
ATTN_WIDTH = 512
CONV_WIDTH = 512
N_HEADS = 4
HEAD_DIM = 64
V_DIM = 128
CONV_K = 31
CONV_GROUP = 64
PEER_HEADS = 8
PEER_KEYS = 128
PEER_TOPK = 16
PEER_PAIRS = PEER_HEADS * PEER_TOPK
N_EXPERTS = PEER_KEYS * PEER_KEYS
HALF_EXPERTS = N_EXPERTS // 2
EPS = 1e-6
NEG = -1e30
LOG2E = math.log2(math.e)

LANES = 128
SUBLANES = 8
ROW_TILES = D_MODEL // LANES

TM = 512
TQ = 384
CONV_ROWS = 64
CONV_PAD = 32
TOPK_TOKENS = 128
TOPK_HEADS_PER_TRIP = 8
PEER_TB = 64
ACT_TOKENS_PER_TRIP = 4
OFF_STREAMS = 8
PREP_UNROLL = 8
VMEM_TABLE_LIMIT = 52 * 1024 * 1024


def _inproj_kernel(h_ref, g_ref, w_ref, q_ref, k_ref, v_ref, u_ref):
    x = h_ref[...]
    ms = jnp.mean(x * x, axis=-1, keepdims=True)
    xn = (x * lax.rsqrt(ms + EPS) * g_ref[...]).astype(BF16)
    proj = jnp.dot(xn, w_ref[...], preferred_element_type=F32)
    q_ref[...] = (proj[:, 0:ATTN_WIDTH] * (HEAD_DIM ** -0.5 * LOG2E)).astype(BF16)
    k_ref[...] = proj[:, ATTN_WIDTH:2 * ATTN_WIDTH].astype(BF16)
    v_ref[...] = proj[:, 2 * ATTN_WIDTH:3 * ATTN_WIDTH].astype(BF16)
    ga = proj[:, 3 * ATTN_WIDTH:3 * ATTN_WIDTH + CONV_WIDTH]
    gg = proj[:, 3 * ATTN_WIDTH + CONV_WIDTH:]
    u_ref[...] = ga * jax.nn.sigmoid(gg)


def _inproj(h2d, g, w_bf16):
    t = h2d.shape[0]
    n_cols = w_bf16.shape[1]
    return pl.pallas_call(
        _inproj_kernel,
        grid=(t // TM,),
        in_specs=[
            pl.BlockSpec((TM, D_MODEL), lambda i: (i, 0)),
            pl.BlockSpec((1, D_MODEL), lambda i: (0, 0)),
            pl.BlockSpec((D_MODEL, n_cols), lambda i: (0, 0)),
        ],
        out_specs=[
            pl.BlockSpec((TM, ATTN_WIDTH), lambda i: (i, 0)),
            pl.BlockSpec((TM, ATTN_WIDTH), lambda i: (i, 0)),
            pl.BlockSpec((TM, ATTN_WIDTH), lambda i: (i, 0)),
            pl.BlockSpec((TM, CONV_WIDTH), lambda i: (i, 0)),
        ],
        out_shape=[
            jax.ShapeDtypeStruct((t, ATTN_WIDTH), BF16),
            jax.ShapeDtypeStruct((t, ATTN_WIDTH), BF16),
            jax.ShapeDtypeStruct((t, ATTN_WIDTH), BF16),
            jax.ShapeDtypeStruct((t, CONV_WIDTH), F32),
        ],
        compiler_params=pltpu.CompilerParams(
            dimension_semantics=("arbitrary",), vmem_limit_bytes=48 * 1024 * 1024),
        name="inproj",
    )(h2d, g, w_bf16)


def _attn_kernel(slopes_ref, lq1_ref, lk1_ref, lq2_ref, lk2_ref, subg_ref,
                 q_ref, k_ref, v_ref, o_ref, *, lam_init):
    hd = pl.program_id(1)
    qi = pl.program_id(2)
    slope = slopes_ref[hd]
    lam = (jnp.exp(jnp.sum(lq1_ref[...] * lk1_ref[...], keepdims=True))
           - jnp.exp(jnp.sum(lq2_ref[...] * lk2_ref[...], keepdims=True)) + lam_init)

    q = q_ref[0]
    lane = lax.broadcasted_iota(I32, q.shape, 1)
    zero = jnp.zeros_like(q)
    qs = jnp.concatenate([jnp.where(lane < HEAD_DIM, q, zero),
                          jnp.where(lane >= HEAD_DIM, q, zero)], axis=0)

    q0 = qi * TQ
    col = lax.broadcasted_iota(I32, (1, TQ), 1)

    def step(j, carry, masked):
        m, l, acc = carry
        k0 = pl.multiple_of(j * TQ, TQ)
        kj = k_ref[0, pl.ds(k0, TQ), :]
        vj = v_ref[0, pl.ds(k0, TQ), :]
        s = lax.dot_general(qs, kj, (((1,), (1,)), ((), ())), preferred_element_type=F32)
        s = s + (slope * LOG2E) * (col + (k0 - q0)).astype(F32)
        if masked:
            row = lax.broadcasted_iota(I32, (2 * TQ, TQ), 0)
            row = jnp.where(row >= TQ, row - TQ, row)
            cc = lax.broadcasted_iota(I32, (2 * TQ, TQ), 1)
            s = jnp.where(cc <= row, s, NEG)
        m_new = jnp.maximum(m, jnp.max(s, axis=1, keepdims=True))
        alpha = jnp.exp2(m - m_new)
        p = jnp.exp2(s - m_new)
        l = alpha * l + jnp.sum(p, axis=1, keepdims=True)
        acc = alpha * acc + jnp.dot(p.astype(BF16), vj, preferred_element_type=F32)
        return m_new, l, acc

    init = (jnp.full((2 * TQ, 1), NEG, F32), jnp.zeros((2 * TQ, 1), F32),
            jnp.zeros((2 * TQ, V_DIM), F32))
    carry = lax.fori_loop(0, qi, lambda j, c: step(j, c, False), init)
    m, l, acc = step(qi, carry, True)
    o = acc / l
    a = o[:TQ] - lam * o[TQ:]
    ms = jnp.mean(a * a, axis=-1, keepdims=True)
    y = a * lax.rsqrt(ms + EPS) * subg_ref[...] * (1.0 - lam_init)
    o_ref[0] = y.astype(BF16)


def _attention(q, k, v, slopes, lq1, lk1, lq2, lk2, subg, lam_init):
    b, lp, _ = q.shape
    nq = lp // TQ
    vec = lambda n: pl.BlockSpec((1, n), lambda bi, hi, i: (0, 0))
    return pl.pallas_call(
        functools.partial(_attn_kernel, lam_init=lam_init),
        grid=(b, N_HEADS, nq),
        in_specs=[
            pl.BlockSpec(memory_space=pltpu.SMEM),
            vec(HEAD_DIM), vec(HEAD_DIM), vec(HEAD_DIM), vec(HEAD_DIM), vec(V_DIM),
            pl.BlockSpec((1, TQ, V_DIM), lambda bi, hi, i: (bi, i, hi)),
            pl.BlockSpec((1, lp, V_DIM), lambda bi, hi, i: (bi, 0, hi)),
            pl.BlockSpec((1, lp, V_DIM), lambda bi, hi, i: (bi, 0, hi)),
        ],
        out_specs=pl.BlockSpec((1, TQ, V_DIM), lambda bi, hi, i: (bi, i, hi)),
        out_shape=jax.ShapeDtypeStruct((b, lp, ATTN_WIDTH), BF16),
        compiler_params=pltpu.CompilerParams(
            dimension_semantics=("arbitrary", "arbitrary", "arbitrary"),
            vmem_limit_bytes=48 * 1024 * 1024),
        name="diff_attn",
    )(slopes, lq1, lk1, lq2, lk2, subg, q, k, v)


def _group_mean(v, gavg_bf16):
    hi = v.astype(BF16)
    lo = (v - hi.astype(F32)).astype(BF16)
    return (jnp.dot(hi, gavg_bf16, preferred_element_type=F32)
            + jnp.dot(lo, gavg_bf16, preferred_element_type=F32))


def _conv_kernel(u_ref, w_ref, b_ref, gavg_ref, g_ref, beta_ref, o_ref, upad_ref, y_ref):
    lp = u_ref.shape[1]
    upad_ref[0:CONV_PAD, :] = jnp.zeros((CONV_PAD, CONV_WIDTH), F32)
    upad_ref[CONV_PAD:, :] = u_ref[0]

    def chunk(c, _):
        base = pl.multiple_of(c * CONV_ROWS, CONV_ROWS)
        for lb in range(CONV_WIDTH // LANES):
            ls = slice(lb * LANES, (lb + 1) * LANES)
            acc = jnp.zeros((CONV_ROWS, LANES), F32) + b_ref[:, ls]
            win = upad_ref[pl.ds(base, CONV_ROWS + CONV_PAD), ls]
            shifted = [win] + [jnp.roll(win, -rho, axis=0) for rho in range(1, SUBLANES)]
            for t in range(CONV_K):
                off = CONV_PAD - (CONV_K - 1) + t
                rho = off % SUBLANES
                acc = acc + w_ref[t:t + 1, ls] * shifted[rho][off - rho:off - rho + CONV_ROWS, :]
            y_ref[:, ls] = acc
        y = y_ref[...]
        mu = _group_mean(y, gavg_ref[...])
        d = y - mu
        var = _group_mean(d * d, gavg_ref[...])
        yn = d * lax.rsqrt(var + EPS) * g_ref[...] + beta_ref[...]
        o_ref[0, pl.ds(base, CONV_ROWS), :] = (yn * jax.nn.sigmoid(yn)).astype(BF16)
        return 0

    lax.fori_loop(0, lp // CONV_ROWS, chunk, 0)


def _conformer_conv(u, conv_w, conv_b, gavg, gn_g, gn_b):
    b, lp, c = u.shape
    full = lambda shape: pl.BlockSpec(shape, lambda bi: (0,) * len(shape))
    return pl.pallas_call(
        _conv_kernel,
        grid=(b,),
        in_specs=[
            pl.BlockSpec((1, lp, c), lambda bi: (bi, 0, 0)),
            full((CONV_K, c)), full((1, c)), full((c, c)), full((1, c)), full((1, c)),
        ],
        out_specs=pl.BlockSpec((1, lp, c), lambda bi: (bi, 0, 0)),
        out_shape=jax.ShapeDtypeStruct((b, lp, c), BF16),
        scratch_shapes=[pltpu.VMEM((lp + CONV_PAD, c), F32), pltpu.VMEM((CONV_ROWS, c), F32)],
        compiler_params=pltpu.CompilerParams(
            dimension_semantics=("arbitrary",), vmem_limit_bytes=56 * 1024 * 1024),
        name="conformer_conv",
    )(u, conv_w, conv_b, gavg, gn_g, gn_b)


def _store_row_tiles(ref, val):
    for k in range(ROW_TILES):
        ref[:, k, :] = val[:, k * LANES:(k + 1) * LANES]


def _outproj_kernel(a_ref, c_ref, h_ref, wo_ref, g_ref, wq_ref, h1_ref, xn_ref, qp_ref):
    mix = (jnp.dot(a_ref[...], wo_ref[0:ATTN_WIDTH, :], preferred_element_type=F32)
           + jnp.dot(c_ref[...], wo_ref[ATTN_WIDTH:, :], preferred_element_type=F32))
    h1 = h_ref[...] + mix
    h1_ref[...] = h1
    ms = jnp.mean(h1 * h1, axis=-1, keepdims=True)
    xn = h1 * lax.rsqrt(ms + EPS) * g_ref[...]
    _store_row_tiles(xn_ref, xn)
    qp = jnp.dot(xn.astype(BF16), wq_ref[...], preferred_element_type=F32)
    for hp in range(2 * PEER_HEADS):
        qp_ref[hp] = qp[:, hp * PEER_KEYS:(hp + 1) * PEER_KEYS].astype(BF16)


def _outproj(attn2d, conv2d, h2d, wo_bf16, g2, wq_bf16):
    t = h2d.shape[0]
    nq = wq_bf16.shape[1]
    return pl.pallas_call(
        _outproj_kernel,
        grid=(t // TM,),
        in_specs=[
            pl.BlockSpec((TM, ATTN_WIDTH), lambda i: (i, 0)),
            pl.BlockSpec((TM, CONV_WIDTH), lambda i: (i, 0)),
            pl.BlockSpec((TM, D_MODEL), lambda i: (i, 0)),
            pl.BlockSpec((D_MODEL, D_MODEL), lambda i: (0, 0)),
            pl.BlockSpec((1, D_MODEL), lambda i: (0, 0)),
            pl.BlockSpec((D_MODEL, nq), lambda i: (0, 0)),
        ],
        out_specs=[
            pl.BlockSpec((TM, D_MODEL), lambda i: (i, 0)),
            pl.BlockSpec((TM, ROW_TILES, LANES), lambda i: (i, 0, 0)),
            pl.BlockSpec((2 * PEER_HEADS, TM, PEER_KEYS), lambda i: (0, i, 0)),
        ],
        out_shape=[
            jax.ShapeDtypeStruct((t, D_MODEL), F32),
            jax.ShapeDtypeStruct((t, ROW_TILES, LANES), F32),
            jax.ShapeDtypeStruct((2 * PEER_HEADS, t, PEER_KEYS), BF16),
        ],
        compiler_params=pltpu.CompilerParams(
            dimension_semantics=("arbitrary",), vmem_limit_bytes=48 * 1024 * 1024),
        name="outproj_peerq",
    )(attn2d, conv2d, h2d, wo_bf16, g2, wq_bf16)


def _merge_network(n):
    pairs = []

    def merge(lo, hi, r):
        step = r * 2
        if step < hi - lo:
            merge(lo, hi, step)
            merge(lo + r, hi, step)
            pairs.extend((i, i + r) for i in range(lo + r, hi - r, step))
        else:
            pairs.append((lo, lo + r))

    def sort(lo, hi):
        if hi - lo >= 1:
            mid = lo + (hi - lo) // 2
            sort(lo, mid)
            sort(mid + 1, hi)
            merge(lo, hi, 1)

    sort(0, n - 1)
    return tuple(pairs)


SORT16 = _merge_network(PEER_KEYS // SUBLANES)


def _sort_lists(vals, ids):
    v, d = list(vals), list(ids)
    for i, j in SORT16:
        swap = (v[j] > v[i]) | ((v[j] == v[i]) & (d[j] < d[i]))
        v[i], v[j] = jnp.where(swap, v[j], v[i]), jnp.where(swap, v[i], v[j])
        d[i], d[j] = jnp.where(swap, d[j], d[i]), jnp.where(swap, d[i], d[j])
    return v, d


def _row_writer(val_ref, idx_ref):
    def emit(r, val, idx):
        val_ref[r:r + 1, :] = val
        idx_ref[r:r + 1, :] = idx
    return emit


def _top_keys(st, emit):
    tokens = st.shape[1]
    sub = lax.broadcasted_iota(I32, (SUBLANES, tokens), 0)
    n = st.shape[0] // SUBLANES
    v, d = _sort_lists([st[k * SUBLANES:(k + 1) * SUBLANES, :] for k in range(n)],
                       [sub + k * SUBLANES for k in range(n)])
    for r in range(PEER_TOPK):
        m = jnp.max(v[0], axis=0, keepdims=True)
        imin = jnp.min(jnp.where(v[0] == m, d[0], st.shape[0]), axis=0, keepdims=True)
        emit(r, m, imin)
        sel = d[0] == imin
        for k in range(PEER_TOPK - 1 - r):
            v[k] = jnp.where(sel, v[k + 1], v[k])
            d[k] = jnp.where(sel, d[k + 1], d[k])


def _top_sums(s1, s2, emit):
    tokens = s1.shape[1]
    sub = lax.broadcasted_iota(I32, (SUBLANES, tokens), 0)
    big = PEER_TOPK * PEER_TOPK
    v = [s1[0:SUBLANES, :] + s2[b:b + 1, :] for b in range(PEER_TOPK)]
    ptr = jnp.zeros((SUBLANES, tokens), I32)
    e = s1[SUBLANES:, :] + s2[0:1, :]
    eid = (sub + SUBLANES) * PEER_TOPK
    for r in range(PEER_TOPK):
        hid = sub * PEER_TOPK + ptr
        m = jnp.max(jnp.maximum(v[0], e), axis=0, keepdims=True)
        cand = jnp.minimum(jnp.where(v[0] == m, hid, big), jnp.where(e == m, eid, big))
        imin = jnp.min(cand, axis=0, keepdims=True)
        emit(r, m, imin)
        sel = hid == imin
        e = jnp.where(eid == imin, NEG, e)
        for k in range(PEER_TOPK - 1 - r):
            v[k] = jnp.where(sel, v[k + 1], v[k])
        ptr = jnp.where(sel, ptr + 1, ptr)


def _topk_kernel(qp_ref, keys_ref, r_ref, sh_ref, gate_ref,
                 s1_ref, i1_ref, s2_ref, i2_ref, ct_ref, ci_ref, e_ref, g_ref):
    def head(h, slot):
        s1_s, i1_s, s2_s, i2_s, ct_s, ci_s = (
            ref.at[slot] for ref in (s1_ref, i1_ref, s2_ref, i2_ref, ct_ref, ci_ref))
        for p, (sv, si) in enumerate(((s1_s, i1_s), (s2_s, i2_s))):
            hp = 2 * h + p
            st = lax.dot_general(keys_ref[hp], qp_ref[hp], (((1,), (1,)), ((), ())),
                                 preferred_element_type=F32)
            _top_keys(st, _row_writer(sv, si))
        _top_sums(s1_s[...], s2_s[...], _row_writer(ct_s, ci_s))
        ct = ct_s[...]
        ci = ci_s[...]
        hi = ci >> 4
        lo = ci & (PEER_TOPK - 1)
        i1 = i1_s[...]
        i2 = i2_s[...]
        e1 = jnp.zeros_like(ci)
        e2 = jnp.zeros_like(ci)
        for a in range(PEER_TOPK):
            e1 = jnp.where(hi == a, i1[a:a + 1, :], e1)
            e2 = jnp.where(lo == a, i2[a:a + 1, :], e2)
        e = e1 * PEER_KEYS + e2
        ex = jnp.exp(ct - jnp.max(ct, axis=0, keepdims=True))
        gate = ex / jnp.sum(ex, axis=0, keepdims=True)
        row0 = pl.multiple_of(h * PEER_TOPK, PEER_TOPK)
        e_ref[pl.ds(row0, PEER_TOPK), :] = e
        g_ref[pl.ds(row0, PEER_TOPK), :] = gate

    def heads(g, _):
        for slot in range(TOPK_HEADS_PER_TRIP):
            head(g * TOPK_HEADS_PER_TRIP + slot, slot)
        return 0

    lax.fori_loop(0, PEER_HEADS // TOPK_HEADS_PER_TRIP, heads, 0)
    e = e_ref[...]
    sh_ref[...] = ((e >> 13) << 4).astype(F32).T
    gate_ref[...] = g_ref[...].T
    e_ref[...] = (e & (HALF_EXPERTS - 1)) * ROW_TILES
    for k in range(OFF_STREAMS):
        r_ref[0, k] = e_ref[pl.ds(k, PEER_PAIRS // OFF_STREAMS, stride=OFF_STREAMS), :]


def _real_tile_start(i, tile, lp, seq_real):
    per_batch = seq_real // tile
    return pl.multiple_of((i // per_batch) * lp + N_META + (i % per_batch) * tile, N_META)


def _peer_topk(qp, keys_bf16, lp, seq_real):
    n_real = (qp.shape[1] // lp) * seq_real
    tt = TOPK_TOKENS
    sc = lambda dt: pltpu.VMEM((TOPK_HEADS_PER_TRIP, PEER_TOPK, tt), dt)
    qp_block = tuple(pl.Element(d) for d in (2 * PEER_HEADS, tt, PEER_KEYS))
    return pl.pallas_call(
        _topk_kernel,
        grid=(n_real // tt,),
        in_specs=[
            pl.BlockSpec(qp_block, lambda i: (0, _real_tile_start(i, tt, lp, seq_real), 0)),
            pl.BlockSpec((2 * PEER_HEADS, PEER_KEYS, PEER_KEYS), lambda i: (0, 0, 0)),
        ],
        out_specs=[
            pl.BlockSpec((1, OFF_STREAMS, PEER_PAIRS // OFF_STREAMS, tt), lambda i: (i, 0, 0, 0)),
            pl.BlockSpec((tt, PEER_PAIRS), lambda i: (i, 0)),
            pl.BlockSpec((tt, PEER_PAIRS), lambda i: (i, 0)),
        ],
        out_shape=[
            jax.ShapeDtypeStruct((n_real // tt, OFF_STREAMS, PEER_PAIRS // OFF_STREAMS, tt), I32),
            jax.ShapeDtypeStruct((n_real, PEER_PAIRS), F32),
            jax.ShapeDtypeStruct((n_real, PEER_PAIRS), F32),
        ],
        scratch_shapes=[sc(F32), sc(I32), sc(F32), sc(I32), sc(F32), sc(I32),
                        pltpu.VMEM((PEER_PAIRS, tt), I32), pltpu.VMEM((PEER_PAIRS, tt), F32)],
        compiler_params=pltpu.CompilerParams(dimension_semantics=("arbitrary",)),
        name="peer_topk",
    )(qp, keys_bf16)


def _pack_kernel(hi_ref, lo_ref, o_ref):
    hi = pltpu.bitcast(hi_ref[...].astype(BF16).astype(F32), U32)
    lo = pltpu.bitcast(lo_ref[...].astype(BF16).astype(F32), U32)
    o_ref[...] = hi | (lo >> 16)


def _pack_table(tab):
    rows = 512
    nb = HALF_EXPERTS // rows
    packed = pl.pallas_call(
        _pack_kernel,
        grid=(nb,),
        in_specs=[pl.BlockSpec((rows, D_MODEL), lambda i: (i, 0)),
                  pl.BlockSpec((rows, D_MODEL), lambda i: (i + nb, 0))],
        out_specs=pl.BlockSpec((rows, D_MODEL), lambda i: (i, 0)),
        out_shape=jax.ShapeDtypeStruct((HALF_EXPERTS, D_MODEL), U32),
        compiler_params=pltpu.CompilerParams(dimension_semantics=("arbitrary",)),
        name="pack_table",
    )(tab, tab)
    return packed.reshape(HALF_EXPERTS * ROW_TILES, LANES)


def _expert_row(tab_ref, off, shv):
    word = tab_ref[pl.ds(pl.multiple_of(off, SUBLANES), SUBLANES), :]
    return pltpu.bitcast(jnp.left_shift(word, shv) & jnp.uint32(0xFFFF0000), F32)


def _rows_to_lanes(row):
    return jnp.transpose(jnp.broadcast_to(row, (PEER_PAIRS, LANES)))


def _shift_rows(shf_row):
    ri = lax.broadcasted_iota(I32, (PEER_PAIRS, PEER_PAIRS), 0)
    ci = lax.broadcasted_iota(I32, (PEER_PAIRS, PEER_PAIRS), 1)
    diag = jnp.where(ri == ci, shf_row, 0.0).astype(BF16)
    rep = jnp.dot(diag, jnp.ones((PEER_PAIRS, LANES), BF16), preferred_element_type=F32)
    return pltpu.bitcast(rep, U32) >> 26


def _fold8(prods):
    sub = lax.broadcasted_iota(I32, (SUBLANES, LANES), 0)
    cur = prods
    for sh in (1, 2, 4):
        keep = (sub & sh) == 0
        nxt = []
        for k in range(0, len(cur), 2):
            a = jnp.where(keep, cur[k], cur[k + 1])
            b = jnp.where(keep, cur[k + 1], cur[k])
            nxt.append(a + pltpu.roll(b, sh, axis=0))
        cur = nxt
    return cur[0]


def _peer_act_kernel(*refs):
    off_refs = refs[:OFF_STREAMS]
    shf_ref, x_ref, gate_ref, tab_ref, w_ref, shb_ref, a_ref = refs[OFF_STREAMS:]

    def prep(t, slot):
        shb_ref[slot] = _shift_rows(shf_ref[pl.ds(jnp.minimum(t, PEER_TB - 1), 1), :])

    pair_slot = (lax.broadcasted_iota(I32, (SUBLANES, LANES), 1)
                 - lax.broadcasted_iota(I32, (SUBLANES, LANES), 0))

    def pairs(t, slot):
        xt = x_ref[t]
        t0 = _offset_index(t)
        spread = jnp.zeros((SUBLANES, LANES), F32)
        for g in range(PEER_PAIRS // SUBLANES):
            prods = []
            for jj in range(g * SUBLANES, (g + 1) * SUBLANES):
                off = off_refs[jj % OFF_STREAMS][t0 + (jj // OFF_STREAMS) * TOPK_TOKENS]
                f = _expert_row(tab_ref, off, shb_ref[slot, jj:jj + 1, :])
                prods.append(f * xt)
            sums = jnp.sum(_fold8(prods), axis=1, keepdims=True)
            spread = jnp.where(pair_slot == g * SUBLANES, sums, spread)
        a_ref[pl.ds(t, 1), :] = jnp.sum(spread, axis=0, keepdims=True)

    prep(0, 0)

    def tokens(i, _):
        for k in range(ACT_TOKENS_PER_TRIP):
            t = i * ACT_TOKENS_PER_TRIP + k
            prep(t + 1, (k + 1) % 2)
            pairs(t, k % 2)
        return 0

    lax.fori_loop(0, PEER_TB // ACT_TOKENS_PER_TRIP, tokens, 0)
    a = a_ref[...]
    act = 0.5 * a * (1.0 + lax.erf(a * (2.0 ** -0.5)))
    w_ref[...] = gate_ref[...] * act


def _peer_specs(lp, seq_real):
    per_tile = TOPK_TOKENS // PEER_TB
    stream_len = (PEER_PAIRS // OFF_STREAMS) * TOPK_TOKENS
    offsets = [pl.BlockSpec((stream_len,), lambda i, k=k: ((i // per_tile) * OFF_STREAMS + k,),
                            memory_space=pltpu.SMEM) for k in range(OFF_STREAMS)]
    rows = lambda *tail: pl.BlockSpec(
        (pl.Element(PEER_TB),) + tuple(pl.Element(d) for d in tail),
        lambda i: (_real_tile_start(i, PEER_TB, lp, seq_real),) + (0,) * len(tail))
    return offsets, rows


def _offset_index(t):
    return (pl.program_id(0) % (TOPK_TOKENS // PEER_TB)) * PEER_TB + t


def _peer_act(off, shf, x3, gate, tab, lp, seq_real):
    n_real = shf.shape[0]
    tile = lambda: pl.BlockSpec((PEER_TB, PEER_PAIRS), lambda i: (i, 0))
    offsets, rows = _peer_specs(lp, seq_real)
    return pl.pallas_call(
        _peer_act_kernel,
        grid=(n_real // PEER_TB,),
        in_specs=offsets + [
            tile(),
            rows(ROW_TILES, LANES),
            tile(),
            pl.BlockSpec((HALF_EXPERTS * ROW_TILES, LANES), lambda i: (0, 0),
                         pipeline_mode=pl.Buffered(1)),
        ],
        out_specs=pl.BlockSpec((PEER_TB, PEER_PAIRS), lambda i: (i, 0)),
        out_shape=jax.ShapeDtypeStruct((n_real, PEER_PAIRS), F32),
        scratch_shapes=[pltpu.VMEM((2, PEER_PAIRS, LANES), U32),
                        pltpu.VMEM((PEER_TB, PEER_PAIRS), F32)],
        compiler_params=pltpu.CompilerParams(
            dimension_semantics=("arbitrary",), vmem_limit_bytes=VMEM_TABLE_LIMIT),
        name="peer_act",
    )(*off, shf, x3, gate, tab)


def _peer_out_kernel(*refs):
    off_refs = refs[:OFF_STREAMS]
    shf_ref, w_ref, h_ref, g_ref, tab_ref, o_ref, shb_ref, wb_ref, ffn_ref = refs[OFF_STREAMS:]
    n_acc = 4

    def prep(t, _):
        r0 = pl.multiple_of(t * PEER_PAIRS, PEER_PAIRS)
        shb_ref[pl.ds(r0, PEER_PAIRS), :] = _shift_rows(shf_ref[pl.ds(t, 1), :])
        wb_ref[pl.ds(r0, PEER_PAIRS), :] = _rows_to_lanes(w_ref[pl.ds(t, 1), :])
        return 0

    lax.fori_loop(0, PEER_TB, prep, 0, unroll=PREP_UNROLL)

    def token(t, _):
        j0 = pl.multiple_of(t * PEER_PAIRS, PEER_PAIRS)
        m0 = _offset_index(t)
        accs = [jnp.zeros((SUBLANES, LANES), F32) for _ in range(n_acc)]
        for jj in range(PEER_PAIRS):
            off = off_refs[jj % OFF_STREAMS][m0 + (jj // OFF_STREAMS) * TOPK_TOKENS]
            f = _expert_row(tab_ref, off, shb_ref[pl.ds(j0 + jj, 1), :])
            accs[jj % n_acc] = accs[jj % n_acc] + wb_ref[pl.ds(j0 + jj, 1), :] * f
        ffn_ref[t] = (accs[0] + accs[1]) + (accs[2] + accs[3])
        return 0

    lax.fori_loop(0, PEER_TB, token, 0)
    ffn = jnp.concatenate([ffn_ref[:, k, :] for k in range(ROW_TILES)], axis=1)
    y = h_ref[...] + ffn
    ms = jnp.mean(y * y, axis=-1, keepdims=True)
    o_ref[...] = y * lax.rsqrt(ms + EPS) * g_ref[...]


def _peer_out(off, shf, w, h2d, g, tab, lp, seq_real):
    n_real = w.shape[0]
    offsets, rows = _peer_specs(lp, seq_real)
    return pl.pallas_call(
        _peer_out_kernel,
        grid=(n_real // PEER_TB,),
        in_specs=offsets + [
            pl.BlockSpec((PEER_TB, PEER_PAIRS), lambda i: (i, 0)),
            pl.BlockSpec((PEER_TB, PEER_PAIRS), lambda i: (i, 0)),
            rows(D_MODEL),
            pl.BlockSpec((1, D_MODEL), lambda i: (0, 0)),
            pl.BlockSpec((HALF_EXPERTS * ROW_TILES, LANES), lambda i: (0, 0),
                         pipeline_mode=pl.Buffered(1)),
        ],
        out_specs=pl.BlockSpec((PEER_TB, D_MODEL), lambda i: (i, 0)),
        out_shape=jax.ShapeDtypeStruct((n_real, D_MODEL), F32),
        scratch_shapes=[pltpu.VMEM((PEER_TB * PEER_PAIRS, LANES), U32),
                        pltpu.VMEM((PEER_TB * PEER_PAIRS, LANES), F32),
                        pltpu.VMEM((PEER_TB, ROW_TILES, LANES), F32)],
        compiler_params=pltpu.CompilerParams(
            dimension_semantics=("arbitrary",), vmem_limit_bytes=VMEM_TABLE_LIMIT),
        name="peer_out",
    )(*off, shf, w, h2d, g, tab)


def kernel(x, meta_tokens, norm1_g, w_in, lambda_q1, lambda_k1, lambda_q2, lambda_k2,
           attn_subln_g, conv_w, conv_b, conv_norm_g, conv_norm_b, w_out, norm2_g,
           peer_wq, peer_subkeys, peer_u, peer_v, final_norm_g):
    b, s, _ = x.shape
    seq = N_META + s
    lp = ((seq + Q_BLOCK - 1) // Q_BLOCK) * Q_BLOCK
    t = b * lp
    assert lp % TQ == 0 and lp % CONV_ROWS == 0 and t % TM == 0
    assert s % TOPK_TOKENS == 0 and TOPK_TOKENS % PEER_TB == 0

    meta = jnp.broadcast_to(meta_tokens[None].astype(x.dtype), (b, N_META, D_MODEL))
    h = jnp.concatenate([meta, x, jnp.zeros((b, lp - seq, D_MODEL), x.dtype)], axis=1)
    h2d = h.reshape(t, D_MODEL)

    lam_init = 0.8 - 0.6 * math.exp(-0.3 * 0)
    slopes = jnp.asarray([2.0 ** (-8.0 * (i + 1) / N_HEADS) for i in range(N_HEADS)], F32)
    group = jnp.arange(CONV_WIDTH) // CONV_GROUP
    gavg = ((group[:, None] == group[None, :]).astype(F32) * (1.0 / CONV_GROUP)).astype(BF16)

    q, k, v, u = _inproj(h2d, norm1_g[0][None], w_in[0].astype(BF16))
    attn = _attention(q.reshape(b, lp, -1), k.reshape(b, lp, -1), v.reshape(b, lp, -1), slopes,
                      lambda_q1[0][None], lambda_k1[0][None], lambda_q2[0][None],
                      lambda_k2[0][None], attn_subln_g[0][None], lam_init)
    conv = _conformer_conv(u.reshape(b, lp, -1), conv_w[0], conv_b[0][None], gavg,
                           conv_norm_g[0][None], conv_norm_b[0][None])
    h1, xn2, qp = _outproj(attn.reshape(t, -1), conv.reshape(t, -1), h2d,
                           w_out[0].astype(BF16), norm2_g[0][None], peer_wq[0].astype(BF16))
    keys = peer_subkeys[0].reshape(2 * PEER_HEADS, PEER_KEYS, PEER_KEYS).astype(BF16)
    off, shf, gate = _peer_topk(qp, keys, lp, s)
    off = [off.reshape(-1)] * OFF_STREAMS
    w = _peer_act(off, shf, xn2, gate, _pack_table(peer_u[0]), lp, s)
    out = _peer_out(off, shf, w, h1, final_norm_g[None], _pack_table(peer_v[0]), lp, s)
    return out.reshape(b, s, D_MODEL)
```

```python
import functools
import math

import jax
import jax.numpy as jnp
from jax import lax
from jax.experimental import pallas as pl
from jax.experimental.pallas import tpu as pltpu

F32 = jnp.float32
BF16 = jnp.bfloat16
I32 = jnp.int32
U32 = jnp.uint32

D_MODEL = 1024
N_META = 16
Q_BLOCK = 128
ATTN_WIDTH = 512
CONV_WIDTH = 512
N_HEADS = 4
HEAD_DIM = 64
V_DIM = 128
CONV_K = 31
CONV_GROUP = 64
PEER_HEADS = 8
PEER_KEYS = 128
PEER_TOPK = 16
PEER_PAIRS = PEER_HEADS * PEER_TOPK
N_EXPERTS = PEER_KEYS * PEER_KEYS
HALF_EXPERTS = N_EXPERTS // 2
EPS = 1e-6
NEG = -1e30
LOG2E = math.log2(math.e)

LANES = 128
SUBLANES = 8
ROW_TILES = D_MODEL // LANES

TM = 512
TQ = 384
CONV_ROWS = 64
CONV_PAD = 32
TOPK_TOKENS = 128
TOPK_HEADS_PER_TRIP = 8
PEER_TB = 128
ACT_TOKENS_PER_TRIP = 4
OFF_STREAMS = 8
PREP_UNROLL = 8
VMEM_TABLE_LIMIT = 52 * 1024 * 1024


def _inproj_kernel(h_ref, g_ref, w_ref, q_ref, k_ref, v_ref, u_ref):
    x = h_ref[...]
    ms = jnp.mean(x * x, axis=-1, keepdims=True)
    xn = (x * lax.rsqrt(ms + EPS) * g_ref[...]).astype(BF16)
    proj = jnp.dot(xn, w_ref[...], preferred_element_type=F32)
    q_ref[...] = (proj[:, 0:ATTN_WIDTH] * (HEAD_DIM ** -0.5 * LOG2E)).astype(BF16)
    k_ref[...] = proj[:, ATTN_WIDTH:2 * ATTN_WIDTH].astype(BF16)
    v_ref[...] = proj[:, 2 * ATTN_WIDTH:3 * ATTN_WIDTH].astype(BF16)
    ga = proj[:, 3 * ATTN_WIDTH:3 * ATTN_WIDTH + CONV_WIDTH]
    gg = proj[:, 3 * ATTN_WIDTH + CONV_WIDTH:]
    u_ref[...] = ga * jax.nn.sigmoid(gg)


def _inproj(h2d, g, w_bf16):
    t = h2d.shape[0]
    n_cols = w_bf16.shape[1]
    return pl.pallas_call(
        _inproj_kernel,
        grid=(t // TM,),
        in_specs=[
            pl.BlockSpec((TM, D_MODEL), lambda i: (i, 0)),
            pl.BlockSpec((1, D_MODEL), lambda i: (0, 0)),
            pl.BlockSpec((D_MODEL, n_cols), lambda i: (0, 0)),
        ],
        out_specs=[
            pl.BlockSpec((TM, ATTN_WIDTH), lambda i: (i, 0)),
            pl.BlockSpec((TM, ATTN_WIDTH), lambda i: (i, 0)),
            pl.BlockSpec((TM, ATTN_WIDTH), lambda i: (i, 0)),
            pl.BlockSpec((TM, CONV_WIDTH), lambda i: (i, 0)),
        ],
        out_shape=[
            jax.ShapeDtypeStruct((t, ATTN_WIDTH), BF16),
            jax.ShapeDtypeStruct((t, ATTN_WIDTH), BF16),
            jax.ShapeDtypeStruct((t, ATTN_WIDTH), BF16),
            jax.ShapeDtypeStruct((t, CONV_WIDTH), F32),
        ],
        compiler_params=pltpu.CompilerParams(
            dimension_semantics=("arbitrary",), vmem_limit_bytes=48 * 1024 * 1024),
        name="inproj",
    )(h2d, g, w_bf16)


def _attn_kernel(slopes_ref, lq1_ref, lk1_ref, lq2_ref, lk2_ref, subg_ref,
                 q_ref, k_ref, v_ref, o_ref, *, lam_init):
    hd = pl.program_id(1)
    qi = pl.program_id(2)
    slope = slopes_ref[hd]
    lam = (jnp.exp(jnp.sum(lq1_ref[...] * lk1_ref[...], keepdims=True))
           - jnp.exp(jnp.sum(lq2_ref[...] * lk2_ref[...], keepdims=True)) + lam_init)

    q = q_ref[0]
    lane = lax.broadcasted_iota(I32, q.shape, 1)
    zero = jnp.zeros_like(q)
    qs = jnp.concatenate([jnp.where(lane < HEAD_DIM, q, zero),
                          jnp.where(lane >= HEAD_DIM, q, zero)], axis=0)

    q0 = qi * TQ
    col = lax.broadcasted_iota(I32, (1, TQ), 1)

    def step(j, carry, masked):
        m, l, acc = carry
        k0 = pl.multiple_of(j * TQ, TQ)
        kj = k_ref[0, pl.ds(k0, TQ), :]
        vj = v_ref[0, pl.ds(k0, TQ), :]
        s = lax.dot_general(qs, kj, (((1,), (1,)), ((), ())), preferred_element_type=F32)
        s = s + (slope * LOG2E) * (col + (k0 - q0)).astype(F32)
        if masked:
            row = lax.broadcasted_iota(I32, (2 * TQ, TQ), 0)
            row = jnp.where(row >= TQ, row - TQ, row)
            cc = lax.broadcasted_iota(I32, (2 * TQ, TQ), 1)
            s = jnp.where(cc <= row, s, NEG)
        m_new = jnp.maximum(m, jnp.max(s, axis=1, keepdims=True))
        alpha = jnp.exp2(m - m_new)
        p = jnp.exp2(s - m_new)
        l = alpha * l + jnp.sum(p, axis=1, keepdims=True)
        acc = alpha * acc + jnp.dot(p.astype(BF16), vj, preferred_element_type=F32)
        return m_new, l, acc

    init = (jnp.full((2 * TQ, 1), NEG, F32), jnp.zeros((2 * TQ, 1), F32),
            jnp.zeros((2 * TQ, V_DIM), F32))
    carry = lax.fori_loop(0, qi, lambda j, c: step(j, c, False), init)
    m, l, acc = step(qi, carry, True)
    o = acc / l
    a = o[:TQ] - lam * o[TQ:]
    ms = jnp.mean(a * a, axis=-1, keepdims=True)
    y = a * lax.rsqrt(ms + EPS) * subg_ref[...] * (1.0 - lam_init)
    o_ref[0] = y.astype(BF16)


def _attention(q, k, v, slopes, lq1, lk1, lq2, lk2, subg, lam_init):
    b, lp, _ = q.shape
    nq = lp // TQ
    vec = lambda n: pl.BlockSpec((1, n), lambda bi, hi, i: (0, 0))
    return pl.pallas_call(
        functools.partial(_attn_kernel, lam_init=lam_init),
        grid=(b, N_HEADS, nq),
        in_specs=[
            pl.BlockSpec(memory_space=pltpu.SMEM),
            vec(HEAD_DIM), vec(HEAD_DIM), vec(HEAD_DIM), vec(HEAD_DIM), vec(V_DIM),
            pl.BlockSpec((1, TQ, V_DIM), lambda bi, hi, i: (bi, i, hi)),
            pl.BlockSpec((1, lp, V_DIM), lambda bi, hi, i: (bi, 0, hi)),
            pl.BlockSpec((1, lp, V_DIM), lambda bi, hi, i: (bi, 0, hi)),
        ],
        out_specs=pl.BlockSpec((1, TQ, V_DIM), lambda bi, hi, i: (bi, i, hi)),
        out_shape=jax.ShapeDtypeStruct((b, lp, ATTN_WIDTH), BF16),
        compiler_params=pltpu.CompilerParams(
            dimension_semantics=("arbitrary", "arbitrary", "arbitrary"),
            vmem_limit_bytes=48 * 1024 * 1024),
        name="diff_attn",
    )(slopes, lq1, lk1, lq2, lk2, subg, q, k, v)


def _group_mean(v, gavg_bf16):
    hi = v.astype(BF16)
    lo = (v - hi.astype(F32)).astype(BF16)
    return (jnp.dot(hi, gavg_bf16, preferred_element_type=F32)
            + jnp.dot(lo, gavg_bf16, preferred_element_type=F32))


def _conv_kernel(u_ref, w_ref, b_ref, gavg_ref, g_ref, beta_ref, o_ref, upad_ref, y_ref):
    lp = u_ref.shape[1]
    upad_ref[0:CONV_PAD, :] = jnp.zeros((CONV_PAD, CONV_WIDTH), F32)
    upad_ref[CONV_PAD:, :] = u_ref[0]

    def chunk(c, _):
        base = pl.multiple_of(c * CONV_ROWS, CONV_ROWS)
        for lb in range(CONV_WIDTH // LANES):
            ls = slice(lb * LANES, (lb + 1) * LANES)
            acc = jnp.zeros((CONV_ROWS, LANES), F32) + b_ref[:, ls]
            win = upad_ref[pl.ds(base, CONV_ROWS + CONV_PAD), ls]
            shifted = [win] + [jnp.roll(win, -rho, axis=0) for rho in range(1, SUBLANES)]
            for t in range(CONV_K):
                off = CONV_PAD - (CONV_K - 1) + t
                rho = off % SUBLANES
                acc = acc + w_ref[t:t + 1, ls] * shifted[rho][off - rho:off - rho + CONV_ROWS, :]
            y_ref[:, ls] = acc
        y = y_ref[...]
        mu = _group_mean(y, gavg_ref[...])
        d = y - mu
        var = _group_mean(d * d, gavg_ref[...])
        yn = d * lax.rsqrt(var + EPS) * g_ref[...] + beta_ref[...]
        o_ref[0, pl.ds(base, CONV_ROWS), :] = (yn * jax.nn.sigmoid(yn)).astype(BF16)
        return 0

    lax.fori_loop(0, lp // CONV_ROWS, chunk, 0)


def _conformer_conv(u, conv_w, conv_b, gavg, gn_g, gn_b):
    b, lp, c = u.shape
    full = lambda shape: pl.BlockSpec(shape, lambda bi: (0,) * len(shape))
    return pl.pallas_call(
        _conv_kernel,
        grid=(b,),
        in_specs=[
            pl.BlockSpec((1, lp, c), lambda bi: (bi, 0, 0)),
            full((CONV_K, c)), full((1, c)), full((c, c)), full((1, c)), full((1, c)),
        ],
        out_specs=pl.BlockSpec((1, lp, c), lambda bi: (bi, 0, 0)),
        out_shape=jax.ShapeDtypeStruct((b, lp, c), BF16),
        scratch_shapes=[pltpu.VMEM((lp + CONV_PAD, c), F32), pltpu.VMEM((CONV_ROWS, c), F32)],
        compiler_params=pltpu.CompilerParams(
            dimension_semantics=("arbitrary",), vmem_limit_bytes=56 * 1024 * 1024),
        name="conformer_conv",
    )(u, conv_w, conv_b, gavg, gn_g, gn_b)


def _store_row_tiles(ref, val):
    for k in range(ROW_TILES):
        ref[:, k, :] = val[:, k * LANES:(k + 1) * LANES]


def _outproj_kernel(a_ref, c_ref, h_ref, wo_ref, g_ref, wq_ref, h1_ref, xn_ref, qp_ref):
    mix = (jnp.dot(a_ref[...], wo_ref[0:ATTN_WIDTH, :], preferred_element_type=F32)
           + jnp.dot(c_ref[...], wo_ref[ATTN_WIDTH:, :], preferred_element_type=F32))
    h1 = h_ref[...] + mix
    h1_ref[...] = h1
    ms = jnp.mean(h1 * h1, axis=-1, keepdims=True)
    xn = h1 * lax.rsqrt(ms + EPS) * g_ref[...]
    _store_row_tiles(xn_ref, xn)
    qp = jnp.dot(xn.astype(BF16), wq_ref[...], preferred_element_type=F32)
    for hp in range(2 * PEER_HEADS):
        qp_ref[hp] = qp[:, hp * PEER_KEYS:(hp + 1) * PEER_KEYS].astype(BF16)


def _outproj(attn2d, conv2d, h2d, wo_bf16, g2, wq_bf16):
    t = h2d.shape[0]
    nq = wq_bf16.shape[1]
    return pl.pallas_call(
        _outproj_kernel,
        grid=(t // TM,),
        in_specs=[
            pl.BlockSpec((TM, ATTN_WIDTH), lambda i: (i, 0)),
            pl.BlockSpec((TM, CONV_WIDTH), lambda i: (i, 0)),
            pl.BlockSpec((TM, D_MODEL), lambda i: (i, 0)),
            pl.BlockSpec((D_MODEL, D_MODEL), lambda i: (0, 0)),
            pl.BlockSpec((1, D_MODEL), lambda i: (0, 0)),
            pl.BlockSpec((D_MODEL, nq), lambda i: (0, 0)),
        ],
        out_specs=[
            pl.BlockSpec((TM, D_MODEL), lambda i: (i, 0)),
            pl.BlockSpec((TM, ROW_TILES, LANES), lambda i: (i, 0, 0)),
            pl.BlockSpec((2 * PEER_HEADS, TM, PEER_KEYS), lambda i: (0, i, 0)),
        ],
        out_shape=[
            jax.ShapeDtypeStruct((t, D_MODEL), F32),
            jax.ShapeDtypeStruct((t, ROW_TILES, LANES), F32),
            jax.ShapeDtypeStruct((2 * PEER_HEADS, t, PEER_KEYS), BF16),
        ],
        compiler_params=pltpu.CompilerParams(
            dimension_semantics=("arbitrary",), vmem_limit_bytes=48 * 1024 * 1024),
        name="outproj_peerq",
    )(attn2d, conv2d, h2d, wo_bf16, g2, wq_bf16)


def _merge_network(n):
    pairs = []

    def merge(lo, hi, r):
        step = r * 2
        if step < hi - lo:
            merge(lo, hi, step)
            merge(lo + r, hi, step)
            pairs.extend((i, i + r) for i in range(lo + r, hi - r, step))
        else:
            pairs.append((lo, lo + r))

    def sort(lo, hi):
        if hi - lo >= 1:
            mid = lo + (hi - lo) // 2
            sort(lo, mid)
            sort(mid + 1, hi)
            merge(lo, hi, 1)

    sort(0, n - 1)
    return tuple(pairs)


SORT16 = _merge_network(PEER_KEYS // SUBLANES)


def _sort_lists(vals, ids):
    v, d = list(vals), list(ids)
    for i, j in SORT16:
        swap = (v[j] > v[i]) | ((v[j] == v[i]) & (d[j] < d[i]))
        v[i], v[j] = jnp.where(swap, v[j], v[i]), jnp.where(swap, v[i], v[j])
        d[i], d[j] = jnp.where(swap, d[j], d[i]), jnp.where(swap, d[i], d[j])
    return v, d


def _row_writer(val_ref, idx_ref):
    def emit(r, val, idx):
        val_ref[r:r + 1, :] = val
        idx_ref[r:r + 1, :] = idx
    return emit


def _top_keys(st, emit):
    tokens = st.shape[1]
    sub = lax.broadcasted_iota(I32, (SUBLANES, tokens), 0)
    n = st.shape[0] // SUBLANES
    v, d = _sort_lists([st[k * SUBLANES:(k + 1) * SUBLANES, :] for k in range(n)],
                       [sub + k * SUBLANES for k in range(n)])
    for r in range(PEER_TOPK):
        m = jnp.max(v[0], axis=0, keepdims=True)
        imin = jnp.min(jnp.where(v[0] == m, d[0], st.shape[0]), axis=0, keepdims=True)
        emit(r, m, imin)
        sel = d[0] == imin
        for k in range(PEER_TOPK - 1 - r):
            v[k] = jnp.where(sel, v[k + 1], v[k])
            d[k] = jnp.where(sel, d[k + 1], d[k])


def _top_sums(s1, s2, emit):
    tokens = s1.shape[1]
    sub = lax.broadcasted_iota(I32, (SUBLANES, tokens), 0)
    big = PEER_TOPK * PEER_TOPK
    v = [s1[0:SUBLANES, :] + s2[b:b + 1, :] for b in range(PEER_TOPK)]
    ptr = jnp.zeros((SUBLANES, tokens), I32)
    e = s1[SUBLANES:, :] + s2[0:1, :]
    eid = (sub + SUBLANES) * PEER_TOPK
    for r in range(PEER_TOPK):
        hid = sub * PEER_TOPK + ptr
        m = jnp.max(jnp.maximum(v[0], e), axis=0, keepdims=True)
        cand = jnp.minimum(jnp.where(v[0] == m, hid, big), jnp.where(e == m, eid, big))
        imin = jnp.min(cand, axis=0, keepdims=True)
        emit(r, m, imin)
        sel = hid == imin
        e = jnp.where(eid == imin, NEG, e)
        for k in range(PEER_TOPK - 1 - r):
            v[k] = jnp.where(sel, v[k + 1], v[k])
        ptr = jnp.where(sel, ptr + 1, ptr)


def _topk_kernel(qp_ref, keys_ref, r_ref, sh_ref, gate_ref,
                 s1_ref, i1_ref, s2_ref, i2_ref, ct_ref, ci_ref, e_ref, g_ref):
    def head(h, slot):
        s1_s, i1_s, s2_s, i2_s, ct_s, ci_s = (
            ref.at[slot] for ref in (s1_ref, i1_ref, s2_ref, i2_ref, ct_ref, ci_ref))
        for p, (sv, si) in enumerate(((s1_s, i1_s), (s2_s, i2_s))):
            hp = 2 * h + p
            st = lax.dot_general(keys_ref[hp], qp_ref[hp], (((1,), (1,)), ((), ())),
                                 preferred_element_type=F32)
            _top_keys(st, _row_writer(sv, si))
        _top_sums(s1_s[...], s2_s[...], _row_writer(ct_s, ci_s))
        ct = ct_s[...]
        ci = ci_s[...]
        hi = ci >> 4
        lo = ci & (PEER_TOPK - 1)
        i1 = i1_s[...]
        i2 = i2_s[...]
        e1 = jnp.zeros_like(ci)
        e2 = jnp.zeros_like(ci)
        for a in range(PEER_TOPK):
            e1 = jnp.where(hi == a, i1[a:a + 1, :], e1)
            e2 = jnp.where(lo == a, i2[a:a + 1, :], e2)
        e = e1 * PEER_KEYS + e2
        ex = jnp.exp(ct - jnp.max(ct, axis=0, keepdims=True))
        gate = ex / jnp.sum(ex, axis=0, keepdims=True)
        row0 = pl.multiple_of(h * PEER_TOPK, PEER_TOPK)
        e_ref[pl.ds(row0, PEER_TOPK), :] = e
        g_ref[pl.ds(row0, PEER_TOPK), :] = gate

    def heads(g, _):
        for slot in range(TOPK_HEADS_PER_TRIP):
            head(g * TOPK_HEADS_PER_TRIP + slot, slot)
        return 0

    lax.fori_loop(0, PEER_HEADS // TOPK_HEADS_PER_TRIP, heads, 0)
    e = e_ref[...]
    sh_ref[...] = ((e >> 13) << 4).astype(F32).T
    gate_ref[...] = g_ref[...].T
    e_ref[...] = (e & (HALF_EXPERTS - 1)) * ROW_TILES
    for k in range(OFF_STREAMS):
        r_ref[0, k] = e_ref[pl.ds(k, PEER_PAIRS // OFF_STREAMS, stride=OFF_STREAMS), :]


def _real_tile_start(i, tile, lp, seq_real):
    per_batch = seq_real // tile
    return pl.multiple_of((i // per_batch) * lp + N_META + (i % per_batch) * tile, N_META)


def _peer_topk(qp, keys_bf16, lp, seq_real):
    n_real = (qp.shape[1] // lp) * seq_real
    tt = TOPK_TOKENS
    sc = lambda dt: pltpu.VMEM((TOPK_HEADS_PER_TRIP, PEER_TOPK, tt), dt)
    qp_block = tuple(pl.Element(d) for d in (2 * PEER_HEADS, tt, PEER_KEYS))
    return pl.pallas_call(
        _topk_kernel,
        grid=(n_real // tt,),
        in_specs=[
            pl.BlockSpec(qp_block, lambda i: (0, _real_tile_start(i, tt, lp, seq_real), 0)),
            pl.BlockSpec((2 * PEER_HEADS, PEER_KEYS, PEER_KEYS), lambda i: (0, 0, 0)),
        ],
        out_specs=[
            pl.BlockSpec((1, OFF_STREAMS, PEER_PAIRS // OFF_STREAMS, tt), lambda i: (i, 0, 0, 0)),
            pl.BlockSpec((tt, PEER_PAIRS), lambda i: (i, 0)),
            pl.BlockSpec((tt, PEER_PAIRS), lambda i: (i, 0)),
        ],
        out_shape=[
            jax.ShapeDtypeStruct((n_real // tt, OFF_STREAMS, PEER_PAIRS // OFF_STREAMS, tt), I32),
            jax.ShapeDtypeStruct((n_real, PEER_PAIRS), F32),
            jax.ShapeDtypeStruct((n_real, PEER_PAIRS), F32),
        ],
        scratch_shapes=[sc(F32), sc(I32), sc(F32), sc(I32), sc(F32), sc(I32),
                        pltpu.VMEM((PEER_PAIRS, tt), I32), pltpu.VMEM((PEER_PAIRS, tt), F32)],
        compiler_params=pltpu.CompilerParams(dimension_semantics=("arbitrary",)),
        name="peer_topk",
    )(qp, keys_bf16)


def _pack_kernel(hi_ref, lo_ref, o_ref):
    hi = pltpu.bitcast(hi_ref[...].astype(BF16).astype(F32), U32)
    lo = pltpu.bitcast(lo_ref[...].astype(BF16).astype(F32), U32)
    o_ref[...] = hi | (lo >> 16)


def _pack_table(tab):
    rows = 512
    nb = HALF_EXPERTS // rows
    packed = pl.pallas_call(
        _pack_kernel,
        grid=(nb,),
        in_specs=[pl.BlockSpec((rows, D_MODEL), lambda i: (i, 0)),
                  pl.BlockSpec((rows, D_MODEL), lambda i: (i + nb, 0))],
        out_specs=pl.BlockSpec((rows, D_MODEL), lambda i: (i, 0)),
        out_shape=jax.ShapeDtypeStruct((HALF_EXPERTS, D_MODEL), U32),
        compiler_params=pltpu.CompilerParams(dimension_semantics=("arbitrary",)),
        name="pack_table",
    )(tab, tab)
    return packed.reshape(HALF_EXPERTS * ROW_TILES, LANES)


def _expert_row(tab_ref, off, shv):
    word = tab_ref[pl.ds(pl.multiple_of(off, SUBLANES), SUBLANES), :]
    return pltpu.bitcast(jnp.left_shift(word, shv) & jnp.uint32(0xFFFF0000), F32)


def _rows_to_lanes(row):
    return jnp.transpose(jnp.broadcast_to(row, (PEER_PAIRS, LANES)))


def _shift_rows(shf_row):
    ri = lax.broadcasted_iota(I32, (PEER_PAIRS, PEER_PAIRS), 0)
    ci = lax.broadcasted_iota(I32, (PEER_PAIRS, PEER_PAIRS), 1)
    diag = jnp.where(ri == ci, shf_row, 0.0).astype(BF16)
    rep = jnp.dot(diag, jnp.ones((PEER_PAIRS, LANES), BF16), preferred_element_type=F32)
    return pltpu.bitcast(rep, U32) >> 26


def _fold8(prods):
    sub = lax.broadcasted_iota(I32, (SUBLANES, LANES), 0)
    cur = prods
    for sh in (1, 2, 4):
        keep = (sub & sh) == 0
        nxt = []
        for k in range(0, len(cur), 2):
            a = jnp.where(keep, cur[k], cur[k + 1])
            b = jnp.where(keep, cur[k + 1], cur[k])
            nxt.append(a + pltpu.roll(b, sh, axis=0))
        cur = nxt
    return cur[0]


def _peer_act_kernel(*refs):
    off_refs = refs[:OFF_STREAMS]
    shf_ref, x_ref, gate_ref, tab_ref, w_ref, shb_ref, a_ref = refs[OFF_STREAMS:]

    def prep(t, slot):
        shb_ref[slot] = _shift_rows(shf_ref[pl.ds(jnp.minimum(t, PEER_TB - 1), 1), :])

    pair_slot = (lax.broadcasted_iota(I32, (SUBLANES, LANES), 1)
                 - lax.broadcasted_iota(I32, (SUBLANES, LANES), 0))

    def pairs(t, slot):
        xt = x_ref[t]
        t0 = _offset_index(t)
        spread = jnp.zeros((SUBLANES, LANES), F32)
        for g in range(PEER_PAIRS // SUBLANES):
            prods = []
            for jj in range(g * SUBLANES, (g + 1) * SUBLANES):
                off = off_refs[jj % OFF_STREAMS][t0 + (jj // OFF_STREAMS) * TOPK_TOKENS]
                f = _expert_row(tab_ref, off, shb_ref[slot, jj:jj + 1, :])
                prods.append(f * xt)
            sums = jnp.sum(_fold8(prods), axis=1, keepdims=True)
            spread = jnp.where(pair_slot == g * SUBLANES, sums, spread)
        a_ref[pl.ds(t, 1), :] = jnp.sum(spread, axis=0, keepdims=True)

    prep(0, 0)

    def tokens(i, _):
        for k in range(ACT_TOKENS_PER_TRIP):
            t = i * ACT_TOKENS_PER_TRIP + k
            prep(t + 1, (k + 1) % 2)
            pairs(t, k % 2)
        return 0

    lax.fori_loop(0, PEER_TB // ACT_TOKENS_PER_TRIP, tokens, 0)
    a = a_ref[...]
    act = 0.5 * a * (1.0 + lax.erf(a * (2.0 ** -0.5)))
    w_ref[...] = gate_ref[...] * act


def _peer_specs(lp, seq_real):
    per_tile = TOPK_TOKENS // PEER_TB
    stream_len = (PEER_PAIRS // OFF_STREAMS) * TOPK_TOKENS
    offsets = [pl.BlockSpec((stream_len,), lambda i, k=k: ((i // per_tile) * OFF_STREAMS + k,),
                            memory_space=pltpu.SMEM) for k in range(OFF_STREAMS)]
    rows = lambda *tail: pl.BlockSpec(
        (pl.Element(PEER_TB),) + tuple(pl.Element(d) for d in tail),
        lambda i: (_real_tile_start(i, PEER_TB, lp, seq_real),) + (0,) * len(tail))
    return offsets, rows


def _offset_index(t):
    return (pl.program_id(0) % (TOPK_TOKENS // PEER_TB)) * PEER_TB + t


def _peer_act(off, shf, x3, gate, tab, lp, seq_real):
    n_real = shf.shape[0]
    tile = lambda: pl.BlockSpec((PEER_TB, PEER_PAIRS), lambda i: (i, 0))
    offsets, rows = _peer_specs(lp, seq_real)
    return pl.pallas_call(
        _peer_act_kernel,
        grid=(n_real // PEER_TB,),
        in_specs=offsets + [
            tile(),
            rows(ROW_TILES, LANES),
            tile(),
            pl.BlockSpec((HALF_EXPERTS * ROW_TILES, LANES), lambda i: (0, 0),
                         pipeline_mode=pl.Buffered(1)),
        ],
        out_specs=pl.BlockSpec((PEER_TB, PEER_PAIRS), lambda i: (i, 0)),
        out_shape=jax.ShapeDtypeStruct((n_real, PEER_PAIRS), F32),
        scratch_shapes=[pltpu.VMEM((2, PEER_PAIRS, LANES), U32),
                        pltpu.VMEM((PEER_TB, PEER_PAIRS), F32)],
        compiler_params=pltpu.CompilerParams(
            dimension_semantics=("arbitrary",), vmem_limit_bytes=VMEM_TABLE_LIMIT),
        name="peer_act",
    )(*off, shf, x3, gate, tab)


def _peer_out_kernel(*refs):
    off_refs = refs[:OFF_STREAMS]
    shf_ref, w_ref, h_ref, g_ref, tab_ref, o_ref, shb_ref, wb_ref, ffn_ref = refs[OFF_STREAMS:]
    n_acc = 4

    def prep(t, _):
        r0 = pl.multiple_of(t * PEER_PAIRS, PEER_PAIRS)
        shb_ref[pl.ds(r0, PEER_PAIRS), :] = _shift_rows(shf_ref[pl.ds(t, 1), :])
        wb_ref[pl.ds(r0, PEER_PAIRS), :] = _rows_to_lanes(w_ref[pl.ds(t, 1), :])
        return 0

    lax.fori_loop(0, PEER_TB, prep, 0, unroll=PREP_UNROLL)

    def token(t, _):
        j0 = pl.multiple_of(t * PEER_PAIRS, PEER_PAIRS)
        m0 = _offset_index(t)
        accs = [jnp.zeros((SUBLANES, LANES), F32) for _ in range(n_acc)]
        for jj in range(PEER_PAIRS):
            off = off_refs[jj % OFF_STREAMS][m0 + (jj // OFF_STREAMS) * TOPK_TOKENS]
            f = _expert_row(tab_ref, off, shb_ref[pl.ds(j0 + jj, 1), :])
            accs[jj % n_acc] = accs[jj % n_acc] + wb_ref[pl.ds(j0 + jj, 1), :] * f
        ffn_ref[t] = (accs[0] + accs[1]) + (accs[2] + accs[3])
        return 0

    lax.fori_loop(0, PEER_TB, token, 0)
    ffn = jnp.concatenate([ffn_ref[:, k, :] for k in range(ROW_TILES)], axis=1)
    y = h_ref[...] + ffn
    ms = jnp.mean(y * y, axis=-1, keepdims=True)
    o_ref[...] = y * lax.rsqrt(ms + EPS) * g_ref[...]


def _peer_out(off, shf, w, h2d, g, tab, lp, seq_real):
    n_real = w.shape[0]
    offsets, rows = _peer_specs(lp, seq_real)
    return pl.pallas_call(
        _peer_out_kernel,
        grid=(n_real // PEER_TB,),
        in_specs=offsets + [
            pl.BlockSpec((PEER_TB, PEER_PAIRS), lambda i: (i, 0)),
            pl.BlockSpec((PEER_TB, PEER_PAIRS), lambda i: (i, 0)),
            rows(D_MODEL),
            pl.BlockSpec((1, D_MODEL), lambda i: (0, 0)),
            pl.BlockSpec((HALF_EXPERTS * ROW_TILES, LANES), lambda i: (0, 0),
                         pipeline_mode=pl.Buffered(1)),
        ],
        out_specs=pl.BlockSpec((PEER_TB, D_MODEL), lambda i: (i, 0)),
        out_shape=jax.ShapeDtypeStruct((n_real, D_MODEL), F32),
        scratch_shapes=[pltpu.VMEM((PEER_TB * PEER_PAIRS, LANES), U32),
                        pltpu.VMEM((PEER_TB * PEER_PAIRS, LANES), F32),
                        pltpu.VMEM((PEER_TB, ROW_TILES, LANES), F32)],
        compiler_params=pltpu.CompilerParams(
            dimension_semantics=("arbitrary",), vmem_limit_bytes=VMEM_TABLE_LIMIT),
        name="peer_out",
    )(*off, shf, w, h2d, g, tab)


def kernel(x, meta_tokens, norm1_g, w_in, lambda_q1, lambda_k1, lambda_q2, lambda_k2,
           attn_subln_g, conv_w, conv_b, conv_norm_g, conv_norm_b, w_out, norm2_g,
           peer_wq, peer_subkeys, peer_u, peer_v, final_norm_g):
    b, s, _ = x.shape
    seq = N_META + s
    lp = ((seq + Q_BLOCK - 1) // Q_BLOCK) * Q_BLOCK
    t = b * lp
    assert lp % TQ == 0 and lp % CONV_ROWS == 0 and t % TM == 0
    assert s % TOPK_TOKENS == 0 and TOPK_TOKENS % PEER_TB == 0

    meta = jnp.broadcast_to(meta_tokens[None].astype(x.dtype), (b, N_META, D_MODEL))
    h = jnp.concatenate([meta, x, jnp.zeros((b, lp - seq, D_MODEL), x.dtype)], axis=1)
    h2d = h.reshape(t, D_MODEL)

    lam_init = 0.8 - 0.6 * math.exp(-0.3 * 0)
    slopes = jnp.asarray([2.0 ** (-8.0 * (i + 1) / N_HEADS) for i in range(N_HEADS)], F32)
    group = jnp.arange(CONV_WIDTH) // CONV_GROUP
    gavg = ((group[:, None] == group[None, :]).astype(F32) * (1.0 / CONV_GROUP)).astype(BF16)

    q, k, v, u = _inproj(h2d, norm1_g[0][None], w_in[0].astype(BF16))
    attn = _attention(q.reshape(b, lp, -1), k.reshape(b, lp, -1), v.reshape(b, lp, -1), slopes,
                      lambda_q1[0][None], lambda_k1[0][None], lambda_q2[0][None],
                      lambda_k2[0][None], attn_subln_g[0][None], lam_init)
    conv = _conformer_conv(u.reshape(b, lp, -1), conv_w[0], conv_b[0][None], gavg,
                           conv_norm_g[0][None], conv_norm_b[0][None])
    h1, xn2, qp = _outproj(attn.reshape(t, -1), conv.reshape(t, -1), h2d,
                           w_out[0].astype(BF16), norm2_g[0][None], peer_wq[0].astype(BF16))
    keys = peer_subkeys[0].reshape(2 * PEER_HEADS, PEER_KEYS, PEER_KEYS).astype(BF16)
    off, shf, gate = _peer_topk(qp, keys, lp, s)
    off = [off.reshape(-1)] * OFF_STREAMS
    w = _peer_act(off, shf, xn2, gate, _pack_table(peer_u[0]), lp, s)
    out = _peer_out(off, shf, w, h1, final_norm_g[None], _pack_table(peer_v[0]), lp, s)
    return out.reshape(b, s, D_MODEL)
```

```python
import functools
import math

import jax
import jax.numpy as jnp
from jax import lax
from jax.experimental import pallas as pl
from jax.experimental.pallas import tpu as pltpu

F32 = jnp.float32
BF16 = jnp.bfloat16
I32 = jnp.int32
U32 = jnp.uint32

D_MODEL = 1024
N_META = 16
Q_BLOCK = 128
ATTN_WIDTH = 512
CONV_WIDTH = 512
N_HEADS = 4
HEAD_DIM = 64
V_DIM = 128
CONV_K = 31
CONV_GROUP = 64
PEER_HEADS = 8
PEER_KEYS = 128
PEER_TOPK = 16
PEER_PAIRS = PEER_HEADS * PEER_TOPK
N_EXPERTS = PEER_KEYS * PEER_KEYS
HALF_EXPERTS = N_EXPERTS // 2
EPS = 1e-6
NEG = -1e30
LOG2E = math.log2(math.e)

LANES = 128
SUBLANES = 8
ROW_TILES = D_MODEL // LANES

TM = 512
TQ = 384
CONV_ROWS = 128
CONV_PAD = 32
TOPK_TOKENS = 128
TOPK_HEADS_PER_TRIP = 8
PEER_TB = 128
ACT_TOKENS_PER_TRIP = 4
OFF_STREAMS = 8
PREP_UNROLL = 8
VMEM_TABLE_LIMIT = 52 * 1024 * 1024


def _inproj_kernel(h_ref, g_ref, w_ref, q_ref, k_ref, v_ref, u_ref):
    x = h_ref[...]
    ms = jnp.mean(x * x, axis=-1, keepdims=True)
    xn = (x * lax.rsqrt(ms + EPS) * g_ref[...]).astype(BF16)
    proj = jnp.dot(xn, w_ref[...], preferred_element_type=F32)
    q_ref[...] = (proj[:, 0:ATTN_WIDTH] * (HEAD_DIM ** -0.5 * LOG2E)).astype(BF16)
    k_ref[...] = proj[:, ATTN_WIDTH:2 * ATTN_WIDTH].astype(BF16)
    v_ref[...] = proj[:, 2 * ATTN_WIDTH:3 * ATTN_WIDTH].astype(BF16)
    ga = proj[:, 3 * ATTN_WIDTH:3 * ATTN_WIDTH + CONV_WIDTH]
    gg = proj[:, 3 * ATTN_WIDTH + CONV_WIDTH:]
    u_ref[...] = ga * jax.nn.sigmoid(gg)


def _inproj(h2d, g, w_bf16):
    t = h2d.shape[0]
    n_cols = w_bf16.shape[1]
    return pl.pallas_call(
        _inproj_kernel,
        grid=(t // TM,),
        in_specs=[
            pl.BlockSpec((TM, D_MODEL), lambda i: (i, 0)),
            pl.BlockSpec((1, D_MODEL), lambda i: (0, 0)),
            pl.BlockSpec((D_MODEL, n_cols), lambda i: (0, 0)),
        ],
        out_specs=[
            pl.BlockSpec((TM, ATTN_WIDTH), lambda i: (i, 0)),
            pl.BlockSpec((TM, ATTN_WIDTH), lambda i: (i, 0)),
            pl.BlockSpec((TM, ATTN_WIDTH), lambda i: (i, 0)),
            pl.BlockSpec((TM, CONV_WIDTH), lambda i: (i, 0)),
        ],
        out_shape=[
            jax.ShapeDtypeStruct((t, ATTN_WIDTH), BF16),
            jax.ShapeDtypeStruct((t, ATTN_WIDTH), BF16),
            jax.ShapeDtypeStruct((t, ATTN_WIDTH), BF16),
            jax.ShapeDtypeStruct((t, CONV_WIDTH), F32),
        ],
        compiler_params=pltpu.CompilerParams(
            dimension_semantics=("arbitrary",), vmem_limit_bytes=48 * 1024 * 1024),
        name="inproj",
    )(h2d, g, w_bf16)


def _attn_kernel(slopes_ref, lq1_ref, lk1_ref, lq2_ref, lk2_ref, subg_ref,
                 q_ref, k_ref, v_ref, o_ref, *, lam_init):
    hd = pl.program_id(1)
    qi = pl.program_id(2)
    slope = slopes_ref[hd]
    lam = (jnp.exp(jnp.sum(lq1_ref[...] * lk1_ref[...], keepdims=True))
           - jnp.exp(jnp.sum(lq2_ref[...] * lk2_ref[...], keepdims=True)) + lam_init)

    q = q_ref[0]
    lane = lax.broadcasted_iota(I32, q.shape, 1)
    zero = jnp.zeros_like(q)
    qs = jnp.concatenate([jnp.where(lane < HEAD_DIM, q, zero),
                          jnp.where(lane >= HEAD_DIM, q, zero)], axis=0)

    q0 = qi * TQ
    col = lax.broadcasted_iota(I32, (1, TQ), 1)

    def step(j, carry, masked):
        m, l, acc = carry
        k0 = pl.multiple_of(j * TQ, TQ)
        kj = k_ref[0, pl.ds(k0, TQ), :]
        vj = v_ref[0, pl.ds(k0, TQ), :]
        s = lax.dot_general(qs, kj, (((1,), (1,)), ((), ())), preferred_element_type=F32)
        s = s + (slope * LOG2E) * (col + (k0 - q0)).astype(F32)
        if masked:
            row = lax.broadcasted_iota(I32, (2 * TQ, TQ), 0)
            row = jnp.where(row >= TQ, row - TQ, row)
            cc = lax.broadcasted_iota(I32, (2 * TQ, TQ), 1)
            s = jnp.where(cc <= row, s, NEG)
        m_new = jnp.maximum(m, jnp.max(s, axis=1, keepdims=True))
        alpha = jnp.exp2(m - m_new)
        p = jnp.exp2(s - m_new)
        l = alpha * l + jnp.sum(p, axis=1, keepdims=True)
        acc = alpha * acc + jnp.dot(p.astype(BF16), vj, preferred_element_type=F32)
        return m_new, l, acc

    init = (jnp.full((2 * TQ, 1), NEG, F32), jnp.zeros((2 * TQ, 1), F32),
            jnp.zeros((2 * TQ, V_DIM), F32))
    carry = lax.fori_loop(0, qi, lambda j, c: step(j, c, False), init)
    m, l, acc = step(qi, carry, True)
    o = acc / l
    a = o[:TQ] - lam * o[TQ:]
    ms = jnp.mean(a * a, axis=-1, keepdims=True)
    y = a * lax.rsqrt(ms + EPS) * subg_ref[...] * (1.0 - lam_init)
    o_ref[0] = y.astype(BF16)


def _attention(q, k, v, slopes, lq1, lk1, lq2, lk2, subg, lam_init):
    b, lp, _ = q.shape
    nq = lp // TQ
    vec = lambda n: pl.BlockSpec((1, n), lambda bi, hi, i: (0, 0))
    return pl.pallas_call(
        functools.partial(_attn_kernel, lam_init=lam_init),
        grid=(b, N_HEADS, nq),
        in_specs=[
            pl.BlockSpec(memory_space=pltpu.SMEM),
            vec(HEAD_DIM), vec(HEAD_DIM), vec(HEAD_DIM), vec(HEAD_DIM), vec(V_DIM),
            pl.BlockSpec((1, TQ, V_DIM), lambda bi, hi, i: (bi, i, hi)),
            pl.BlockSpec((1, lp, V_DIM), lambda bi, hi, i: (bi, 0, hi)),
            pl.BlockSpec((1, lp, V_DIM), lambda bi, hi, i: (bi, 0, hi)),
        ],
        out_specs=pl.BlockSpec((1, TQ, V_DIM), lambda bi, hi, i: (bi, i, hi)),
        out_shape=jax.ShapeDtypeStruct((b, lp, ATTN_WIDTH), BF16),
        compiler_params=pltpu.CompilerParams(
            dimension_semantics=("arbitrary", "arbitrary", "arbitrary"),
            vmem_limit_bytes=48 * 1024 * 1024),
        name="diff_attn",
    )(slopes, lq1, lk1, lq2, lk2, subg, q, k, v)


def _group_mean(v, gavg_bf16):
    hi = v.astype(BF16)
    lo = (v - hi.astype(F32)).astype(BF16)
    return (jnp.dot(hi, gavg_bf16, preferred_element_type=F32)
            + jnp.dot(lo, gavg_bf16, preferred_element_type=F32))


def _conv_kernel(u_ref, w_ref, b_ref, gavg_ref, g_ref, beta_ref, o_ref, upad_ref, y_ref):
    lp = u_ref.shape[1]
    upad_ref[0:CONV_PAD, :] = jnp.zeros((CONV_PAD, CONV_WIDTH), F32)
    upad_ref[CONV_PAD:, :] = u_ref[0]

    def chunk(c, _):
        base = pl.multiple_of(c * CONV_ROWS, CONV_ROWS)
        for lb in range(CONV_WIDTH // LANES):
            ls = slice(lb * LANES, (lb + 1) * LANES)
            acc = jnp.zeros((CONV_ROWS, LANES), F32) + b_ref[:, ls]
            win = upad_ref[pl.ds(base, CONV_ROWS + CONV_PAD), ls]
            shifted = [win] + [jnp.roll(win, -rho, axis=0) for rho in range(1, SUBLANES)]
            for t in range(CONV_K):
                off = CONV_PAD - (CONV_K - 1) + t
                rho = off % SUBLANES
                acc = acc + w_ref[t:t + 1, ls] * shifted[rho][off - rho:off - rho + CONV_ROWS, :]
            y_ref[:, ls] = acc
        y = y_ref[...]
        mu = _group_mean(y, gavg_ref[...])
        d = y - mu
        var = _group_mean(d * d, gavg_ref[...])
        yn = d * lax.rsqrt(var + EPS) * g_ref[...] + beta_ref[...]
        o_ref[0, pl.ds(base, CONV_ROWS), :] = (yn * jax.nn.sigmoid(yn)).astype(BF16)
        return 0

    lax.fori_loop(0, lp // CONV_ROWS, chunk, 0)


def _conformer_conv(u, conv_w, conv_b, gavg, gn_g, gn_b):
    b, lp, c = u.shape
    full = lambda shape: pl.BlockSpec(shape, lambda bi: (0,) * len(shape))
    return pl.pallas_call(
        _conv_kernel,
        grid=(b,),
        in_specs=[
            pl.BlockSpec((1, lp, c), lambda bi: (bi, 0, 0)),
            full((CONV_K, c)), full((1, c)), full((c, c)), full((1, c)), full((1, c)),
        ],
        out_specs=pl.BlockSpec((1, lp, c), lambda bi: (bi, 0, 0)),
        out_shape=jax.ShapeDtypeStruct((b, lp, c), BF16),
        scratch_shapes=[pltpu.VMEM((lp + CONV_PAD, c), F32), pltpu.VMEM((CONV_ROWS, c), F32)],
        compiler_params=pltpu.CompilerParams(
            dimension_semantics=("arbitrary",), vmem_limit_bytes=56 * 1024 * 1024),
        name="conformer_conv",
    )(u, conv_w, conv_b, gavg, gn_g, gn_b)


def _store_row_tiles(ref, val):
    for k in range(ROW_TILES):
        ref[:, k, :] = val[:, k * LANES:(k + 1) * LANES]


def _outproj_kernel(a_ref, c_ref, h_ref, wo_ref, g_ref, wq_ref, h1_ref, xn_ref, qp_ref):
    mix = (jnp.dot(a_ref[...], wo_ref[0:ATTN_WIDTH, :], preferred_element_type=F32)
           + jnp.dot(c_ref[...], wo_ref[ATTN_WIDTH:, :], preferred_element_type=F32))
    h1 = h_ref[...] + mix
    h1_ref[...] = h1
    ms = jnp.mean(h1 * h1, axis=-1, keepdims=True)
    xn = h1 * lax.rsqrt(ms + EPS) * g_ref[...]
    _store_row_tiles(xn_ref, xn)
    qp = jnp.dot(xn.astype(BF16), wq_ref[...], preferred_element_type=F32)
    for hp in range(2 * PEER_HEADS):
        qp_ref[hp] = qp[:, hp * PEER_KEYS:(hp + 1) * PEER_KEYS].astype(BF16)


def _outproj(attn2d, conv2d, h2d, wo_bf16, g2, wq_bf16):
    t = h2d.shape[0]
    nq = wq_bf16.shape[1]
    return pl.pallas_call(
        _outproj_kernel,
        grid=(t // TM,),
        in_specs=[
            pl.BlockSpec((TM, ATTN_WIDTH), lambda i: (i, 0)),
            pl.BlockSpec((TM, CONV_WIDTH), lambda i: (i, 0)),
            pl.BlockSpec((TM, D_MODEL), lambda i: (i, 0)),
            pl.BlockSpec((D_MODEL, D_MODEL), lambda i: (0, 0)),
            pl.BlockSpec((1, D_MODEL), lambda i: (0, 0)),
            pl.BlockSpec((D_MODEL, nq), lambda i: (0, 0)),
        ],
        out_specs=[
            pl.BlockSpec((TM, D_MODEL), lambda i: (i, 0)),
            pl.BlockSpec((TM, ROW_TILES, LANES), lambda i: (i, 0, 0)),
            pl.BlockSpec((2 * PEER_HEADS, TM, PEER_KEYS), lambda i: (0, i, 0)),
        ],
        out_shape=[
            jax.ShapeDtypeStruct((t, D_MODEL), F32),
            jax.ShapeDtypeStruct((t, ROW_TILES, LANES), F32),
            jax.ShapeDtypeStruct((2 * PEER_HEADS, t, PEER_KEYS), BF16),
        ],
        compiler_params=pltpu.CompilerParams(
            dimension_semantics=("arbitrary",), vmem_limit_bytes=48 * 1024 * 1024),
        name="outproj_peerq",
    )(attn2d, conv2d, h2d, wo_bf16, g2, wq_bf16)


def _merge_network(n):
    pairs = []

    def merge(lo, hi, r):
        step = r * 2
        if step < hi - lo:
            merge(lo, hi, step)
            merge(lo + r, hi, step)
            pairs.extend((i, i + r) for i in range(lo + r, hi - r, step))
        else:
            pairs.append((lo, lo + r))

    def sort(lo, hi):
        if hi - lo >= 1:
            mid = lo + (hi - lo) // 2
            sort(lo, mid)
            sort(mid + 1, hi)
            merge(lo, hi, 1)

    sort(0, n - 1)
    return tuple(pairs)


SORT16 = _merge_network(PEER_KEYS // SUBLANES)


def _sort_lists(vals, ids):
    v, d = list(vals), list(ids)
    for i, j in SORT16:
        swap = (v[j] > v[i]) | ((v[j] == v[i]) & (d[j] < d[i]))
        v[i], v[j] = jnp.where(swap, v[j], v[i]), jnp.where(swap, v[i], v[j])
        d[i], d[j] = jnp.where(swap, d[j], d[i]), jnp.where(swap, d[i], d[j])
    return v, d


def _row_writer(val_ref, idx_ref):
    def emit(r, val, idx):
        val_ref[r:r + 1, :] = val
        idx_ref[r:r + 1, :] = idx
    return emit


def _top_keys(st, emit):
    tokens = st.shape[1]
    sub = lax.broadcasted_iota(I32, (SUBLANES, tokens), 0)
    n = st.shape[0] // SUBLANES
    v, d = _sort_lists([st[k * SUBLANES:(k + 1) * SUBLANES, :] for k in range(n)],
                       [sub + k * SUBLANES for k in range(n)])
    for r in range(PEER_TOPK):
        m = jnp.max(v[0], axis=0, keepdims=True)
        imin = jnp.min(jnp.where(v[0] == m, d[0], st.shape[0]), axis=0, keepdims=True)
        emit(r, m, imin)
        sel = d[0] == imin
        for k in range(PEER_TOPK - 1 - r):
            v[k] = jnp.where(sel, v[k + 1], v[k])
            d[k] = jnp.where(sel, d[k + 1], d[k])


def _top_sums(s1, s2, emit):
    tokens = s1.shape[1]
    sub = lax.broadcasted_iota(I32, (SUBLANES, tokens), 0)
    big = PEER_TOPK * PEER_TOPK
    v = [s1[0:SUBLANES, :] + s2[b:b + 1, :] for b in range(PEER_TOPK)]
    ptr = jnp.zeros((SUBLANES, tokens), I32)
    e = s1[SUBLANES:, :] + s2[0:1, :]
    eid = (sub + SUBLANES) * PEER_TOPK
    for r in range(PEER_TOPK):
        hid = sub * PEER_TOPK + ptr
        m = jnp.max(jnp.maximum(v[0], e), axis=0, keepdims=True)
        cand = jnp.minimum(jnp.where(v[0] == m, hid, big), jnp.where(e == m, eid, big))
        imin = jnp.min(cand, axis=0, keepdims=True)
        emit(r, m, imin)
        sel = hid == imin
        e = jnp.where(eid == imin, NEG, e)
        for k in range(PEER_TOPK - 1 - r):
            v[k] = jnp.where(sel, v[k + 1], v[k])
        ptr = jnp.where(sel, ptr + 1, ptr)


def _topk_kernel(qp_ref, keys_ref, r_ref, sh_ref, gate_ref,
                 s1_ref, i1_ref, s2_ref, i2_ref, ct_ref, ci_ref, e_ref, g_ref):
    def head(h, slot):
        s1_s, i1_s, s2_s, i2_s, ct_s, ci_s = (
            ref.at[slot] for ref in (s1_ref, i1_ref, s2_ref, i2_ref, ct_ref, ci_ref))
        for p, (sv, si) in enumerate(((s1_s, i1_s), (s2_s, i2_s))):
            hp = 2 * h + p
            st = lax.dot_general(keys_ref[hp], qp_ref[hp], (((1,), (1,)), ((), ())),
                                 preferred_element_type=F32)
            _top_keys(st, _row_writer(sv, si))
        _top_sums(s1_s[...], s2_s[...], _row_writer(ct_s, ci_s))
        ct = ct_s[...]
        ci = ci_s[...]
        hi = ci >> 4
        lo = ci & (PEER_TOPK - 1)
        i1 = i1_s[...]
        i2 = i2_s[...]
        e1 = jnp.zeros_like(ci)
        e2 = jnp.zeros_like(ci)
        for a in range(PEER_TOPK):
            e1 = jnp.where(hi == a, i1[a:a + 1, :], e1)
            e2 = jnp.where(lo == a, i2[a:a + 1, :], e2)
        e = e1 * PEER_KEYS + e2
        ex = jnp.exp(ct - jnp.max(ct, axis=0, keepdims=True))
        gate = ex / jnp.sum(ex, axis=0, keepdims=True)
        row0 = pl.multiple_of(h * PEER_TOPK, PEER_TOPK)
        e_ref[pl.ds(row0, PEER_TOPK), :] = e
        g_ref[pl.ds(row0, PEER_TOPK), :] = gate

    def heads(g, _):
        for slot in range(TOPK_HEADS_PER_TRIP):
            head(g * TOPK_HEADS_PER_TRIP + slot, slot)
        return 0

    lax.fori_loop(0, PEER_HEADS // TOPK_HEADS_PER_TRIP, heads, 0)
    e = e_ref[...]
    sh_ref[...] = ((e >> 13) << 4).astype(F32).T
    gate_ref[...] = g_ref[...].T
    e_ref[...] = (e & (HALF_EXPERTS - 1)) * ROW_TILES
    for k in range(OFF_STREAMS):
        r_ref[0, k] = e_ref[pl.ds(k, PEER_PAIRS // OFF_STREAMS, stride=OFF_STREAMS), :]


def _real_tile_start(i, tile, lp, seq_real):
    per_batch = seq_real // tile
    return pl.multiple_of((i // per_batch) * lp + N_META + (i % per_batch) * tile, N_META)


def _peer_topk(qp, keys_bf16, lp, seq_real):
    n_real = (qp.shape[1] // lp) * seq_real
    tt = TOPK_TOKENS
    sc = lambda dt: pltpu.VMEM((TOPK_HEADS_PER_TRIP, PEER_TOPK, tt), dt)
    qp_block = tuple(pl.Element(d) for d in (2 * PEER_HEADS, tt, PEER_KEYS))
    return pl.pallas_call(
        _topk_kernel,
        grid=(n_real // tt,),
        in_specs=[
            pl.BlockSpec(qp_block, lambda i: (0, _real_tile_start(i, tt, lp, seq_real), 0)),
            pl.BlockSpec((2 * PEER_HEADS, PEER_KEYS, PEER_KEYS), lambda i: (0, 0, 0)),
        ],
        out_specs=[
            pl.BlockSpec((1, OFF_STREAMS, PEER_PAIRS // OFF_STREAMS, tt), lambda i: (i, 0, 0, 0)),
            pl.BlockSpec((tt, PEER_PAIRS), lambda i: (i, 0)),
            pl.BlockSpec((tt, PEER_PAIRS), lambda i: (i, 0)),
        ],
        out_shape=[
            jax.ShapeDtypeStruct((n_real // tt, OFF_STREAMS, PEER_PAIRS // OFF_STREAMS, tt), I32),
            jax.ShapeDtypeStruct((n_real, PEER_PAIRS), F32),
            jax.ShapeDtypeStruct((n_real, PEER_PAIRS), F32),
        ],
        scratch_shapes=[sc(F32), sc(I32), sc(F32), sc(I32), sc(F32), sc(I32),
                        pltpu.VMEM((PEER_PAIRS, tt), I32), pltpu.VMEM((PEER_PAIRS, tt), F32)],
        compiler_params=pltpu.CompilerParams(dimension_semantics=("arbitrary",)),
        name="peer_topk",
    )(qp, keys_bf16)


def _pack_kernel(hi_ref, lo_ref, o_ref):
    hi = pltpu.bitcast(hi_ref[...].astype(BF16).astype(F32), U32)
    lo = pltpu.bitcast(lo_ref[...].astype(BF16).astype(F32), U32)
    o_ref[...] = hi | (lo >> 16)


def _pack_table(tab):
    rows = 512
    nb = HALF_EXPERTS // rows
    packed = pl.pallas_call(
        _pack_kernel,
        grid=(nb,),
        in_specs=[pl.BlockSpec((rows, D_MODEL), lambda i: (i, 0)),
                  pl.BlockSpec((rows, D_MODEL), lambda i: (i + nb, 0))],
        out_specs=pl.BlockSpec((rows, D_MODEL), lambda i: (i, 0)),
        out_shape=jax.ShapeDtypeStruct((HALF_EXPERTS, D_MODEL), U32),
        compiler_params=pltpu.CompilerParams(dimension_semantics=("arbitrary",)),
        name="pack_table",
    )(tab, tab)
    return packed.reshape(HALF_EXPERTS * ROW_TILES, LANES)


def _expert_row(tab_ref, off, shv):
    word = tab_ref[pl.ds(pl.multiple_of(off, SUBLANES), SUBLANES), :]
    return pltpu.bitcast(jnp.left_shift(word, shv) & jnp.uint32(0xFFFF0000), F32)


def _rows_to_lanes(row):
    return jnp.transpose(jnp.broadcast_to(row, (PEER_PAIRS, LANES)))


def _shift_rows(shf_row):
    ri = lax.broadcasted_iota(I32, (PEER_PAIRS, PEER_PAIRS), 0)
    ci = lax.broadcasted_iota(I32, (PEER_PAIRS, PEER_PAIRS), 1)
    diag = jnp.where(ri == ci, shf_row, 0.0).astype(BF16)
    rep = jnp.dot(diag, jnp.ones((PEER_PAIRS, LANES), BF16), preferred_element_type=F32)
    return pltpu.bitcast(rep, U32) >> 26


def _fold8(prods):
    sub = lax.broadcasted_iota(I32, (SUBLANES, LANES), 0)
    cur = prods
    for sh in (1, 2, 4):
        keep = (sub & sh) == 0
        nxt = []
        for k in range(0, len(cur), 2):
            a = jnp.where(keep, cur[k], cur[k + 1])
            b = jnp.where(keep, cur[k + 1], cur[k])
            nxt.append(a + pltpu.roll(b, sh, axis=0))
        cur = nxt
    return cur[0]


def _peer_act_kernel(*refs):
    off_refs = refs[:OFF_STREAMS]
    shf_ref, x_ref, gate_ref, tab_ref, w_ref, shb_ref, a_ref = refs[OFF_STREAMS:]

    def prep(t, slot):
        shb_ref[slot] = _shift_rows(shf_ref[pl.ds(jnp.minimum(t, PEER_TB - 1), 1), :])

    pair_slot = (lax.broadcasted_iota(I32, (SUBLANES, LANES), 1)
                 - lax.broadcasted_iota(I32, (SUBLANES, LANES), 0))

    def pairs(t, slot):
        xt = x_ref[t]
        t0 = _offset_index(t)
        spread = jnp.zeros((SUBLANES, LANES), F32)
        for g in range(PEER_PAIRS // SUBLANES):
            prods = []
            for jj in range(g * SUBLANES, (g + 1) * SUBLANES):
                off = off_refs[jj % OFF_STREAMS][t0 + (jj // OFF_STREAMS) * TOPK_TOKENS]
                f = _expert_row(tab_ref, off, shb_ref[slot, jj:jj + 1, :])
                prods.append(f * xt)
            sums = jnp.sum(_fold8(prods), axis=1, keepdims=True)
            spread = jnp.where(pair_slot == g * SUBLANES, sums, spread)
        a_ref[pl.ds(t, 1), :] = jnp.sum(spread, axis=0, keepdims=True)

    prep(0, 0)

    def tokens(i, _):
        for k in range(ACT_TOKENS_PER_TRIP):
            t = i * ACT_TOKENS_PER_TRIP + k
            prep(t + 1, (k + 1) % 2)
            pairs(t, k % 2)
        return 0

    lax.fori_loop(0, PEER_TB // ACT_TOKENS_PER_TRIP, tokens, 0)
    a = a_ref[...]
    act = 0.5 * a * (1.0 + lax.erf(a * (2.0 ** -0.5)))
    w_ref[...] = gate_ref[...] * act


def _peer_specs(lp, seq_real):
    per_tile = TOPK_TOKENS // PEER_TB
    stream_len = (PEER_PAIRS // OFF_STREAMS) * TOPK_TOKENS
    offsets = [pl.BlockSpec((stream_len,), lambda i, k=k: ((i // per_tile) * OFF_STREAMS + k,),
                            memory_space=pltpu.SMEM) for k in range(OFF_STREAMS)]
    rows = lambda *tail: pl.BlockSpec(
        (pl.Element(PEER_TB),) + tuple(pl.Element(d) for d in tail),
        lambda i: (_real_tile_start(i, PEER_TB, lp, seq_real),) + (0,) * len(tail))
    return offsets, rows


def _offset_index(t):
    return (pl.program_id(0) % (TOPK_TOKENS // PEER_TB)) * PEER_TB + t


def _peer_act(off, shf, x3, gate, tab, lp, seq_real):
    n_real = shf.shape[0]
    tile = lambda: pl.BlockSpec((PEER_TB, PEER_PAIRS), lambda i: (i, 0))
    offsets, rows = _peer_specs(lp, seq_real)
    return pl.pallas_call(
        _peer_act_kernel,
        grid=(n_real // PEER_TB,),
        in_specs=offsets + [
            tile(),
            rows(ROW_TILES, LANES),
            tile(),
            pl.BlockSpec((HALF_EXPERTS * ROW_TILES, LANES), lambda i: (0, 0),
                         pipeline_mode=pl.Buffered(1)),
        ],
        out_specs=pl.BlockSpec((PEER_TB, PEER_PAIRS), lambda i: (i, 0)),
        out_shape=jax.ShapeDtypeStruct((n_real, PEER_PAIRS), F32),
        scratch_shapes=[pltpu.VMEM((2, PEER_PAIRS, LANES), U32),
                        pltpu.VMEM((PEER_TB, PEER_PAIRS), F32)],
        compiler_params=pltpu.CompilerParams(
            dimension_semantics=("arbitrary",), vmem_limit_bytes=VMEM_TABLE_LIMIT),
        name="peer_act",
    )(*off, shf, x3, gate, tab)


def _peer_out_kernel(*refs):
    off_refs = refs[:OFF_STREAMS]
    shf_ref, w_ref, h_ref, g_ref, tab_ref, o_ref, shb_ref, wb_ref, ffn_ref = refs[OFF_STREAMS:]
    n_acc = 4

    def prep(t, _):
        r0 = pl.multiple_of(t * PEER_PAIRS, PEER_PAIRS)
        shb_ref[pl.ds(r0, PEER_PAIRS), :] = _shift_rows(shf_ref[pl.ds(t, 1), :])
        wb_ref[pl.ds(r0, PEER_PAIRS), :] = _rows_to_lanes(w_ref[pl.ds(t, 1), :])
        return 0

    lax.fori_loop(0, PEER_TB, prep, 0, unroll=PREP_UNROLL)

    def token(t, _):
        j0 = pl.multiple_of(t * PEER_PAIRS, PEER_PAIRS)
        m0 = _offset_index(t)
        accs = [jnp.zeros((SUBLANES, LANES), F32) for _ in range(n_acc)]
        for jj in range(PEER_PAIRS):
            off = off_refs[jj % OFF_STREAMS][m0 + (jj // OFF_STREAMS) * TOPK_TOKENS]
            f = _expert_row(tab_ref, off, shb_ref[pl.ds(j0 + jj, 1), :])
            accs[jj % n_acc] = accs[jj % n_acc] + wb_ref[pl.ds(j0 + jj, 1), :] * f
        ffn_ref[t] = (accs[0] + accs[1]) + (accs[2] + accs[3])
        return 0

    lax.fori_loop(0, PEER_TB, token, 0)
    ffn = jnp.concatenate([ffn_ref[:, k, :] for k in range(ROW_TILES)], axis=1)
    y = h_ref[...] + ffn
    ms = jnp.mean(y * y, axis=-1, keepdims=True)
    o_ref[...] = y * lax.rsqrt(ms + EPS) * g_ref[...]


def _peer_out(off, shf, w, h2d, g, tab, lp, seq_real):
    n_real = w.shape[0]
    offsets, rows = _peer_specs(lp, seq_real)
    return pl.pallas_call(
        _peer_out_kernel,
        grid=(n_real // PEER_TB,),
        in_specs=offsets + [
            pl.BlockSpec((PEER_TB, PEER_PAIRS), lambda i: (i, 0)),
            pl.BlockSpec((PEER_TB, PEER_PAIRS), lambda i: (i, 0)),
            rows(D_MODEL),
            pl.BlockSpec((1, D_MODEL), lambda i: (0, 0)),
            pl.BlockSpec((HALF_EXPERTS * ROW_TILES, LANES), lambda i: (0, 0),
                         pipeline_mode=pl.Buffered(1)),
        ],
        out_specs=pl.BlockSpec((PEER_TB, D_MODEL), lambda i: (i, 0)),
        out_shape=jax.ShapeDtypeStruct((n_real, D_MODEL), F32),
        scratch_shapes=[pltpu.VMEM((PEER_TB * PEER_PAIRS, LANES), U32),
                        pltpu.VMEM((PEER_TB * PEER_PAIRS, LANES), F32),
                        pltpu.VMEM((PEER_TB, ROW_TILES, LANES), F32)],
        compiler_params=pltpu.CompilerParams(
            dimension_semantics=("arbitrary",), vmem_limit_bytes=VMEM_TABLE_LIMIT),
        name="peer_out",
    )(*off, shf, w, h2d, g, tab)


def kernel(x, meta_tokens, norm1_g, w_in, lambda_q1, lambda_k1, lambda_q2, lambda_k2,
           attn_subln_g, conv_w, conv_b, conv_norm_g, conv_norm_b, w_out, norm2_g,
           peer_wq, peer_subkeys, peer_u, peer_v, final_norm_g):
    b, s, _ = x.shape
    seq = N_META + s
    lp = ((seq + Q_BLOCK - 1) // Q_BLOCK) * Q_BLOCK
    t = b * lp
    assert lp % TQ == 0 and lp % CONV_ROWS == 0 and t % TM == 0
    assert s % TOPK_TOKENS == 0 and TOPK_TOKENS % PEER_TB == 0

    meta = jnp.broadcast_to(meta_tokens[None].astype(x.dtype), (b, N_META, D_MODEL))
    h = jnp.concatenate([meta, x, jnp.zeros((b, lp - seq, D_MODEL), x.dtype)], axis=1)
    h2d = h.reshape(t, D_MODEL)

    lam_init = 0.8 - 0.6 * math.exp(-0.3 * 0)
    slopes = jnp.asarray([2.0 ** (-8.0 * (i + 1) / N_HEADS) for i in range(N_HEADS)], F32)
    group = jnp.arange(CONV_WIDTH) // CONV_GROUP
    gavg = ((group[:, None] == group[None, :]).astype(F32) * (1.0 / CONV_GROUP)).astype(BF16)

    q, k, v, u = _inproj(h2d, norm1_g[0][None], w_in[0].astype(BF16))
    attn = _attention(q.reshape(b, lp, -1), k.reshape(b, lp, -1), v.reshape(b, lp, -1), slopes,
                      lambda_q1[0][None], lambda_k1[0][None], lambda_q2[0][None],
                      lambda_k2[0][None], attn_subln_g[0][None], lam_init)
    conv = _conformer_conv(u.reshape(b, lp, -1), conv_w[0], conv_b[0][None], gavg,
                           conv_norm_g[0][None], conv_norm_b[0][None])
    h1, xn2, qp = _outproj(attn.reshape(t, -1), conv.reshape(t, -1), h2d,
                           w_out[0].astype(BF16), norm2_g[0][None], peer_wq[0].astype(BF16))
    keys = peer_subkeys[0].reshape(2 * PEER_HEADS, PEER_KEYS, PEER_KEYS).astype(BF16)
    off, shf, gate = _peer_topk(qp, keys, lp, s)
    off = [off.reshape(-1)] * OFF_STREAMS
    w = _peer_act(off, shf, xn2, gate, _pack_table(peer_u[0]), lp, s)
    out = _peer_out(off, shf, w, h1, final_norm_g[None], _pack_table(peer_v[0]), lp, s)
    return out.reshape(b, s, D_MODEL)
```

```python
import functools
import math

import jax
import jax.numpy as jnp
from jax import lax
from jax.experimental import pallas as pl
from jax.experimental.pallas import tpu as pltpu

F32 = jnp.float32
BF16 = jnp.bfloat16
I32 = jnp.int32
U32 = jnp.uint32

D_MODEL = 1024
N_META = 16
Q_BLOCK = 128
ATTN_WIDTH = 512
CONV_WIDTH = 512
N_HEADS = 4
HEAD_DIM = 64
V_DIM = 128
CONV_K = 31
CONV_GROUP = 64
PEER_HEADS = 8
PEER_KEYS = 128
PEER_TOPK = 16
PEER_PAIRS = PEER_HEADS * PEER_TOPK
N_EXPERTS = PEER_KEYS * PEER_KEYS
HALF_EXPERTS = N_EXPERTS // 2
EPS = 1e-6
NEG = -1e30
LOG2E = math.log2(math.e)

LANES = 128
SUBLANES = 8
ROW_TILES = D_MODEL // LANES

TM = 512
TQ = 384
CONV_ROWS = 128
CONV_PAD = 32
TOPK_TOKENS = 128
TOPK_HEADS_PER_TRIP = 8
PEER_TB = 128
ACT_TOKENS_PER_TRIP = 4
OFF_STREAMS = 8
PREP_UNROLL = 8
VMEM_TABLE_LIMIT = 52 * 1024 * 1024


def _inproj_kernel(h_ref, g_ref, w_ref, q_ref, k_ref, v_ref, u_ref):
    x = h_ref[...]
    ms = jnp.mean(x * x, axis=-1, keepdims=True)
    xn = (x * lax.rsqrt(ms + EPS) * g_ref[...]).astype(BF16)
    proj = jnp.dot(xn, w_ref[...], preferred_element_type=F32)
    q_ref[...] = (proj[:, 0:ATTN_WIDTH] * (HEAD_DIM ** -0.5 * LOG2E)).astype(BF16)
    k_ref[...] = proj[:, ATTN_WIDTH:2 * ATTN_WIDTH].astype(BF16)
    v_ref[...] = proj[:, 2 * ATTN_WIDTH:3 * ATTN_WIDTH].astype(BF16)
    ga = proj[:, 3 * ATTN_WIDTH:3 * ATTN_WIDTH + CONV_WIDTH]
    gg = proj[:, 3 * ATTN_WIDTH + CONV_WIDTH:]
    u_ref[...] = ga * jax.nn.sigmoid(gg)


def _inproj(h2d, g, w_bf16):
    t = h2d.shape[0]
    n_cols = w_bf16.shape[1]
    return pl.pallas_call(
        _inproj_kernel,
        grid=(t // TM,),
        in_specs=[
            pl.BlockSpec((TM, D_MODEL), lambda i: (i, 0)),
            pl.BlockSpec((1, D_MODEL), lambda i: (0, 0)),
            pl.BlockSpec((D_MODEL, n_cols), lambda i: (0, 0)),
        ],
        out_specs=[
            pl.BlockSpec((TM, ATTN_WIDTH), lambda i: (i, 0)),
            pl.BlockSpec((TM, ATTN_WIDTH), lambda i: (i, 0)),
            pl.BlockSpec((TM, ATTN_WIDTH), lambda i: (i, 0)),
            pl.BlockSpec((TM, CONV_WIDTH), lambda i: (i, 0)),
        ],
        out_shape=[
            jax.ShapeDtypeStruct((t, ATTN_WIDTH), BF16),
            jax.ShapeDtypeStruct((t, ATTN_WIDTH), BF16),
            jax.ShapeDtypeStruct((t, ATTN_WIDTH), BF16),
            jax.ShapeDtypeStruct((t, CONV_WIDTH), F32),
        ],
        compiler_params=pltpu.CompilerParams(
            dimension_semantics=("arbitrary",), vmem_limit_bytes=48 * 1024 * 1024),
        name="inproj",
    )(h2d, g, w_bf16)


def _attn_kernel(slopes_ref, lq1_ref, lk1_ref, lq2_ref, lk2_ref, subg_ref,
                 q_ref, k_ref, v_ref, o_ref, *, lam_init):
    hd = pl.program_id(1)
    qi = pl.program_id(2)
    slope = slopes_ref[hd]
    lam = (jnp.exp(jnp.sum(lq1_ref[...] * lk1_ref[...], keepdims=True))
           - jnp.exp(jnp.sum(lq2_ref[...] * lk2_ref[...], keepdims=True)) + lam_init)

    q = q_ref[0]
    lane = lax.broadcasted_iota(I32, q.shape, 1)
    zero = jnp.zeros_like(q)
    qs = jnp.concatenate([jnp.where(lane < HEAD_DIM, q, zero),
                          jnp.where(lane >= HEAD_DIM, q, zero)], axis=0)

    q0 = qi * TQ
    col = lax.broadcasted_iota(I32, (1, TQ), 1)

    def step(j, carry, masked):
        m, l, acc = carry
        k0 = pl.multiple_of(j * TQ, TQ)
        kj = k_ref[0, pl.ds(k0, TQ), :]
        vj = v_ref[0, pl.ds(k0, TQ), :]
        s = lax.dot_general(qs, kj, (((1,), (1,)), ((), ())), preferred_element_type=F32)
        s = s + (slope * LOG2E) * (col + (k0 - q0)).astype(F32)
        if masked:
            row = lax.broadcasted_iota(I32, (2 * TQ, TQ), 0)
            row = jnp.where(row >= TQ, row - TQ, row)
            cc = lax.broadcasted_iota(I32, (2 * TQ, TQ), 1)
            s = jnp.where(cc <= row, s, NEG)
        m_new = jnp.maximum(m, jnp.max(s, axis=1, keepdims=True))
        alpha = jnp.exp2(m - m_new)
        p = jnp.exp2(s - m_new)
        l = alpha * l + jnp.sum(p, axis=1, keepdims=True)
        acc = alpha * acc + jnp.dot(p.astype(BF16), vj, preferred_element_type=F32)
        return m_new, l, acc

    init = (jnp.full((2 * TQ, 1), NEG, F32), jnp.zeros((2 * TQ, 1), F32),
            jnp.zeros((2 * TQ, V_DIM), F32))
    carry = lax.fori_loop(0, qi, lambda j, c: step(j, c, False), init)
    m, l, acc = step(qi, carry, True)
    o = acc / l
    a = o[:TQ] - lam * o[TQ:]
    ms = jnp.mean(a * a, axis=-1, keepdims=True)
    y = a * lax.rsqrt(ms + EPS) * subg_ref[...] * (1.0 - lam_init)
    o_ref[0] = y.astype(BF16)


def _attention(q, k, v, slopes, lq1, lk1, lq2, lk2, subg, lam_init):
    b, lp, _ = q.shape
    nq = lp // TQ
    vec = lambda n: pl.BlockSpec((1, n), lambda bi, hi, i: (0, 0))
    return pl.pallas_call(
        functools.partial(_attn_kernel, lam_init=lam_init),
        grid=(b, N_HEADS, nq),
        in_specs=[
            pl.BlockSpec(memory_space=pltpu.SMEM),
            vec(HEAD_DIM), vec(HEAD_DIM), vec(HEAD_DIM), vec(HEAD_DIM), vec(V_DIM),
            pl.BlockSpec((1, TQ, V_DIM), lambda bi, hi, i: (bi, i, hi)),
            pl.BlockSpec((1, lp, V_DIM), lambda bi, hi, i: (bi, 0, hi)),
            pl.BlockSpec((1, lp, V_DIM), lambda bi, hi, i: (bi, 0, hi)),
        ],
        out_specs=pl.BlockSpec((1, TQ, V_DIM), lambda bi, hi, i: (bi, i, hi)),
        out_shape=jax.ShapeDtypeStruct((b, lp, ATTN_WIDTH), BF16),
        compiler_params=pltpu.CompilerParams(
            dimension_semantics=("arbitrary", "arbitrary", "arbitrary"),
            vmem_limit_bytes=48 * 1024 * 1024),
        name="diff_attn",
    )(slopes, lq1, lk1, lq2, lk2, subg, q, k, v)


def _group_mean(v, gavg_bf16):
    hi = v.astype(BF16)
    lo = (v - hi.astype(F32)).astype(BF16)
    return (jnp.dot(hi, gavg_bf16, preferred_element_type=F32)
            + jnp.dot(lo, gavg_bf16, preferred_element_type=F32))


def _conv_kernel(u_ref, w_ref, b_ref, gavg_ref, g_ref, beta_ref, o_ref, upad_ref, y_ref):
    lp = u_ref.shape[1]
    upad_ref[0:CONV_PAD, :] = jnp.zeros((CONV_PAD, CONV_WIDTH), F32)
    upad_ref[CONV_PAD:, :] = u_ref[0]

    def chunk(c, _):
        base = pl.multiple_of(c * CONV_ROWS, CONV_ROWS)
        for lb in range(CONV_WIDTH // LANES):
            ls = slice(lb * LANES, (lb + 1) * LANES)
            acc = jnp.zeros((CONV_ROWS, LANES), F32) + b_ref[:, ls]
            win = upad_ref[pl.ds(base, CONV_ROWS + CONV_PAD), ls]
            shifted = [win] + [jnp.roll(win, -rho, axis=0) for rho in range(1, SUBLANES)]
            for t in range(CONV_K):
                off = CONV_PAD - (CONV_K - 1) + t
                rho = off % SUBLANES
                acc = acc + w_ref[t:t + 1, ls] * shifted[rho][off - rho:off - rho + CONV_ROWS, :]
            y_ref[:, ls] = acc
        y = y_ref[...]
        mu = _group_mean(y, gavg_ref[...])
        d = y - mu
        var = _group_mean(d * d, gavg_ref[...])
        yn = d * lax.rsqrt(var + EPS) * g_ref[...] + beta_ref[...]
        o_ref[0, pl.ds(base, CONV_ROWS), :] = (yn * jax.nn.sigmoid(yn)).astype(BF16)
        return 0

    lax.fori_loop(0, lp // CONV_ROWS, chunk, 0)


def _conformer_conv(u, conv_w, conv_b, gavg, gn_g, gn_b):
    b, lp, c = u.shape
    full = lambda shape: pl.BlockSpec(shape, lambda bi: (0,) * len(shape))
    return pl.pallas_call(
        _conv_kernel,
        grid=(b,),
        in_specs=[
            pl.BlockSpec((1, lp, c), lambda bi: (bi, 0, 0)),
            full((CONV_K, c)), full((1, c)), full((c, c)), full((1, c)), full((1, c)),
        ],
        out_specs=pl.BlockSpec((1, lp, c), lambda bi: (bi, 0, 0)),
        out_shape=jax.ShapeDtypeStruct((b, lp, c), BF16),
        scratch_shapes=[pltpu.VMEM((lp + CONV_PAD, c), F32), pltpu.VMEM((CONV_ROWS, c), F32)],
        compiler_params=pltpu.CompilerParams(
            dimension_semantics=("arbitrary",), vmem_limit_bytes=56 * 1024 * 1024),
        name="conformer_conv",
    )(u, conv_w, conv_b, gavg, gn_g, gn_b)


def _store_row_tiles(ref, val):
    for k in range(ROW_TILES):
        ref[:, k, :] = val[:, k * LANES:(k + 1) * LANES]


def _outproj_kernel(a_ref, c_ref, h_ref, wo_ref, g_ref, wq_ref, h1_ref, xn_ref, qp_ref):
    mix = (jnp.dot(a_ref[...], wo_ref[0:ATTN_WIDTH, :], preferred_element_type=F32)
           + jnp.dot(c_ref[...], wo_ref[ATTN_WIDTH:, :], preferred_element_type=F32))
    h1 = h_ref[...] + mix
    h1_ref[...] = h1
    ms = jnp.mean(h1 * h1, axis=-1, keepdims=True)
    xn = h1 * lax.rsqrt(ms + EPS) * g_ref[...]
    _store_row_tiles(xn_ref, xn)
    qp = jnp.dot(xn.astype(BF16), wq_ref[...], preferred_element_type=F32)
    for hp in range(2 * PEER_HEADS):
        qp_ref[hp] = qp[:, hp * PEER_KEYS:(hp + 1) * PEER_KEYS].astype(BF16)


def _outproj(attn2d, conv2d, h2d, wo_bf16, g2, wq_bf16):
    t = h2d.shape[0]
    nq = wq_bf16.shape[1]
    return pl.pallas_call(
        _outproj_kernel,
        grid=(t // TM,),
        in_specs=[
            pl.BlockSpec((TM, ATTN_WIDTH), lambda i: (i, 0)),
            pl.BlockSpec((TM, CONV_WIDTH), lambda i: (i, 0)),
            pl.BlockSpec((TM, D_MODEL), lambda i: (i, 0)),
            pl.BlockSpec((D_MODEL, D_MODEL), lambda i: (0, 0)),
            pl.BlockSpec((1, D_MODEL), lambda i: (0, 0)),
            pl.BlockSpec((D_MODEL, nq), lambda i: (0, 0)),
        ],
        out_specs=[
            pl.BlockSpec((TM, D_MODEL), lambda i: (i, 0)),
            pl.BlockSpec((TM, ROW_TILES, LANES), lambda i: (i, 0, 0)),
            pl.BlockSpec((2 * PEER_HEADS, TM, PEER_KEYS), lambda i: (0, i, 0)),
        ],
        out_shape=[
            jax.ShapeDtypeStruct((t, D_MODEL), F32),
            jax.ShapeDtypeStruct((t, ROW_TILES, LANES), F32),
            jax.ShapeDtypeStruct((2 * PEER_HEADS, t, PEER_KEYS), BF16),
        ],
        compiler_params=pltpu.CompilerParams(
            dimension_semantics=("arbitrary",), vmem_limit_bytes=48 * 1024 * 1024),
        name="outproj_peerq",
    )(attn2d, conv2d, h2d, wo_bf16, g2, wq_bf16)


def _merge_network(n):
    pairs = []

    def merge(lo, hi, r):
        step = r * 2
        if step < hi - lo:
            merge(lo, hi, step)
            merge(lo + r, hi, step)
            pairs.extend((i, i + r) for i in range(lo + r, hi - r, step))
        else:
            pairs.append((lo, lo + r))

    def sort(lo, hi):
        if hi - lo >= 1:
            mid = lo + (hi - lo) // 2
            sort(lo, mid)
            sort(mid + 1, hi)
            merge(lo, hi, 1)

    sort(0, n - 1)
    return tuple(pairs)


SORT16 = _merge_network(PEER_KEYS // SUBLANES)


def _sort_lists(vals, ids):
    v, d = list(vals), list(ids)
    for i, j in SORT16:
        swap = (v[j] > v[i]) | ((v[j] == v[i]) & (d[j] < d[i]))
        v[i], v[j] = jnp.where(swap, v[j], v[i]), jnp.where(swap, v[i], v[j])
        d[i], d[j] = jnp.where(swap, d[j], d[i]), jnp.where(swap, d[i], d[j])
    return v, d


def _row_writer(val_ref, idx_ref):
    def emit(r, val, idx):
        val_ref[r:r + 1, :] = val
        idx_ref[r:r + 1, :] = idx
    return emit


def _top_keys(st, emit):
    tokens = st.shape[1]
    sub = lax.broadcasted_iota(I32, (SUBLANES, tokens), 0)
    n = st.shape[0] // SUBLANES
    v, d = _sort_lists([st[k * SUBLANES:(k + 1) * SUBLANES, :] for k in range(n)],
                       [sub + k * SUBLANES for k in range(n)])
    for r in range(PEER_TOPK):
        m = jnp.max(v[0], axis=0, keepdims=True)
        imin = jnp.min(jnp.where(v[0] == m, d[0], st.shape[0]), axis=0, keepdims=True)
        emit(r, m, imin)
        sel = d[0] == imin
        for k in range(PEER_TOPK - 1 - r):
            v[k] = jnp.where(sel, v[k + 1], v[k])
            d[k] = jnp.where(sel, d[k + 1], d[k])


def _top_sums(s1, s2, emit):
    tokens = s1.shape[1]
    sub = lax.broadcasted_iota(I32, (SUBLANES, tokens), 0)
    big = PEER_TOPK * PEER_TOPK
    v = [s1[0:SUBLANES, :] + s2[b:b + 1, :] for b in range(PEER_TOPK)]
    ptr = jnp.zeros((SUBLANES, tokens), I32)
    e = s1[SUBLANES:, :] + s2[0:1, :]
    eid = (sub + SUBLANES) * PEER_TOPK
    for r in range(PEER_TOPK):
        hid = sub * PEER_TOPK + ptr
        m = jnp.max(jnp.maximum(v[0], e), axis=0, keepdims=True)
        cand = jnp.minimum(jnp.where(v[0] == m, hid, big), jnp.where(e == m, eid, big))
        imin = jnp.min(cand, axis=0, keepdims=True)
        emit(r, m, imin)
        sel = hid == imin
        e = jnp.where(eid == imin, NEG, e)
        for k in range(PEER_TOPK - 1 - r):
            v[k] = jnp.where(sel, v[k + 1], v[k])
        ptr = jnp.where(sel, ptr + 1, ptr)


def _topk_kernel(qp_ref, keys_ref, r_ref, sh_ref, gate_ref,
                 s1_ref, i1_ref, s2_ref, i2_ref, ct_ref, ci_ref, e_ref, g_ref):
    def head(h, slot):
        s1_s, i1_s, s2_s, i2_s, ct_s, ci_s = (
            ref.at[slot] for ref in (s1_ref, i1_ref, s2_ref, i2_ref, ct_ref, ci_ref))
        for p, (sv, si) in enumerate(((s1_s, i1_s), (s2_s, i2_s))):
            hp = 2 * h + p
            st = lax.dot_general(keys_ref[hp], qp_ref[hp], (((1,), (1,)), ((), ())),
                                 preferred_element_type=F32)
            _top_keys(st, _row_writer(sv, si))
        _top_sums(s1_s[...], s2_s[...], _row_writer(ct_s, ci_s))
        ct = ct_s[...]
        ci = ci_s[...]
        hi = ci >> 4
        lo = ci & (PEER_TOPK - 1)
        i1 = i1_s[...]
        i2 = i2_s[...]
        e1 = jnp.zeros_like(ci)
        e2 = jnp.zeros_like(ci)
        for a in range(PEER_TOPK):
            e1 = jnp.where(hi == a, i1[a:a + 1, :], e1)
            e2 = jnp.where(lo == a, i2[a:a + 1, :], e2)
        e = e1 * PEER_KEYS + e2
        ex = jnp.exp(ct - jnp.max(ct, axis=0, keepdims=True))
        gate = ex / jnp.sum(ex, axis=0, keepdims=True)
        row0 = pl.multiple_of(h * PEER_TOPK, PEER_TOPK)
        e_ref[pl.ds(row0, PEER_TOPK), :] = e
        g_ref[pl.ds(row0, PEER_TOPK), :] = gate

    def heads(g, _):
        for slot in range(TOPK_HEADS_PER_TRIP):
            head(g * TOPK_HEADS_PER_TRIP + slot, slot)
        return 0

    lax.fori_loop(0, PEER_HEADS // TOPK_HEADS_PER_TRIP, heads, 0)
    e = e_ref[...]
    sh_ref[...] = ((e >> 13) << 4).astype(F32).T
    gate_ref[...] = g_ref[...].T
    e_ref[...] = (e & (HALF_EXPERTS - 1)) * ROW_TILES
    for k in range(OFF_STREAMS):
        r_ref[0, k] = e_ref[pl.ds(k, PEER_PAIRS // OFF_STREAMS, stride=OFF_STREAMS), :]


def _real_tile_start(i, tile, lp, seq_real):
    per_batch = seq_real // tile
    return pl.multiple_of((i // per_batch) * lp + N_META + (i % per_batch) * tile, N_META)


def _peer_topk(qp, keys_bf16, lp, seq_real):
    n_real = (qp.shape[1] // lp) * seq_real
    tt = TOPK_TOKENS
    sc = lambda dt: pltpu.VMEM((TOPK_HEADS_PER_TRIP, PEER_TOPK, tt), dt)
    qp_block = tuple(pl.Element(d) for d in (2 * PEER_HEADS, tt, PEER_KEYS))
    return pl.pallas_call(
        _topk_kernel,
        grid=(n_real // tt,),
        in_specs=[
            pl.BlockSpec(qp_block, lambda i: (0, _real_tile_start(i, tt, lp, seq_real), 0)),
            pl.BlockSpec((2 * PEER_HEADS, PEER_KEYS, PEER_KEYS), lambda i: (0, 0, 0)),
        ],
        out_specs=[
            pl.BlockSpec((1, OFF_STREAMS, PEER_PAIRS // OFF_STREAMS, tt), lambda i: (i, 0, 0, 0)),
            pl.BlockSpec((tt, PEER_PAIRS), lambda i: (i, 0)),
            pl.BlockSpec((tt, PEER_PAIRS), lambda i: (i, 0)),
        ],
        out_shape=[
            jax.ShapeDtypeStruct((n_real // tt, OFF_STREAMS, PEER_PAIRS // OFF_STREAMS, tt), I32),
            jax.ShapeDtypeStruct((n_real, PEER_PAIRS), F32),
            jax.ShapeDtypeStruct((n_real, PEER_PAIRS), F32),
        ],
        scratch_shapes=[sc(F32), sc(I32), sc(F32), sc(I32), sc(F32), sc(I32),
                        pltpu.VMEM((PEER_PAIRS, tt), I32), pltpu.VMEM((PEER_PAIRS, tt), F32)],
        compiler_params=pltpu.CompilerParams(dimension_semantics=("arbitrary",)),
        name="peer_topk",
    )(qp, keys_bf16)


def _pack_kernel(hi_ref, lo_ref, o_ref):
    hi = pltpu.bitcast(hi_ref[...].astype(BF16).astype(F32), U32)
    lo = pltpu.bitcast(lo_ref[...].astype(BF16).astype(F32), U32)
    _store_row_tiles(o_ref, hi | (lo >> 16))


def _pack_table(tab):
    rows = 512
    nb = HALF_EXPERTS // rows
    packed = pl.pallas_call(
        _pack_kernel,
        grid=(nb,),
        in_specs=[pl.BlockSpec((rows, D_MODEL), lambda i: (i, 0)),
                  pl.BlockSpec((rows, D_MODEL), lambda i: (i + nb, 0))],
        out_specs=pl.BlockSpec((rows, ROW_TILES, LANES), lambda i: (i, 0, 0)),
        out_shape=jax.ShapeDtypeStruct((HALF_EXPERTS, ROW_TILES, LANES), U32),
        compiler_params=pltpu.CompilerParams(dimension_semantics=("arbitrary",)),
        name="pack_table",
    )(tab, tab)
    return packed.reshape(HALF_EXPERTS * ROW_TILES, LANES)


def _expert_row(tab_ref, off, shv):
    word = tab_ref[pl.ds(pl.multiple_of(off, SUBLANES), SUBLANES), :]
    return pltpu.bitcast(jnp.left_shift(word, shv) & jnp.uint32(0xFFFF0000), F32)


def _rows_to_lanes(row):
    return jnp.transpose(jnp.broadcast_to(row, (PEER_PAIRS, LANES)))


def _shift_rows(shf_row):
    ri = lax.broadcasted_iota(I32, (PEER_PAIRS, PEER_PAIRS), 0)
    ci = lax.broadcasted_iota(I32, (PEER_PAIRS, PEER_PAIRS), 1)
    diag = jnp.where(ri == ci, shf_row, 0.0).astype(BF16)
    rep = jnp.dot(diag, jnp.ones((PEER_PAIRS, LANES), BF16), preferred_element_type=F32)
    return pltpu.bitcast(rep, U32) >> 26


def _fold8(prods):
    sub = lax.broadcasted_iota(I32, (SUBLANES, LANES), 0)
    cur = prods
    for sh in (1, 2, 4):
        keep = (sub & sh) == 0
        nxt = []
        for k in range(0, len(cur), 2):
            a = jnp.where(keep, cur[k], cur[k + 1])
            b = jnp.where(keep, cur[k + 1], cur[k])
            nxt.append(a + pltpu.roll(b, sh, axis=0))
        cur = nxt
    return cur[0]


def _peer_act_kernel(*refs):
    off_refs = refs[:OFF_STREAMS]
    shf_ref, x_ref, gate_ref, tab_ref, w_ref, shb_ref, a_ref = refs[OFF_STREAMS:]

    def prep(t, slot):
        shb_ref[slot] = _shift_rows(shf_ref[pl.ds(jnp.minimum(t, PEER_TB - 1), 1), :])

    pair_slot = (lax.broadcasted_iota(I32, (SUBLANES, LANES), 1)
                 - lax.broadcasted_iota(I32, (SUBLANES, LANES), 0))

    def pairs(t, slot):
        xt = x_ref[t]
        t0 = _offset_index(t)
        spread = jnp.zeros((SUBLANES, LANES), F32)
        for g in range(PEER_PAIRS // SUBLANES):
            prods = []
            for jj in range(g * SUBLANES, (g + 1) * SUBLANES):
                off = off_refs[jj % OFF_STREAMS][t0 + (jj // OFF_STREAMS) * TOPK_TOKENS]
                f = _expert_row(tab_ref, off, shb_ref[slot, jj:jj + 1, :])
                prods.append(f * xt)
            sums = jnp.sum(_fold8(prods), axis=1, keepdims=True)
            spread = jnp.where(pair_slot == g * SUBLANES, sums, spread)
        a_ref[pl.ds(t, 1), :] = jnp.sum(spread, axis=0, keepdims=True)

    prep(0, 0)

    def tokens(i, _):
        for k in range(ACT_TOKENS_PER_TRIP):
            t = i * ACT_TOKENS_PER_TRIP + k
            prep(t + 1, (k + 1) % 2)
            pairs(t, k % 2)
        return 0

    lax.fori_loop(0, PEER_TB // ACT_TOKENS_PER_TRIP, tokens, 0)
    a = a_ref[...]
    act = 0.5 * a * (1.0 + lax.erf(a * (2.0 ** -0.5)))
    w_ref[...] = gate_ref[...] * act


def _peer_specs(lp, seq_real):
    per_tile = TOPK_TOKENS // PEER_TB
    stream_len = (PEER_PAIRS // OFF_STREAMS) * TOPK_TOKENS
    offsets = [pl.BlockSpec((stream_len,), lambda i, k=k: ((i // per_tile) * OFF_STREAMS + k,),
                            memory_space=pltpu.SMEM) for k in range(OFF_STREAMS)]
    rows = lambda *tail: pl.BlockSpec(
        (pl.Element(PEER_TB),) + tuple(pl.Element(d) for d in tail),
        lambda i: (_real_tile_start(i, PEER_TB, lp, seq_real),) + (0,) * len(tail))
    return offsets, rows


def _offset_index(t):
    return (pl.program_id(0) % (TOPK_TOKENS // PEER_TB)) * PEER_TB + t


def _peer_act(off, shf, x3, gate, tab, lp, seq_real):
    n_real = shf.shape[0]
    tile = lambda: pl.BlockSpec((PEER_TB, PEER_PAIRS), lambda i: (i, 0))
    offsets, rows = _peer_specs(lp, seq_real)
    return pl.pallas_call(
        _peer_act_kernel,
        grid=(n_real // PEER_TB,),
        in_specs=offsets + [
            tile(),
            rows(ROW_TILES, LANES),
            tile(),
            pl.BlockSpec((HALF_EXPERTS * ROW_TILES, LANES), lambda i: (0, 0),
                         pipeline_mode=pl.Buffered(1)),
        ],
        out_specs=pl.BlockSpec((PEER_TB, PEER_PAIRS), lambda i: (i, 0)),
        out_shape=jax.ShapeDtypeStruct((n_real, PEER_PAIRS), F32),
        scratch_shapes=[pltpu.VMEM((2, PEER_PAIRS, LANES), U32),
                        pltpu.VMEM((PEER_TB, PEER_PAIRS), F32)],
        compiler_params=pltpu.CompilerParams(
            dimension_semantics=("arbitrary",), vmem_limit_bytes=VMEM_TABLE_LIMIT),
        name="peer_act",
    )(*off, shf, x3, gate, tab)


def _peer_out_kernel(*refs):
    off_refs = refs[:OFF_STREAMS]
    shf_ref, w_ref, h_ref, g_ref, tab_ref, o_ref, shb_ref, wb_ref, ffn_ref = refs[OFF_STREAMS:]
    n_acc = 4

    def prep(t, _):
        r0 = pl.multiple_of(t * PEER_PAIRS, PEER_PAIRS)
        shb_ref[pl.ds(r0, PEER_PAIRS), :] = _shift_rows(shf_ref[pl.ds(t, 1), :])
        wb_ref[pl.ds(r0, PEER_PAIRS), :] = _rows_to_lanes(w_ref[pl.ds(t, 1), :])
        return 0

    lax.fori_loop(0, PEER_TB, prep, 0, unroll=PREP_UNROLL)

    def token(t, _):
        j0 = pl.multiple_of(t * PEER_PAIRS, PEER_PAIRS)
        m0 = _offset_index(t)
        accs = [jnp.zeros((SUBLANES, LANES), F32) for _ in range(n_acc)]
        for jj in range(PEER_PAIRS):
            off = off_refs[jj % OFF_STREAMS][m0 + (jj // OFF_STREAMS) * TOPK_TOKENS]
            f = _expert_row(tab_ref, off, shb_ref[pl.ds(j0 + jj, 1), :])
            accs[jj % n_acc] = accs[jj % n_acc] + wb_ref[pl.ds(j0 + jj, 1), :] * f
        ffn_ref[t] = (accs[0] + accs[1]) + (accs[2] + accs[3])
        return 0

    lax.fori_loop(0, PEER_TB, token, 0, unroll=4)
    ffn = jnp.concatenate([ffn_ref[:, k, :] for k in range(ROW_TILES)], axis=1)
    y = h_ref[...] + ffn
    ms = jnp.mean(y * y, axis=-1, keepdims=True)
    o_ref[...] = y * lax.rsqrt(ms + EPS) * g_ref[...]


def _peer_out(off, shf, w, h2d, g, tab, lp, seq_real):
    n_real = w.shape[0]
    offsets, rows = _peer_specs(lp, seq_real)
    return pl.pallas_call(
        _peer_out_kernel,
        grid=(n_real // PEER_TB,),
        in_specs=offsets + [
            pl.BlockSpec((PEER_TB, PEER_PAIRS), lambda i: (i, 0)),
            pl.BlockSpec((PEER_TB, PEER_PAIRS), lambda i: (i, 0)),
            rows(D_MODEL),
            pl.BlockSpec((1, D_MODEL), lambda i: (0, 0)),
            pl.BlockSpec((HALF_EXPERTS * ROW_TILES, LANES), lambda i: (0, 0),
                         pipeline_mode=pl.Buffered(1)),
        ],
        out_specs=pl.BlockSpec((PEER_TB, D_MODEL), lambda i: (i, 0)),
        out_shape=jax.ShapeDtypeStruct((n_real, D_MODEL), F32),
        scratch_shapes=[pltpu.VMEM((PEER_TB * PEER_PAIRS, LANES), U32),
                        pltpu.VMEM((PEER_TB * PEER_PAIRS, LANES), F32),
                        pltpu.VMEM((PEER_TB, ROW_TILES, LANES), F32)],
        compiler_params=pltpu.CompilerParams(
            dimension_semantics=("arbitrary",), vmem_limit_bytes=VMEM_TABLE_LIMIT),
        name="peer_out",
    )(*off, shf, w, h2d, g, tab)


def kernel(x, meta_tokens, norm1_g, w_in, lambda_q1, lambda_k1, lambda_q2, lambda_k2,
           attn_subln_g, conv_w, conv_b, conv_norm_g, conv_norm_b, w_out, norm2_g,
           peer_wq, peer_subkeys, peer_u, peer_v, final_norm_g):
    b, s, _ = x.shape
    seq = N_META + s
    lp = ((seq + Q_BLOCK - 1) // Q_BLOCK) * Q_BLOCK
    t = b * lp
    assert lp % TQ == 0 and lp % CONV_ROWS == 0 and t % TM == 0
    assert s % TOPK_TOKENS == 0 and TOPK_TOKENS % PEER_TB == 0

    meta = jnp.broadcast_to(meta_tokens[None].astype(x.dtype), (b, N_META, D_MODEL))
    h = jnp.concatenate([meta, x, jnp.zeros((b, lp - seq, D_MODEL), x.dtype)], axis=1)
    h2d = h.reshape(t, D_MODEL)

    lam_init = 0.8 - 0.6 * math.exp(-0.3 * 0)
    slopes = jnp.asarray([2.0 ** (-8.0 * (i + 1) / N_HEADS) for i in range(N_HEADS)], F32)
    group = jnp.arange(CONV_WIDTH) // CONV_GROUP
    gavg = ((group[:, None] == group[None, :]).astype(F32) * (1.0 / CONV_GROUP)).astype(BF16)

    q, k, v, u = _inproj(h2d, norm1_g[0][None], w_in[0].astype(BF16))
    attn = _attention(q.reshape(b, lp, -1), k.reshape(b, lp, -1), v.reshape(b, lp, -1), slopes,
                      lambda_q1[0][None], lambda_k1[0][None], lambda_q2[0][None],
                      lambda_k2[0][None], attn_subln_g[0][None], lam_init)
    conv = _conformer_conv(u.reshape(b, lp, -1), conv_w[0], conv_b[0][None], gavg,
                           conv_norm_g[0][None], conv_norm_b[0][None])
    h1, xn2, qp = _outproj(attn.reshape(t, -1), conv.reshape(t, -1), h2d,
                           w_out[0].astype(BF16), norm2_g[0][None], peer_wq[0].astype(BF16))
    keys = peer_subkeys[0].reshape(2 * PEER_HEADS, PEER_KEYS, PEER_KEYS).astype(BF16)
    off, shf, gate = _peer_topk(qp, keys, lp, s)
    off = [off.reshape(-1)] * OFF_STREAMS
    w = _peer_act(off, shf, xn2, gate, _pack_table(peer_u[0]), lp, s)
    out = _peer_out(off, shf, w, h1, final_norm_g[None], _pack_table(peer_v[0]), lp, s)
    return out.reshape(b, s, D_MODEL)
```

```python
import functools
import math

import jax
import jax.numpy as jnp
from jax import lax
from jax.experimental import pallas as pl
from jax.experimental.pallas import tpu as pltpu

F32 = jnp.float32
BF16 = jnp.bfloat16
I32 = jnp.int32
U32 = jnp.uint32

D_MODEL = 1024
N_META = 16
Q_BLOCK = 128
ATTN_WIDTH = 512
CONV_WIDTH = 512
N_HEADS = 4
HEAD_DIM = 64
V_DIM = 128
CONV_K = 31
CONV_GROUP = 64
PEER_HEADS = 8
PEER_KEYS = 128
PEER_TOPK = 16
PEER_PAIRS = PEER_HEADS * PEER_TOPK
N_EXPERTS = PEER_KEYS * PEER_KEYS
HALF_EXPERTS = N_EXPERTS // 2
EPS = 1e-6
NEG = -1e30
LOG2E = math.log2(math.e)

LANES = 128
SUBLANES = 8
ROW_TILES = D_MODEL // LANES

TM = 512
TQ = 384
CONV_ROWS = 128
CONV_PAD = 32
TOPK_TOKENS = 128
TOPK_HEADS_PER_TRIP = 8
PEER_TB = 128
ACT_TOKENS_PER_TRIP = 8
OFF_STREAMS = 8
PREP_UNROLL = 8
VMEM_TABLE_LIMIT = 52 * 1024 * 1024


def _inproj_kernel(h_ref, g_ref, w_ref, q_ref, k_ref, v_ref, u_ref):
    x = h_ref[...]
    ms = jnp.mean(x * x, axis=-1, keepdims=True)
    xn = (x * lax.rsqrt(ms + EPS) * g_ref[...]).astype(BF16)
    proj = jnp.dot(xn, w_ref[...], preferred_element_type=F32)
    q_ref[...] = (proj[:, 0:ATTN_WIDTH] * (HEAD_DIM ** -0.5 * LOG2E)).astype(BF16)
    k_ref[...] = proj[:, ATTN_WIDTH:2 * ATTN_WIDTH].astype(BF16)
    v_ref[...] = proj[:, 2 * ATTN_WIDTH:3 * ATTN_WIDTH].astype(BF16)
    ga = proj[:, 3 * ATTN_WIDTH:3 * ATTN_WIDTH + CONV_WIDTH]
    gg = proj[:, 3 * ATTN_WIDTH + CONV_WIDTH:]
    u_ref[...] = ga * jax.nn.sigmoid(gg)


def _inproj(h2d, g, w_bf16):
    t = h2d.shape[0]
    n_cols = w_bf16.shape[1]
    return pl.pallas_call(
        _inproj_kernel,
        grid=(t // TM,),
        in_specs=[
            pl.BlockSpec((TM, D_MODEL), lambda i: (i, 0)),
            pl.BlockSpec((1, D_MODEL), lambda i: (0, 0)),
            pl.BlockSpec((D_MODEL, n_cols), lambda i: (0, 0)),
        ],
        out_specs=[
            pl.BlockSpec((TM, ATTN_WIDTH), lambda i: (i, 0)),
            pl.BlockSpec((TM, ATTN_WIDTH), lambda i: (i, 0)),
            pl.BlockSpec((TM, ATTN_WIDTH), lambda i: (i, 0)),
            pl.BlockSpec((TM, CONV_WIDTH), lambda i: (i, 0)),
        ],
        out_shape=[
            jax.ShapeDtypeStruct((t, ATTN_WIDTH), BF16),
            jax.ShapeDtypeStruct((t, ATTN_WIDTH), BF16),
            jax.ShapeDtypeStruct((t, ATTN_WIDTH), BF16),
            jax.ShapeDtypeStruct((t, CONV_WIDTH), F32),
        ],
        compiler_params=pltpu.CompilerParams(
            dimension_semantics=("arbitrary",), vmem_limit_bytes=48 * 1024 * 1024),
        name="inproj",
    )(h2d, g, w_bf16)


def _attn_kernel(slopes_ref, lq1_ref, lk1_ref, lq2_ref, lk2_ref, subg_ref,
                 q_ref, k_ref, v_ref, o_ref, *, lam_init):
    hd = pl.program_id(1)
    qi = pl.program_id(2)
    slope = slopes_ref[hd]
    lam = (jnp.exp(jnp.sum(lq1_ref[...] * lk1_ref[...], keepdims=True))
           - jnp.exp(jnp.sum(lq2_ref[...] * lk2_ref[...], keepdims=True)) + lam_init)

    q = q_ref[0]
    lane = lax.broadcasted_iota(I32, q.shape, 1)
    zero = jnp.zeros_like(q)
    qs = jnp.concatenate([jnp.where(lane < HEAD_DIM, q, zero),
                          jnp.where(lane >= HEAD_DIM, q, zero)], axis=0)

    q0 = qi * TQ
    col = lax.broadcasted_iota(I32, (1, TQ), 1)

    def step(j, carry, masked):
        m, l, acc = carry
        k0 = pl.multiple_of(j * TQ, TQ)
        kj = k_ref[0, pl.ds(k0, TQ), :]
        vj = v_ref[0, pl.ds(k0, TQ), :]
        s = lax.dot_general(qs, kj, (((1,), (1,)), ((), ())), preferred_element_type=F32)
        s = s + (slope * LOG2E) * (col + (k0 - q0)).astype(F32)
        if masked:
            row = lax.broadcasted_iota(I32, (2 * TQ, TQ), 0)
            row = jnp.where(row >= TQ, row - TQ, row)
            cc = lax.broadcasted_iota(I32, (2 * TQ, TQ), 1)
            s = jnp.where(cc <= row, s, NEG)
        m_new = jnp.maximum(m, jnp.max(s, axis=1, keepdims=True))
        alpha = jnp.exp2(m - m_new)
        p = jnp.exp2(s - m_new)
        l = alpha * l + jnp.sum(p, axis=1, keepdims=True)
        acc = alpha * acc + jnp.dot(p.astype(BF16), vj, preferred_element_type=F32)
        return m_new, l, acc

    init = (jnp.full((2 * TQ, 1), NEG, F32), jnp.zeros((2 * TQ, 1), F32),
            jnp.zeros((2 * TQ, V_DIM), F32))
    carry = lax.fori_loop(0, qi, lambda j, c: step(j, c, False), init)
    m, l, acc = step(qi, carry, True)
    o = acc / l
    a = o[:TQ] - lam * o[TQ:]
    ms = jnp.mean(a * a, axis=-1, keepdims=True)
    y = a * lax.rsqrt(ms + EPS) * subg_ref[...] * (1.0 - lam_init)
    o_ref[0] = y.astype(BF16)


def _attention(q, k, v, slopes, lq1, lk1, lq2, lk2, subg, lam_init):
    b, lp, _ = q.shape
    nq = lp // TQ
    vec = lambda n: pl.BlockSpec((1, n), lambda bi, hi, i: (0, 0))
    return pl.pallas_call(
        functools.partial(_attn_kernel, lam_init=lam_init),
        grid=(b, N_HEADS, nq),
        in_specs=[
            pl.BlockSpec(memory_space=pltpu.SMEM),
            vec(HEAD_DIM), vec(HEAD_DIM), vec(HEAD_DIM), vec(HEAD_DIM), vec(V_DIM),
            pl.BlockSpec((1, TQ, V_DIM), lambda bi, hi, i: (bi, i, hi)),
            pl.BlockSpec((1, lp, V_DIM), lambda bi, hi, i: (bi, 0, hi)),
            pl.BlockSpec((1, lp, V_DIM), lambda bi, hi, i: (bi, 0, hi)),
        ],
        out_specs=pl.BlockSpec((1, TQ, V_DIM), lambda bi, hi, i: (bi, i, hi)),
        out_shape=jax.ShapeDtypeStruct((b, lp, ATTN_WIDTH), BF16),
        compiler_params=pltpu.CompilerParams(
            dimension_semantics=("arbitrary", "arbitrary", "arbitrary"),
            vmem_limit_bytes=48 * 1024 * 1024),
        name="diff_attn",
    )(slopes, lq1, lk1, lq2, lk2, subg, q, k, v)


def _group_mean(v, gavg_bf16):
    hi = v.astype(BF16)
    lo = (v - hi.astype(F32)).astype(BF16)
    return (jnp.dot(hi, gavg_bf16, preferred_element_type=F32)
            + jnp.dot(lo, gavg_bf16, preferred_element_type=F32))


def _conv_kernel(u_ref, w_ref, b_ref, gavg_ref, g_ref, beta_ref, o_ref, upad_ref, y_ref):
    lp = u_ref.shape[1]
    upad_ref[0:CONV_PAD, :] = jnp.zeros((CONV_PAD, CONV_WIDTH), F32)
    upad_ref[CONV_PAD:, :] = u_ref[0]

    def chunk(c, _):
        base = pl.multiple_of(c * CONV_ROWS, CONV_ROWS)
        for lb in range(CONV_WIDTH // LANES):
            ls = slice(lb * LANES, (lb + 1) * LANES)
            acc = jnp.zeros((CONV_ROWS, LANES), F32) + b_ref[:, ls]
            win = upad_ref[pl.ds(base, CONV_ROWS + CONV_PAD), ls]
            shifted = [win] + [jnp.roll(win, -rho, axis=0) for rho in range(1, SUBLANES)]
            for t in range(CONV_K):
                off = CONV_PAD - (CONV_K - 1) + t
                rho = off % SUBLANES
                acc = acc + w_ref[t:t + 1, ls] * shifted[rho][off - rho:off - rho + CONV_ROWS, :]
            y_ref[:, ls] = acc
        y = y_ref[...]
        mu = _group_mean(y, gavg_ref[...])
        d = y - mu
        var = _group_mean(d * d, gavg_ref[...])
        yn = d * lax.rsqrt(var + EPS) * g_ref[...] + beta_ref[...]
        o_ref[0, pl.ds(base, CONV_ROWS), :] = (yn * jax.nn.sigmoid(yn)).astype(BF16)
        return 0

    lax.fori_loop(0, lp // CONV_ROWS, chunk, 0)


def _conformer_conv(u, conv_w, conv_b, gavg, gn_g, gn_b):
    b, lp, c = u.shape
    full = lambda shape: pl.BlockSpec(shape, lambda bi: (0,) * len(shape))
    return pl.pallas_call(
        _conv_kernel,
        grid=(b,),
        in_specs=[
            pl.BlockSpec((1, lp, c), lambda bi: (bi, 0, 0)),
            full((CONV_K, c)), full((1, c)), full((c, c)), full((1, c)), full((1, c)),
        ],
        out_specs=pl.BlockSpec((1, lp, c), lambda bi: (bi, 0, 0)),
        out_shape=jax.ShapeDtypeStruct((b, lp, c), BF16),
        scratch_shapes=[pltpu.VMEM((lp + CONV_PAD, c), F32), pltpu.VMEM((CONV_ROWS, c), F32)],
        compiler_params=pltpu.CompilerParams(
            dimension_semantics=("arbitrary",), vmem_limit_bytes=56 * 1024 * 1024),
        name="conformer_conv",
    )(u, conv_w, conv_b, gavg, gn_g, gn_b)


def _store_row_tiles(ref, val):
    for k in range(ROW_TILES):
        ref[:, k, :] = val[:, k * LANES:(k + 1) * LANES]


def _outproj_kernel(a_ref, c_ref, h_ref, wo_ref, g_ref, wq_ref, h1_ref, xn_ref, qp_ref):
    mix = (jnp.dot(a_ref[...], wo_ref[0:ATTN_WIDTH, :], preferred_element_type=F32)
           + jnp.dot(c_ref[...], wo_ref[ATTN_WIDTH:, :], preferred_element_type=F32))
    h1 = h_ref[...] + mix
    h1_ref[...] = h1
    ms = jnp.mean(h1 * h1, axis=-1, keepdims=True)
    xn = h1 * lax.rsqrt(ms + EPS) * g_ref[...]
    _store_row_tiles(xn_ref, xn)
    qp = jnp.dot(xn.astype(BF16), wq_ref[...], preferred_element_type=F32)
    for hp in range(2 * PEER_HEADS):
        qp_ref[hp] = qp[:, hp * PEER_KEYS:(hp + 1) * PEER_KEYS].astype(BF16)


def _outproj(attn2d, conv2d, h2d, wo_bf16, g2, wq_bf16):
    t = h2d.shape[0]
    nq = wq_bf16.shape[1]
    return pl.pallas_call(
        _outproj_kernel,
        grid=(t // TM,),
        in_specs=[
            pl.BlockSpec((TM, ATTN_WIDTH), lambda i: (i, 0)),
            pl.BlockSpec((TM, CONV_WIDTH), lambda i: (i, 0)),
            pl.BlockSpec((TM, D_MODEL), lambda i: (i, 0)),
            pl.BlockSpec((D_MODEL, D_MODEL), lambda i: (0, 0)),
            pl.BlockSpec((1, D_MODEL), lambda i: (0, 0)),
            pl.BlockSpec((D_MODEL, nq), lambda i: (0, 0)),
        ],
        out_specs=[
            pl.BlockSpec((TM, D_MODEL), lambda i: (i, 0)),
            pl.BlockSpec((TM, ROW_TILES, LANES), lambda i: (i, 0, 0)),
            pl.BlockSpec((2 * PEER_HEADS, TM, PEER_KEYS), lambda i: (0, i, 0)),
        ],
        out_shape=[
            jax.ShapeDtypeStruct((t, D_MODEL), F32),
            jax.ShapeDtypeStruct((t, ROW_TILES, LANES), F32),
            jax.ShapeDtypeStruct((2 * PEER_HEADS, t, PEER_KEYS), BF16),
        ],
        compiler_params=pltpu.CompilerParams(
            dimension_semantics=("arbitrary",), vmem_limit_bytes=48 * 1024 * 1024),
        name="outproj_peerq",
    )(attn2d, conv2d, h2d, wo_bf16, g2, wq_bf16)


def _merge_network(n):
    pairs = []

    def merge(lo, hi, r):
        step = r * 2
        if step < hi - lo:
            merge(lo, hi, step)
            merge(lo + r, hi, step)
            pairs.extend((i, i + r) for i in range(lo + r, hi - r, step))
        else:
            pairs.append((lo, lo + r))

    def sort(lo, hi):
        if hi - lo >= 1:
            mid = lo + (hi - lo) // 2
            sort(lo, mid)
            sort(mid + 1, hi)
            merge(lo, hi, 1)

    sort(0, n - 1)
    return tuple(pairs)


SORT16 = _merge_network(PEER_KEYS // SUBLANES)


def _sort_lists(vals, ids):
    v, d = list(vals), list(ids)
    for i, j in SORT16:
        swap = (v[j] > v[i]) | ((v[j] == v[i]) & (d[j] < d[i]))
        v[i], v[j] = jnp.where(swap, v[j], v[i]), jnp.where(swap, v[i], v[j])
        d[i], d[j] = jnp.where(swap, d[j], d[i]), jnp.where(swap, d[i], d[j])
    return v, d


def _row_writer(val_ref, idx_ref):
    def emit(r, val, idx):
        val_ref[r:r + 1, :] = val
        idx_ref[r:r + 1, :] = idx
    return emit


def _top_keys(st, emit):
    tokens = st.shape[1]
    sub = lax.broadcasted_iota(I32, (SUBLANES, tokens), 0)
    n = st.shape[0] // SUBLANES
    v, d = _sort_lists([st[k * SUBLANES:(k + 1) * SUBLANES, :] for k in range(n)],
                       [sub + k * SUBLANES for k in range(n)])
    for r in range(PEER_TOPK):
        m = jnp.max(v[0], axis=0, keepdims=True)
        imin = jnp.min(jnp.where(v[0] == m, d[0], st.shape[0]), axis=0, keepdims=True)
        emit(r, m, imin)
        sel = d[0] == imin
        for k in range(PEER_TOPK - 1 - r):
            v[k] = jnp.where(sel, v[k + 1], v[k])
            d[k] = jnp.where(sel, d[k + 1], d[k])


def _top_sums(s1, s2, emit):
    tokens = s1.shape[1]
    sub = lax.broadcasted_iota(I32, (SUBLANES, tokens), 0)
    big = PEER_TOPK * PEER_TOPK
    v = [s1[0:SUBLANES, :] + s2[b:b + 1, :] for b in range(PEER_TOPK)]
    ptr = jnp.zeros((SUBLANES, tokens), I32)
    e = s1[SUBLANES:, :] + s2[0:1, :]
    eid = (sub + SUBLANES) * PEER_TOPK
    for r in range(PEER_TOPK):
        hid = sub * PEER_TOPK + ptr
        m = jnp.max(jnp.maximum(v[0], e), axis=0, keepdims=True)
        cand = jnp.minimum(jnp.where(v[0] == m, hid, big), jnp.where(e == m, eid, big))
        imin = jnp.min(cand, axis=0, keepdims=True)
        emit(r, m, imin)
        sel = hid == imin
        e = jnp.where(eid == imin, NEG, e)
        for k in range(PEER_TOPK - 1 - r):
            v[k] = jnp.where(sel, v[k + 1], v[k])
        ptr = jnp.where(sel, ptr + 1, ptr)


def _topk_kernel(qp_ref, keys_ref, r_ref, sh_ref, gate_ref,
                 s1_ref, i1_ref, s2_ref, i2_ref, ct_ref, ci_ref, e_ref, g_ref):
    def head(h, slot):
        s1_s, i1_s, s2_s, i2_s, ct_s, ci_s = (
            ref.at[slot] for ref in (s1_ref, i1_ref, s2_ref, i2_ref, ct_ref, ci_ref))
        for p, (sv, si) in enumerate(((s1_s, i1_s), (s2_s, i2_s))):
            hp = 2 * h + p
            st = lax.dot_general(keys_ref[hp], qp_ref[hp], (((1,), (1,)), ((), ())),
                                 preferred_element_type=F32)
            _top_keys(st, _row_writer(sv, si))
        _top_sums(s1_s[...], s2_s[...], _row_writer(ct_s, ci_s))
        ct = ct_s[...]
        ci = ci_s[...]
        hi = ci >> 4
        lo = ci & (PEER_TOPK - 1)
        i1 = i1_s[...]
        i2 = i2_s[...]
        e1 = jnp.zeros_like(ci)
        e2 = jnp.zeros_like(ci)
        for a in range(PEER_TOPK):
            e1 = jnp.where(hi == a, i1[a:a + 1, :], e1)
            e2 = jnp.where(lo == a, i2[a:a + 1, :], e2)
        e = e1 * PEER_KEYS + e2
        ex = jnp.exp(ct - jnp.max(ct, axis=0, keepdims=True))
        gate = ex / jnp.sum(ex, axis=0, keepdims=True)
        row0 = pl.multiple_of(h * PEER_TOPK, PEER_TOPK)
        e_ref[pl.ds(row0, PEER_TOPK), :] = e
        g_ref[pl.ds(row0, PEER_TOPK), :] = gate

    def heads(g, _):
        for slot in range(TOPK_HEADS_PER_TRIP):
            head(g * TOPK_HEADS_PER_TRIP + slot, slot)
        return 0

    lax.fori_loop(0, PEER_HEADS // TOPK_HEADS_PER_TRIP, heads, 0)
    e = e_ref[...]
    sh_ref[...] = ((e >> 13) << 4).astype(F32).T
    gate_ref[...] = g_ref[...].T
    e_ref[...] = (e & (HALF_EXPERTS - 1)) * ROW_TILES
    for k in range(OFF_STREAMS):
        r_ref[0, k] = e_ref[pl.ds(k, PEER_PAIRS // OFF_STREAMS, stride=OFF_STREAMS), :]


def _real_tile_start(i, tile, lp, seq_real):
    per_batch = seq_real // tile
    return pl.multiple_of((i // per_batch) * lp + N_META + (i % per_batch) * tile, N_META)


def _peer_topk(qp, keys_bf16, lp, seq_real):
    n_real = (qp.shape[1] // lp) * seq_real
    tt = TOPK_TOKENS
    sc = lambda dt: pltpu.VMEM((TOPK_HEADS_PER_TRIP, PEER_TOPK, tt), dt)
    qp_block = tuple(pl.Element(d) for d in (2 * PEER_HEADS, tt, PEER_KEYS))
    return pl.pallas_call(
        _topk_kernel,
        grid=(n_real // tt,),
        in_specs=[
            pl.BlockSpec(qp_block, lambda i: (0, _real_tile_start(i, tt, lp, seq_real), 0)),
            pl.BlockSpec((2 * PEER_HEADS, PEER_KEYS, PEER_KEYS), lambda i: (0, 0, 0)),
        ],
        out_specs=[
            pl.BlockSpec((1, OFF_STREAMS, PEER_PAIRS // OFF_STREAMS, tt), lambda i: (i, 0, 0, 0)),
            pl.BlockSpec((tt, PEER_PAIRS), lambda i: (i, 0)),
            pl.BlockSpec((tt, PEER_PAIRS), lambda i: (i, 0)),
        ],
        out_shape=[
            jax.ShapeDtypeStruct((n_real // tt, OFF_STREAMS, PEER_PAIRS // OFF_STREAMS, tt), I32),
            jax.ShapeDtypeStruct((n_real, PEER_PAIRS), F32),
            jax.ShapeDtypeStruct((n_real, PEER_PAIRS), F32),
        ],
        scratch_shapes=[sc(F32), sc(I32), sc(F32), sc(I32), sc(F32), sc(I32),
                        pltpu.VMEM((PEER_PAIRS, tt), I32), pltpu.VMEM((PEER_PAIRS, tt), F32)],
        compiler_params=pltpu.CompilerParams(dimension_semantics=("arbitrary",)),
        name="peer_topk",
    )(qp, keys_bf16)


def _pack_kernel(hi_ref, lo_ref, o_ref):
    hi = pltpu.bitcast(hi_ref[...].astype(BF16).astype(F32), U32)
    lo = pltpu.bitcast(lo_ref[...].astype(BF16).astype(F32), U32)
    _store_row_tiles(o_ref, hi | (lo >> 16))


def _pack_table(tab):
    rows = 512
    nb = HALF_EXPERTS // rows
    packed = pl.pallas_call(
        _pack_kernel,
        grid=(nb,),
        in_specs=[pl.BlockSpec((rows, D_MODEL), lambda i: (i, 0)),
                  pl.BlockSpec((rows, D_MODEL), lambda i: (i + nb, 0))],
        out_specs=pl.BlockSpec((rows, ROW_TILES, LANES), lambda i: (i, 0, 0)),
        out_shape=jax.ShapeDtypeStruct((HALF_EXPERTS, ROW_TILES, LANES), U32),
        compiler_params=pltpu.CompilerParams(dimension_semantics=("arbitrary",)),
        name="pack_table",
    )(tab, tab)
    return packed.reshape(HALF_EXPERTS * ROW_TILES, LANES)


def _expert_row(tab_ref, off, shv):
    word = tab_ref[pl.ds(pl.multiple_of(off, SUBLANES), SUBLANES), :]
    return pltpu.bitcast(jnp.left_shift(word, shv) & jnp.uint32(0xFFFF0000), F32)


def _rows_to_lanes(row):
    return jnp.transpose(jnp.broadcast_to(row, (PEER_PAIRS, LANES)))


def _shift_rows(shf_row):
    ri = lax.broadcasted_iota(I32, (PEER_PAIRS, PEER_PAIRS), 0)
    ci = lax.broadcasted_iota(I32, (PEER_PAIRS, PEER_PAIRS), 1)
    diag = jnp.where(ri == ci, shf_row, 0.0).astype(BF16)
    rep = jnp.dot(diag, jnp.ones((PEER_PAIRS, LANES), BF16), preferred_element_type=F32)
    return pltpu.bitcast(rep, U32) >> 26


def _fold8(prods):
    sub = lax.broadcasted_iota(I32, (SUBLANES, LANES), 0)
    cur = prods
    for sh in (1, 2, 4):
        keep = (sub & sh) == 0
        nxt = []
        for k in range(0, len(cur), 2):
            a = jnp.where(keep, cur[k], cur[k + 1])
            b = jnp.where(keep, cur[k + 1], cur[k])
            nxt.append(a + pltpu.roll(b, sh, axis=0))
        cur = nxt
    return cur[0]


def _peer_act_kernel(*refs):
    off_refs = refs[:OFF_STREAMS]
    shf_ref, x_ref, gate_ref, tab_ref, w_ref, shb_ref, a_ref = refs[OFF_STREAMS:]

    def prep(t, slot):
        shb_ref[slot] = _shift_rows(shf_ref[pl.ds(jnp.minimum(t, PEER_TB - 1), 1), :])

    pair_slot = (lax.broadcasted_iota(I32, (SUBLANES, LANES), 1)
                 - lax.broadcasted_iota(I32, (SUBLANES, LANES), 0))

    def pairs(t, slot):
        xt = x_ref[t]
        t0 = _offset_index(t)
        spread = jnp.zeros((SUBLANES, LANES), F32)
        for g in range(PEER_PAIRS // SUBLANES):
            prods = []
            for jj in range(g * SUBLANES, (g + 1) * SUBLANES):
                off = off_refs[jj % OFF_STREAMS][t0 + (jj // OFF_STREAMS) * TOPK_TOKENS]
                f = _expert_row(tab_ref, off, shb_ref[slot, jj:jj + 1, :])
                prods.append(f * xt)
            sums = jnp.sum(_fold8(prods), axis=1, keepdims=True)
            spread = jnp.where(pair_slot == g * SUBLANES, sums, spread)
        a_ref[pl.ds(t, 1), :] = jnp.sum(spread, axis=0, keepdims=True)

    prep(0, 0)

    def tokens(i, _):
        for k in range(ACT_TOKENS_PER_TRIP):
            t = i * ACT_TOKENS_PER_TRIP + k
            prep(t + 1, (k + 1) % 2)
            pairs(t, k % 2)
        return 0

    lax.fori_loop(0, PEER_TB // ACT_TOKENS_PER_TRIP, tokens, 0)
    a = a_ref[...]
    act = 0.5 * a * (1.0 + lax.erf(a * (2.0 ** -0.5)))
    w_ref[...] = gate_ref[...] * act


def _peer_specs(lp, seq_real):
    per_tile = TOPK_TOKENS // PEER_TB
    stream_len = (PEER_PAIRS // OFF_STREAMS) * TOPK_TOKENS
    offsets = [pl.BlockSpec((stream_len,), lambda i, k=k: ((i // per_tile) * OFF_STREAMS + k,),
                            memory_space=pltpu.SMEM) for k in range(OFF_STREAMS)]
    rows = lambda *tail: pl.BlockSpec(
        (pl.Element(PEER_TB),) + tuple(pl.Element(d) for d in tail),
        lambda i: (_real_tile_start(i, PEER_TB, lp, seq_real),) + (0,) * len(tail))
    return offsets, rows


def _offset_index(t):
    return (pl.program_id(0) % (TOPK_TOKENS // PEER_TB)) * PEER_TB + t


def _peer_act(off, shf, x3, gate, tab, lp, seq_real):
    n_real = shf.shape[0]
    tile = lambda: pl.BlockSpec((PEER_TB, PEER_PAIRS), lambda i: (i, 0))
    offsets, rows = _peer_specs(lp, seq_real)
    return pl.pallas_call(
        _peer_act_kernel,
        grid=(n_real // PEER_TB,),
        in_specs=offsets + [
            tile(),
            rows(ROW_TILES, LANES),
            tile(),
            pl.BlockSpec((HALF_EXPERTS * ROW_TILES, LANES), lambda i: (0, 0),
                         pipeline_mode=pl.Buffered(1)),
        ],
        out_specs=pl.BlockSpec((PEER_TB, PEER_PAIRS), lambda i: (i, 0)),
        out_shape=jax.ShapeDtypeStruct((n_real, PEER_PAIRS), F32),
        scratch_shapes=[pltpu.VMEM((2, PEER_PAIRS, LANES), U32),
                        pltpu.VMEM((PEER_TB, PEER_PAIRS), F32)],
        compiler_params=pltpu.CompilerParams(
            dimension_semantics=("arbitrary",), vmem_limit_bytes=VMEM_TABLE_LIMIT),
        name="peer_act",
    )(*off, shf, x3, gate, tab)


def _peer_out_kernel(*refs):
    off_refs = refs[:OFF_STREAMS]
    shf_ref, w_ref, h_ref, g_ref, tab_ref, o_ref, shb_ref, wb_ref, ffn_ref = refs[OFF_STREAMS:]
    n_acc = 4

    def prep(t, _):
        r0 = pl.multiple_of(t * PEER_PAIRS, PEER_PAIRS)
        shb_ref[pl.ds(r0, PEER_PAIRS), :] = _shift_rows(shf_ref[pl.ds(t, 1), :])
        wb_ref[pl.ds(r0, PEER_PAIRS), :] = _rows_to_lanes(w_ref[pl.ds(t, 1), :])
        return 0

    lax.fori_loop(0, PEER_TB, prep, 0, unroll=PREP_UNROLL)

    def token(t, _):
        j0 = pl.multiple_of(t * PEER_PAIRS, PEER_PAIRS)
        m0 = _offset_index(t)
        accs = [jnp.zeros((SUBLANES, LANES), F32) for _ in range(n_acc)]
        for jj in range(PEER_PAIRS):
            off = off_refs[jj % OFF_STREAMS][m0 + (jj // OFF_STREAMS) * TOPK_TOKENS]
            f = _expert_row(tab_ref, off, shb_ref[pl.ds(j0 + jj, 1), :])
            accs[jj % n_acc] = accs[jj % n_acc] + wb_ref[pl.ds(j0 + jj, 1), :] * f
        ffn_ref[t] = (accs[0] + accs[1]) + (accs[2] + accs[3])
        return 0

    lax.fori_loop(0, PEER_TB, token, 0, unroll=4)
    ffn = jnp.concatenate([ffn_ref[:, k, :] for k in range(ROW_TILES)], axis=1)
    y = h_ref[...] + ffn
    ms = jnp.mean(y * y, axis=-1, keepdims=True)
    o_ref[...] = y * lax.rsqrt(ms + EPS) * g_ref[...]


def _peer_out(off, shf, w, h2d, g, tab, lp, seq_real):
    n_real = w.shape[0]
    offsets, rows = _peer_specs(lp, seq_real)
    return pl.pallas_call(
        _peer_out_kernel,
        grid=(n_real // PEER_TB,),
        in_specs=offsets + [
            pl.BlockSpec((PEER_TB, PEER_PAIRS), lambda i: (i, 0)),
            pl.BlockSpec((PEER_TB, PEER_PAIRS), lambda i: (i, 0)),
            rows(D_MODEL),
            pl.BlockSpec((1, D_MODEL), lambda i: (0, 0)),
            pl.BlockSpec((HALF_EXPERTS * ROW_TILES, LANES), lambda i: (0, 0),
                         pipeline_mode=pl.Buffered(1)),
        ],
        out_specs=pl.BlockSpec((PEER_TB, D_MODEL), lambda i: (i, 0)),
        out_shape=jax.ShapeDtypeStruct((n_real, D_MODEL), F32),
        scratch_shapes=[pltpu.VMEM((PEER_TB * PEER_PAIRS, LANES), U32),
                        pltpu.VMEM((PEER_TB * PEER_PAIRS, LANES), F32),
                        pltpu.VMEM((PEER_TB, ROW_TILES, LANES), F32)],
        compiler_params=pltpu.CompilerParams(
            dimension_semantics=("arbitrary",), vmem_limit_bytes=VMEM_TABLE_LIMIT),
        name="peer_out",
    )(*off, shf, w, h2d, g, tab)


def kernel(x, meta_tokens, norm1_g, w_in, lambda_q1, lambda_k1, lambda_q2, lambda_k2,
           attn_subln_g, conv_w, conv_b, conv_norm_g, conv_norm_b, w_out, norm2_g,
           peer_wq, peer_subkeys, peer_u, peer_v, final_norm_g):
    b, s, _ = x.shape
    seq = N_META + s
    lp = ((seq + Q_BLOCK - 1) // Q_BLOCK) * Q_BLOCK
    t = b * lp
    assert lp % TQ == 0 and lp % CONV_ROWS == 0 and t % TM == 0
    assert s % TOPK_TOKENS == 0 and TOPK_TOKENS % PEER_TB == 0

    meta = jnp.broadcast_to(meta_tokens[None].astype(x.dtype), (b, N_META, D_MODEL))
    h = jnp.concatenate([meta, x, jnp.zeros((b, lp - seq, D_MODEL), x.dtype)], axis=1)
    h2d = h.reshape(t, D_MODEL)

    lam_init = 0.8 - 0.6 * math.exp(-0.3 * 0)
    slopes = jnp.asarray([2.0 ** (-8.0 * (i + 1) / N_HEADS) for i in range(N_HEADS)], F32)
    group = jnp.arange(CONV_WIDTH) // CONV_GROUP
    gavg = ((group[:, None] == group[None, :]).astype(F32) * (1.0 / CONV_GROUP)).astype(BF16)

    q, k, v, u = _inproj(h2d, norm1_g[0][None], w_in[0].astype(BF16))
    attn = _attention(q.reshape(b, lp, -1), k.reshape(b, lp, -1), v.reshape(b, lp, -1), slopes,
                      lambda_q1[0][None], lambda_k1[0][None], lambda_q2[0][None],
                      lambda_k2[0][None], attn_subln_g[0][None], lam_init)
    conv = _conformer_conv(u.reshape(b, lp, -1), conv_w[0], conv_b[0][None], gavg,
                           conv_norm_g[0][None], conv_norm_b[0][None])
    h1, xn2, qp = _outproj(attn.reshape(t, -1), conv.reshape(t, -1), h2d,
                           w_out[0].astype(BF16), norm2_g[0][None], peer_wq[0].astype(BF16))
    keys = peer_subkeys[0].reshape(2 * PEER_HEADS, PEER_KEYS, PEER_KEYS).astype(BF16)
    off, shf, gate = _peer_topk(qp, keys, lp, s)
    off = [off.reshape(-1)] * OFF_STREAMS
    w = _peer_act(off, shf, xn2, gate, _pack_table(peer_u[0]), lp, s)
    out = _peer_out(off, shf, w, h1, final_norm_g[None], _pack_table(peer_v[0]), lp, s)
    return out.reshape(b, s, D_MODEL)
```

```python
import functools
import math

import jax
import jax.numpy as jnp
from jax import lax
from jax.experimental import pallas as pl
from jax.experimental.pallas import tpu as pltpu

F32 = jnp.float32
BF16 = jnp.bfloat16
I32 = jnp.int32
U32 = jnp.uint32

D_MODEL = 1024
N_META = 16
Q_BLOCK = 128
ATTN_WIDTH = 512
CONV_WIDTH = 512
N_HEADS = 4
HEAD_DIM = 64
V_DIM = 128
CONV_K = 31
CONV_GROUP = 64
PEER_HEADS = 8
PEER_KEYS = 128
PEER_TOPK = 16
PEER_PAIRS = PEER_HEADS * PEER_TOPK
N_EXPERTS = PEER_KEYS * PEER_KEYS
HALF_EXPERTS = N_EXPERTS // 2
EPS = 1e-6
NEG = -1e30
LOG2E = math.log2(math.e)

LANES = 128
SUBLANES = 8
ROW_TILES = D_MODEL // LANES

TM = 512
TQ = 384
CONV_ROWS = 128
CONV_PAD = 32
TOPK_TOKENS = 128
TOPK_HEADS_PER_TRIP = 8
PEER_TB = 128
ACT_TOKENS_PER_TRIP = 16
OFF_STREAMS = 8
PREP_UNROLL = 8
VMEM_TABLE_LIMIT = 52 * 1024 * 1024


def _inproj_kernel(h_ref, g_ref, w_ref, q_ref, k_ref, v_ref, u_ref):
    x = h_ref[...]
    ms = jnp.mean(x * x, axis=-1, keepdims=True)
    xn = (x * lax.rsqrt(ms + EPS) * g_ref[...]).astype(BF16)
    proj = jnp.dot(xn, w_ref[...], preferred_element_type=F32)
    q_ref[...] = (proj[:, 0:ATTN_WIDTH] * (HEAD_DIM ** -0.5 * LOG2E)).astype(BF16)
    k_ref[...] = proj[:, ATTN_WIDTH:2 * ATTN_WIDTH].astype(BF16)
    v_ref[...] = proj[:, 2 * ATTN_WIDTH:3 * ATTN_WIDTH].astype(BF16)
    ga = proj[:, 3 * ATTN_WIDTH:3 * ATTN_WIDTH + CONV_WIDTH]
    gg = proj[:, 3 * ATTN_WIDTH + CONV_WIDTH:]
    u_ref[...] = ga * jax.nn.sigmoid(gg)


def _inproj(h2d, g, w_bf16):
    t = h2d.shape[0]
    n_cols = w_bf16.shape[1]
    return pl.pallas_call(
        _inproj_kernel,
        grid=(t // TM,),
        in_specs=[
            pl.BlockSpec((TM, D_MODEL), lambda i: (i, 0)),
            pl.BlockSpec((1, D_MODEL), lambda i: (0, 0)),
            pl.BlockSpec((D_MODEL, n_cols), lambda i: (0, 0)),
        ],
        out_specs=[
            pl.BlockSpec((TM, ATTN_WIDTH), lambda i: (i, 0)),
            pl.BlockSpec((TM, ATTN_WIDTH), lambda i: (i, 0)),
            pl.BlockSpec((TM, ATTN_WIDTH), lambda i: (i, 0)),
            pl.BlockSpec((TM, CONV_WIDTH), lambda i: (i, 0)),
        ],
        out_shape=[
            jax.ShapeDtypeStruct((t, ATTN_WIDTH), BF16),
            jax.ShapeDtypeStruct((t, ATTN_WIDTH), BF16),
            jax.ShapeDtypeStruct((t, ATTN_WIDTH), BF16),
            jax.ShapeDtypeStruct((t, CONV_WIDTH), F32),
        ],
        compiler_params=pltpu.CompilerParams(
            dimension_semantics=("arbitrary",), vmem_limit_bytes=48 * 1024 * 1024),
        name="inproj",
    )(h2d, g, w_bf16)


def _attn_kernel(slopes_ref, lq1_ref, lk1_ref, lq2_ref, lk2_ref, subg_ref,
                 q_ref, k_ref, v_ref, o_ref, *, lam_init):
    hd = pl.program_id(1)
    qi = pl.program_id(2)
    slope = slopes_ref[hd]
    lam = (jnp.exp(jnp.sum(lq1_ref[...] * lk1_ref[...], keepdims=True))
           - jnp.exp(jnp.sum(lq2_ref[...] * lk2_ref[...], keepdims=True)) + lam_init)

    q = q_ref[0]
    lane = lax.broadcasted_iota(I32, q.shape, 1)
    zero = jnp.zeros_like(q)
    qs = jnp.concatenate([jnp.where(lane < HEAD_DIM, q, zero),
                          jnp.where(lane >= HEAD_DIM, q, zero)], axis=0)

    q0 = qi * TQ
    col = lax.broadcasted_iota(I32, (1, TQ), 1)

    def step(j, carry, masked):
        m, l, acc = carry
        k0 = pl.multiple_of(j * TQ, TQ)
        kj = k_ref[0, pl.ds(k0, TQ), :]
        vj = v_ref[0, pl.ds(k0, TQ), :]
        s = lax.dot_general(qs, kj, (((1,), (1,)), ((), ())), preferred_element_type=F32)
        s = s + (slope * LOG2E) * (col + (k0 - q0)).astype(F32)
        if masked:
            row = lax.broadcasted_iota(I32, (2 * TQ, TQ), 0)
            row = jnp.where(row >= TQ, row - TQ, row)
            cc = lax.broadcasted_iota(I32, (2 * TQ, TQ), 1)
            s = jnp.where(cc <= row, s, NEG)
        m_new = jnp.maximum(m, jnp.max(s, axis=1, keepdims=True))
        alpha = jnp.exp2(m - m_new)
        p = jnp.exp2(s - m_new)
        l = alpha * l + jnp.sum(p, axis=1, keepdims=True)
        acc = alpha * acc + jnp.dot(p.astype(BF16), vj, preferred_element_type=F32)
        return m_new, l, acc

    init = (jnp.full((2 * TQ, 1), NEG, F32), jnp.zeros((2 * TQ, 1), F32),
            jnp.zeros((2 * TQ, V_DIM), F32))
    carry = lax.fori_loop(0, qi, lambda j, c: step(j, c, False), init)
    m, l, acc = step(qi, carry, True)
    o = acc / l
    a = o[:TQ] - lam * o[TQ:]
    ms = jnp.mean(a * a, axis=-1, keepdims=True)
    y = a * lax.rsqrt(ms + EPS) * subg_ref[...] * (1.0 - lam_init)
    o_ref[0] = y.astype(BF16)


def _attention(q, k, v, slopes, lq1, lk1, lq2, lk2, subg, lam_init):
    b, lp, _ = q.shape
    nq = lp // TQ
    vec = lambda n: pl.BlockSpec((1, n), lambda bi, hi, i: (0, 0))
    return pl.pallas_call(
        functools.partial(_attn_kernel, lam_init=lam_init),
        grid=(b, N_HEADS, nq),
        in_specs=[
            pl.BlockSpec(memory_space=pltpu.SMEM),
            vec(HEAD_DIM), vec(HEAD_DIM), vec(HEAD_DIM), vec(HEAD_DIM), vec(V_DIM),
            pl.BlockSpec((1, TQ, V_DIM), lambda bi, hi, i: (bi, i, hi)),
            pl.BlockSpec((1, lp, V_DIM), lambda bi, hi, i: (bi, 0, hi)),
            pl.BlockSpec((1, lp, V_DIM), lambda bi, hi, i: (bi, 0, hi)),
        ],
        out_specs=pl.BlockSpec((1, TQ, V_DIM), lambda bi, hi, i: (bi, i, hi)),
        out_shape=jax.ShapeDtypeStruct((b, lp, ATTN_WIDTH), BF16),
        compiler_params=pltpu.CompilerParams(
            dimension_semantics=("arbitrary", "arbitrary", "arbitrary"),
            vmem_limit_bytes=48 * 1024 * 1024),
        name="diff_attn",
    )(slopes, lq1, lk1, lq2, lk2, subg, q, k, v)


def _group_mean(v, gavg_bf16):
    hi = v.astype(BF16)
    lo = (v - hi.astype(F32)).astype(BF16)
    return (jnp.dot(hi, gavg_bf16, preferred_element_type=F32)
            + jnp.dot(lo, gavg_bf16, preferred_element_type=F32))


def _conv_kernel(u_ref, w_ref, b_ref, gavg_ref, g_ref, beta_ref, o_ref, upad_ref, y_ref):
    lp = u_ref.shape[1]
    upad_ref[0:CONV_PAD, :] = jnp.zeros((CONV_PAD, CONV_WIDTH), F32)
    upad_ref[CONV_PAD:, :] = u_ref[0]

    def chunk(c, _):
        base = pl.multiple_of(c * CONV_ROWS, CONV_ROWS)
        for lb in range(CONV_WIDTH // LANES):
            ls = slice(lb * LANES, (lb + 1) * LANES)
            acc = jnp.zeros((CONV_ROWS, LANES), F32) + b_ref[:, ls]
            win = upad_ref[pl.ds(base, CONV_ROWS + CONV_PAD), ls]
            shifted = [win] + [jnp.roll(win, -rho, axis=0) for rho in range(1, SUBLANES)]
            for t in range(CONV_K):
                off = CONV_PAD - (CONV_K - 1) + t
                rho = off % SUBLANES
                acc = acc + w_ref[t:t + 1, ls] * shifted[rho][off - rho:off - rho + CONV_ROWS, :]
            y_ref[:, ls] = acc
        y = y_ref[...]
        mu = _group_mean(y, gavg_ref[...])
        d = y - mu
        var = _group_mean(d * d, gavg_ref[...])
        yn = d * lax.rsqrt(var + EPS) * g_ref[...] + beta_ref[...]
        o_ref[0, pl.ds(base, CONV_ROWS), :] = (yn * jax.nn.sigmoid(yn)).astype(BF16)
        return 0

    lax.fori_loop(0, lp // CONV_ROWS, chunk, 0)


def _conformer_conv(u, conv_w, conv_b, gavg, gn_g, gn_b):
    b, lp, c = u.shape
    full = lambda shape: pl.BlockSpec(shape, lambda bi: (0,) * len(shape))
    return pl.pallas_call(
        _conv_kernel,
        grid=(b,),
        in_specs=[
            pl.BlockSpec((1, lp, c), lambda bi: (bi, 0, 0)),
            full((CONV_K, c)), full((1, c)), full((c, c)), full((1, c)), full((1, c)),
        ],
        out_specs=pl.BlockSpec((1, lp, c), lambda bi: (bi, 0, 0)),
        out_shape=jax.ShapeDtypeStruct((b, lp, c), BF16),
        scratch_shapes=[pltpu.VMEM((lp + CONV_PAD, c), F32), pltpu.VMEM((CONV_ROWS, c), F32)],
        compiler_params=pltpu.CompilerParams(
            dimension_semantics=("arbitrary",), vmem_limit_bytes=56 * 1024 * 1024),
        name="conformer_conv",
    )(u, conv_w, conv_b, gavg, gn_g, gn_b)


def _store_row_tiles(ref, val):
    for k in range(ROW_TILES):
        ref[:, k, :] = val[:, k * LANES:(k + 1) * LANES]


def _outproj_kernel(a_ref, c_ref, h_ref, wo_ref, g_ref, wq_ref, h1_ref, xn_ref, qp_ref):
    mix = (jnp.dot(a_ref[...], wo_ref[0:ATTN_WIDTH, :], preferred_element_type=F32)
           + jnp.dot(c_ref[...], wo_ref[ATTN_WIDTH:, :], preferred_element_type=F32))
    h1 = h_ref[...] + mix
    h1_ref[...] = h1
    ms = jnp.mean(h1 * h1, axis=-1, keepdims=True)
    xn = h1 * lax.rsqrt(ms + EPS) * g_ref[...]
    _store_row_tiles(xn_ref, xn)
    qp = jnp.dot(xn.astype(BF16), wq_ref[...], preferred_element_type=F32)
    for hp in range(2 * PEER_HEADS):
        qp_ref[hp] = qp[:, hp * PEER_KEYS:(hp + 1) * PEER_KEYS].astype(BF16)


def _outproj(attn2d, conv2d, h2d, wo_bf16, g2, wq_bf16):
    t = h2d.shape[0]
    nq = wq_bf16.shape[1]
    return pl.pallas_call(
        _outproj_kernel,
        grid=(t // TM,),
        in_specs=[
            pl.BlockSpec((TM, ATTN_WIDTH), lambda i: (i, 0)),
            pl.BlockSpec((TM, CONV_WIDTH), lambda i: (i, 0)),
            pl.BlockSpec((TM, D_MODEL), lambda i: (i, 0)),
            pl.BlockSpec((D_MODEL, D_MODEL), lambda i: (0, 0)),
            pl.BlockSpec((1, D_MODEL), lambda i: (0, 0)),
            pl.BlockSpec((D_MODEL, nq), lambda i: (0, 0)),
        ],
        out_specs=[
            pl.BlockSpec((TM, D_MODEL), lambda i: (i, 0)),
            pl.BlockSpec((TM, ROW_TILES, LANES), lambda i: (i, 0, 0)),
            pl.BlockSpec((2 * PEER_HEADS, TM, PEER_KEYS), lambda i: (0, i, 0)),
        ],
        out_shape=[
            jax.ShapeDtypeStruct((t, D_MODEL), F32),
            jax.ShapeDtypeStruct((t, ROW_TILES, LANES), F32),
            jax.ShapeDtypeStruct((2 * PEER_HEADS, t, PEER_KEYS), BF16),
        ],
        compiler_params=pltpu.CompilerParams(
            dimension_semantics=("arbitrary",), vmem_limit_bytes=48 * 1024 * 1024),
        name="outproj_peerq",
    )(attn2d, conv2d, h2d, wo_bf16, g2, wq_bf16)


def _merge_network(n):
    pairs = []

    def merge(lo, hi, r):
        step = r * 2
        if step < hi - lo:
            merge(lo, hi, step)
            merge(lo + r, hi, step)
            pairs.extend((i, i + r) for i in range(lo + r, hi - r, step))
        else:
            pairs.append((lo, lo + r))

    def sort(lo, hi):
        if hi - lo >= 1:
            mid = lo + (hi - lo) // 2
            sort(lo, mid)
            sort(mid + 1, hi)
            merge(lo, hi, 1)

    sort(0, n - 1)
    return tuple(pairs)


SORT16 = _merge_network(PEER_KEYS // SUBLANES)


def _sort_lists(vals, ids):
    v, d = list(vals), list(ids)
    for i, j in SORT16:
        swap = (v[j] > v[i]) | ((v[j] == v[i]) & (d[j] < d[i]))
        v[i], v[j] = jnp.where(swap, v[j], v[i]), jnp.where(swap, v[i], v[j])
        d[i], d[j] = jnp.where(swap, d[j], d[i]), jnp.where(swap, d[i], d[j])
    return v, d


def _row_writer(val_ref, idx_ref):
    def emit(r, val, idx):
        val_ref[r:r + 1, :] = val
        idx_ref[r:r + 1, :] = idx
    return emit


def _top_keys(st, emit):
    tokens = st.shape[1]
    sub = lax.broadcasted_iota(I32, (SUBLANES, tokens), 0)
    n = st.shape[0] // SUBLANES
    v, d = _sort_lists([st[k * SUBLANES:(k + 1) * SUBLANES, :] for k in range(n)],
                       [sub + k * SUBLANES for k in range(n)])
    for r in range(PEER_TOPK):
        m = jnp.max(v[0], axis=0, keepdims=True)
        imin = jnp.min(jnp.where(v[0] == m, d[0], st.shape[0]), axis=0, keepdims=True)
        emit(r, m, imin)
        sel = d[0] == imin
        for k in range(PEER_TOPK - 1 - r):
            v[k] = jnp.where(sel, v[k + 1], v[k])
            d[k] = jnp.where(sel, d[k + 1], d[k])


def _top_sums(s1, s2, emit):
    tokens = s1.shape[1]
    sub = lax.broadcasted_iota(I32, (SUBLANES, tokens), 0)
    big = PEER_TOPK * PEER_TOPK
    v = [s1[0:SUBLANES, :] + s2[b:b + 1, :] for b in range(PEER_TOPK)]
    ptr = jnp.zeros((SUBLANES, tokens), I32)
    e = s1[SUBLANES:, :] + s2[0:1, :]
    eid = (sub + SUBLANES) * PEER_TOPK
    for r in range(PEER_TOPK):
        hid = sub * PEER_TOPK + ptr
        m = jnp.max(jnp.maximum(v[0], e), axis=0, keepdims=True)
        cand = jnp.minimum(jnp.where(v[0] == m, hid, big), jnp.where(e == m, eid, big))
        imin = jnp.min(cand, axis=0, keepdims=True)
        emit(r, m, imin)
        sel = hid == imin
        e = jnp.where(eid == imin, NEG, e)
        for k in range(PEER_TOPK - 1 - r):
            v[k] = jnp.where(sel, v[k + 1], v[k])
        ptr = jnp.where(sel, ptr + 1, ptr)


def _topk_kernel(qp_ref, keys_ref, r_ref, sh_ref, gate_ref,
                 s1_ref, i1_ref, s2_ref, i2_ref, ct_ref, ci_ref, e_ref, g_ref):
    def head(h, slot):
        s1_s, i1_s, s2_s, i2_s, ct_s, ci_s = (
            ref.at[slot] for ref in (s1_ref, i1_ref, s2_ref, i2_ref, ct_ref, ci_ref))
        for p, (sv, si) in enumerate(((s1_s, i1_s), (s2_s, i2_s))):
            hp = 2 * h + p
            st = lax.dot_general(keys_ref[hp], qp_ref[hp], (((1,), (1,)), ((), ())),
                                 preferred_element_type=F32)
            _top_keys(st, _row_writer(sv, si))
        _top_sums(s1_s[...], s2_s[...], _row_writer(ct_s, ci_s))
        ct = ct_s[...]
        ci = ci_s[...]
        hi = ci >> 4
        lo = ci & (PEER_TOPK - 1)
        i1 = i1_s[...]
        i2 = i2_s[...]
        e1 = jnp.zeros_like(ci)
        e2 = jnp.zeros_like(ci)
        for a in range(PEER_TOPK):
            e1 = jnp.where(hi == a, i1[a:a + 1, :], e1)
            e2 = jnp.where(lo == a, i2[a:a + 1, :], e2)
        e = e1 * PEER_KEYS + e2
        ex = jnp.exp(ct - jnp.max(ct, axis=0, keepdims=True))
        gate = ex / jnp.sum(ex, axis=0, keepdims=True)
        row0 = pl.multiple_of(h * PEER_TOPK, PEER_TOPK)
        e_ref[pl.ds(row0, PEER_TOPK), :] = e
        g_ref[pl.ds(row0, PEER_TOPK), :] = gate

    def heads(g, _):
        for slot in range(TOPK_HEADS_PER_TRIP):
            head(g * TOPK_HEADS_PER_TRIP + slot, slot)
        return 0

    lax.fori_loop(0, PEER_HEADS // TOPK_HEADS_PER_TRIP, heads, 0)
    e = e_ref[...]
    sh_ref[...] = ((e >> 13) << 4).astype(F32).T
    gate_ref[...] = g_ref[...].T
    e_ref[...] = (e & (HALF_EXPERTS - 1)) * ROW_TILES
    for k in range(OFF_STREAMS):
        r_ref[0, k] = e_ref[pl.ds(k, PEER_PAIRS // OFF_STREAMS, stride=OFF_STREAMS), :]


def _real_tile_start(i, tile, lp, seq_real):
    per_batch = seq_real // tile
    return pl.multiple_of((i // per_batch) * lp + N_META + (i % per_batch) * tile, N_META)


def _peer_topk(qp, keys_bf16, lp, seq_real):
    n_real = (qp.shape[1] // lp) * seq_real
    tt = TOPK_TOKENS
    sc = lambda dt: pltpu.VMEM((TOPK_HEADS_PER_TRIP, PEER_TOPK, tt), dt)
    qp_block = tuple(pl.Element(d) for d in (2 * PEER_HEADS, tt, PEER_KEYS))
    return pl.pallas_call(
        _topk_kernel,
        grid=(n_real // tt,),
        in_specs=[
            pl.BlockSpec(qp_block, lambda i: (0, _real_tile_start(i, tt, lp, seq_real), 0)),
            pl.BlockSpec((2 * PEER_HEADS, PEER_KEYS, PEER_KEYS), lambda i: (0, 0, 0)),
        ],
        out_specs=[
            pl.BlockSpec((1, OFF_STREAMS, PEER_PAIRS // OFF_STREAMS, tt), lambda i: (i, 0, 0, 0)),
            pl.BlockSpec((tt, PEER_PAIRS), lambda i: (i, 0)),
            pl.BlockSpec((tt, PEER_PAIRS), lambda i: (i, 0)),
        ],
        out_shape=[
            jax.ShapeDtypeStruct((n_real // tt, OFF_STREAMS, PEER_PAIRS // OFF_STREAMS, tt), I32),
            jax.ShapeDtypeStruct((n_real, PEER_PAIRS), F32),
            jax.ShapeDtypeStruct((n_real, PEER_PAIRS), F32),
        ],
        scratch_shapes=[sc(F32), sc(I32), sc(F32), sc(I32), sc(F32), sc(I32),
                        pltpu.VMEM((PEER_PAIRS, tt), I32), pltpu.VMEM((PEER_PAIRS, tt), F32)],
        compiler_params=pltpu.CompilerParams(dimension_semantics=("arbitrary",)),
        name="peer_topk",
    )(qp, keys_bf16)


def _pack_kernel(hi_ref, lo_ref, o_ref):
    hi = pltpu.bitcast(hi_ref[...].astype(BF16).astype(F32), U32)
    lo = pltpu.bitcast(lo_ref[...].astype(BF16).astype(F32), U32)
    _store_row_tiles(o_ref, hi | (lo >> 16))


def _pack_table(tab):
    rows = 512
    nb = HALF_EXPERTS // rows
    packed = pl.pallas_call(
        _pack_kernel,
        grid=(nb,),
        in_specs=[pl.BlockSpec((rows, D_MODEL), lambda i: (i, 0)),
                  pl.BlockSpec((rows, D_MODEL), lambda i: (i + nb, 0))],
        out_specs=pl.BlockSpec((rows, ROW_TILES, LANES), lambda i: (i, 0, 0)),
        out_shape=jax.ShapeDtypeStruct((HALF_EXPERTS, ROW_TILES, LANES), U32),
        compiler_params=pltpu.CompilerParams(dimension_semantics=("arbitrary",)),
        name="pack_table",
    )(tab, tab)
    return packed.reshape(HALF_EXPERTS * ROW_TILES, LANES)


def _expert_row(tab_ref, off, shv):
    word = tab_ref[pl.ds(pl.multiple_of(off, SUBLANES), SUBLANES), :]
    return pltpu.bitcast(jnp.left_shift(word, shv) & jnp.uint32(0xFFFF0000), F32)


def _rows_to_lanes(row):
    return jnp.transpose(jnp.broadcast_to(row, (PEER_PAIRS, LANES)))


def _shift_rows(shf_row):
    ri = lax.broadcasted_iota(I32, (PEER_PAIRS, PEER_PAIRS), 0)
    ci = lax.broadcasted_iota(I32, (PEER_PAIRS, PEER_PAIRS), 1)
    diag = jnp.where(ri == ci, shf_row, 0.0).astype(BF16)
    rep = jnp.dot(diag, jnp.ones((PEER_PAIRS, LANES), BF16), preferred_element_type=F32)
    return pltpu.bitcast(rep, U32) >> 26


def _fold8(prods):
    sub = lax.broadcasted_iota(I32, (SUBLANES, LANES), 0)
    cur = prods
    for sh in (1, 2, 4):
        keep = (sub & sh) == 0
        nxt = []
        for k in range(0, len(cur), 2):
            a = jnp.where(keep, cur[k], cur[k + 1])
            b = jnp.where(keep, cur[k + 1], cur[k])
            nxt.append(a + pltpu.roll(b, sh, axis=0))
        cur = nxt
    return cur[0]


def _peer_act_kernel(*refs):
    off_refs = refs[:OFF_STREAMS]
    shf_ref, x_ref, gate_ref, tab_ref, w_ref, shb_ref, a_ref = refs[OFF_STREAMS:]

    def prep(t, slot):
        shb_ref[slot] = _shift_rows(shf_ref[pl.ds(jnp.minimum(t, PEER_TB - 1), 1), :])

    pair_slot = (lax.broadcasted_iota(I32, (SUBLANES, LANES), 1)
                 - lax.broadcasted_iota(I32, (SUBLANES, LANES), 0))

    def pairs(t, slot):
        xt = x_ref[t]
        t0 = _offset_index(t)
        spread = jnp.zeros((SUBLANES, LANES), F32)
        for g in range(PEER_PAIRS // SUBLANES):
            prods = []
            for jj in range(g * SUBLANES, (g + 1) * SUBLANES):
                off = off_refs[jj % OFF_STREAMS][t0 + (jj // OFF_STREAMS) * TOPK_TOKENS]
                f = _expert_row(tab_ref, off, shb_ref[slot, jj:jj + 1, :])
                prods.append(f * xt)
            sums = jnp.sum(_fold8(prods), axis=1, keepdims=True)
            spread = jnp.where(pair_slot == g * SUBLANES, sums, spread)
        a_ref[pl.ds(t, 1), :] = jnp.sum(spread, axis=0, keepdims=True)

    prep(0, 0)

    def tokens(i, _):
        for k in range(ACT_TOKENS_PER_TRIP):
            t = i * ACT_TOKENS_PER_TRIP + k
            prep(t + 1, (k + 1) % 2)
            pairs(t, k % 2)
        return 0

    lax.fori_loop(0, PEER_TB // ACT_TOKENS_PER_TRIP, tokens, 0)
    a = a_ref[...]
    act = 0.5 * a * (1.0 + lax.erf(a * (2.0 ** -0.5)))
    w_ref[...] = gate_ref[...] * act


def _peer_specs(lp, seq_real):
    per_tile = TOPK_TOKENS // PEER_TB
    stream_len = (PEER_PAIRS // OFF_STREAMS) * TOPK_TOKENS
    offsets = [pl.BlockSpec((stream_len,), lambda i, k=k: ((i // per_tile) * OFF_STREAMS + k,),
                            memory_space=pltpu.SMEM) for k in range(OFF_STREAMS)]
    rows = lambda *tail: pl.BlockSpec(
        (pl.Element(PEER_TB),) + tuple(pl.Element(d) for d in tail),
        lambda i: (_real_tile_start(i, PEER_TB, lp, seq_real),) + (0,) * len(tail))
    return offsets, rows


def _offset_index(t):
    return (pl.program_id(0) % (TOPK_TOKENS // PEER_TB)) * PEER_TB + t


def _peer_act(off, shf, x3, gate, tab, lp, seq_real):
    n_real = shf.shape[0]
    tile = lambda: pl.BlockSpec((PEER_TB, PEER_PAIRS), lambda i: (i, 0))
    offsets, rows = _peer_specs(lp, seq_real)
    return pl.pallas_call(
        _peer_act_kernel,
        grid=(n_real // PEER_TB,),
        in_specs=offsets + [
            tile(),
            rows(ROW_TILES, LANES),
            tile(),
            pl.BlockSpec((HALF_EXPERTS * ROW_TILES, LANES), lambda i: (0, 0),
                         pipeline_mode=pl.Buffered(1)),
        ],
        out_specs=pl.BlockSpec((PEER_TB, PEER_PAIRS), lambda i: (i, 0)),
        out_shape=jax.ShapeDtypeStruct((n_real, PEER_PAIRS), F32),
        scratch_shapes=[pltpu.VMEM((2, PEER_PAIRS, LANES), U32),
                        pltpu.VMEM((PEER_TB, PEER_PAIRS), F32)],
        compiler_params=pltpu.CompilerParams(
            dimension_semantics=("arbitrary",), vmem_limit_bytes=VMEM_TABLE_LIMIT),
        name="peer_act",
    )(*off, shf, x3, gate, tab)


def _peer_out_kernel(*refs):
    off_refs = refs[:OFF_STREAMS]
    shf_ref, w_ref, h_ref, g_ref, tab_ref, o_ref, shb_ref, wb_ref, ffn_ref = refs[OFF_STREAMS:]
    n_acc = 4

    def prep(t, _):
        r0 = pl.multiple_of(t * PEER_PAIRS, PEER_PAIRS)
        shb_ref[pl.ds(r0, PEER_PAIRS), :] = _shift_rows(shf_ref[pl.ds(t, 1), :])
        wb_ref[pl.ds(r0, PEER_PAIRS), :] = _rows_to_lanes(w_ref[pl.ds(t, 1), :])
        return 0

    lax.fori_loop(0, PEER_TB, prep, 0, unroll=PREP_UNROLL)

    def token(t, _):
        j0 = pl.multiple_of(t * PEER_PAIRS, PEER_PAIRS)
        m0 = _offset_index(t)
        accs = [jnp.zeros((SUBLANES, LANES), F32) for _ in range(n_acc)]
        for jj in range(PEER_PAIRS):
            off = off_refs[jj % OFF_STREAMS][m0 + (jj // OFF_STREAMS) * TOPK_TOKENS]
            f = _expert_row(tab_ref, off, shb_ref[pl.ds(j0 + jj, 1), :])
            accs[jj % n_acc] = accs[jj % n_acc] + wb_ref[pl.ds(j0 + jj, 1), :] * f
        ffn_ref[t] = (accs[0] + accs[1]) + (accs[2] + accs[3])
        return 0

    lax.fori_loop(0, PEER_TB, token, 0, unroll=4)
    ffn = jnp.concatenate([ffn_ref[:, k, :] for k in range(ROW_TILES)], axis=1)
    y = h_ref[...] + ffn
    ms = jnp.mean(y * y, axis=-1, keepdims=True)
    o_ref[...] = y * lax.rsqrt(ms + EPS) * g_ref[...]


def _peer_out(off, shf, w, h2d, g, tab, lp, seq_real):
    n_real = w.shape[0]
    offsets, rows = _peer_specs(lp, seq_real)
    return pl.pallas_call(
        _peer_out_kernel,
        grid=(n_real // PEER_TB,),
        in_specs=offsets + [
            pl.BlockSpec((PEER_TB, PEER_PAIRS), lambda i: (i, 0)),
            pl.BlockSpec((PEER_TB, PEER_PAIRS), lambda i: (i, 0)),
            rows(D_MODEL),
            pl.BlockSpec((1, D_MODEL), lambda i: (0, 0)),
            pl.BlockSpec((HALF_EXPERTS * ROW_TILES, LANES), lambda i: (0, 0),
                         pipeline_mode=pl.Buffered(1)),
        ],
        out_specs=pl.BlockSpec((PEER_TB, D_MODEL), lambda i: (i, 0)),
        out_shape=jax.ShapeDtypeStruct((n_real, D_MODEL), F32),
        scratch_shapes=[pltpu.VMEM((PEER_TB * PEER_PAIRS, LANES), U32),
                        pltpu.VMEM((PEER_TB * PEER_PAIRS, LANES), F32),
                        pltpu.VMEM((PEER_TB, ROW_TILES, LANES), F32)],
        compiler_params=pltpu.CompilerParams(
            dimension_semantics=("arbitrary",), vmem_limit_bytes=VMEM_TABLE_LIMIT),
        name="peer_out",
    )(*off, shf, w, h2d, g, tab)


def kernel(x, meta_tokens, norm1_g, w_in, lambda_q1, lambda_k1, lambda_q2, lambda_k2,
           attn_subln_g, conv_w, conv_b, conv_norm_g, conv_norm_b, w_out, norm2_g,
           peer_wq, peer_subkeys, peer_u, peer_v, final_norm_g):
    b, s, _ = x.shape
    seq = N_META + s
    lp = ((seq + Q_BLOCK - 1) // Q_BLOCK) * Q_BLOCK
    t = b * lp
    assert lp % TQ == 0 and lp % CONV_ROWS == 0 and t % TM == 0
    assert s % TOPK_TOKENS == 0 and TOPK_TOKENS % PEER_TB == 0

    meta = jnp.broadcast_to(meta_tokens[None].astype(x.dtype), (b, N_META, D_MODEL))
    h = jnp.concatenate([meta, x, jnp.zeros((b, lp - seq, D_MODEL), x.dtype)], axis=1)
    h2d = h.reshape(t, D_MODEL)

    lam_init = 0.8 - 0.6 * math.exp(-0.3 * 0)
    slopes = jnp.asarray([2.0 ** (-8.0 * (i + 1) / N_HEADS) for i in range(N_HEADS)], F32)
    group = jnp.arange(CONV_WIDTH) // CONV_GROUP
    gavg = ((group[:, None] == group[None, :]).astype(F32) * (1.0 / CONV_GROUP)).astype(BF16)

    q, k, v, u = _inproj(h2d, norm1_g[0][None], w_in[0].astype(BF16))
    attn = _attention(q.reshape(b, lp, -1), k.reshape(b, lp, -1), v.reshape(b, lp, -1), slopes,
                      lambda_q1[0][None], lambda_k1[0][None], lambda_q2[0][None],
                      lambda_k2[0][None], attn_subln_g[0][None], lam_init)
    conv = _conformer_conv(u.reshape(b, lp, -1), conv_w[0], conv_b[0][None], gavg,
                           conv_norm_g[0][None], conv_norm_b[0][None])
    h1, xn2, qp = _outproj(attn.reshape(t, -1), conv.reshape(t, -1), h2d,
                           w_out[0].astype(BF16), norm2_g[0][None], peer_wq[0].astype(BF16))
    keys = peer_subkeys[0].reshape(2 * PEER_HEADS, PEER_KEYS, PEER_KEYS).astype(BF16)
    off, shf, gate = _peer_topk(qp, keys, lp, s)
    off = [off.reshape(-1)] * OFF_STREAMS
    w = _peer_act(off, shf, xn2, gate, _pack_table(peer_u[0]), lp, s)
    out = _peer_out(off, shf, w, h1, final_norm_g[None], _pack_table(peer_v[0]), lp, s)
    return out.reshape(b, s, D_MODEL)
```

```python
import functools
import math

import jax
import jax.numpy as jnp
from jax import lax
from jax.experimental import pallas as pl
from jax.experimental.pallas import tpu as pltpu

F32 = jnp.float32
BF16 = jnp.bfloat16
I32 = jnp.int32
U32 = jnp.uint32

D_MODEL = 1024
N_META = 16
Q_BLOCK = 128
ATTN_WIDTH = 512
CONV_WIDTH = 512
N_HEADS = 4
HEAD_DIM = 64
V_DIM = 128
CONV_K = 31
CONV_GROUP = 64
PEER_HEADS = 8
PEER_KEYS = 128
PEER_TOPK = 16
PEER_PAIRS = PEER_HEADS * PEER_TOPK
N_EXPERTS = PEER_KEYS * PEER_KEYS
HALF_EXPERTS = N_EXPERTS // 2
EPS = 1e-6
NEG = -1e30
LOG2E = math.log2(math.e)

LANES = 128
SUBLANES = 8
ROW_TILES = D_MODEL // LANES

TM = 512
TQ = 384
CONV_ROWS = 128
CONV_PAD = 32
TOPK_TOKENS = 128
TOPK_HEADS_PER_TRIP = 8
PEER_TB = 128
ACT_TOKENS_PER_TRIP = 32
OFF_STREAMS = 8
PREP_UNROLL = 8
VMEM_TABLE_LIMIT = 52 * 1024 * 1024


def _inproj_kernel(h_ref, g_ref, w_ref, q_ref, k_ref, v_ref, u_ref):
    x = h_ref[...]
    ms = jnp.mean(x * x, axis=-1, keepdims=True)
    xn = (x * lax.rsqrt(ms + EPS) * g_ref[...]).astype(BF16)
    proj = jnp.dot(xn, w_ref[...], preferred_element_type=F32)
    q_ref[...] = (proj[:, 0:ATTN_WIDTH] * (HEAD_DIM ** -0.5 * LOG2E)).astype(BF16)
    k_ref[...] = proj[:, ATTN_WIDTH:2 * ATTN_WIDTH].astype(BF16)
    v_ref[...] = proj[:, 2 * ATTN_WIDTH:3 * ATTN_WIDTH].astype(BF16)
    ga = proj[:, 3 * ATTN_WIDTH:3 * ATTN_WIDTH + CONV_WIDTH]
    gg = proj[:, 3 * ATTN_WIDTH + CONV_WIDTH:]
    u_ref[...] = ga * jax.nn.sigmoid(gg)


def _inproj(h2d, g, w_bf16):
    t = h2d.shape[0]
    n_cols = w_bf16.shape[1]
    return pl.pallas_call(
        _inproj_kernel,
        grid=(t // TM,),
        in_specs=[
            pl.BlockSpec((TM, D_MODEL), lambda i: (i, 0)),
            pl.BlockSpec((1, D_MODEL), lambda i: (0, 0)),
            pl.BlockSpec((D_MODEL, n_cols), lambda i: (0, 0)),
        ],
        out_specs=[
            pl.BlockSpec((TM, ATTN_WIDTH), lambda i: (i, 0)),
            pl.BlockSpec((TM, ATTN_WIDTH), lambda i: (i, 0)),
            pl.BlockSpec((TM, ATTN_WIDTH), lambda i: (i, 0)),
            pl.BlockSpec((TM, CONV_WIDTH), lambda i: (i, 0)),
        ],
        out_shape=[
            jax.ShapeDtypeStruct((t, ATTN_WIDTH), BF16),
            jax.ShapeDtypeStruct((t, ATTN_WIDTH), BF16),
            jax.ShapeDtypeStruct((t, ATTN_WIDTH), BF16),
            jax.ShapeDtypeStruct((t, CONV_WIDTH), F32),
        ],
        compiler_params=pltpu.CompilerParams(
            dimension_semantics=("arbitrary",), vmem_limit_bytes=48 * 1024 * 1024),
        name="inproj",
    )(h2d, g, w_bf16)


def _attn_kernel(slopes_ref, lq1_ref, lk1_ref, lq2_ref, lk2_ref, subg_ref,
                 q_ref, k_ref, v_ref, o_ref, *, lam_init):
    hd = pl.program_id(1)
    qi = pl.program_id(2)
    slope = slopes_ref[hd]
    lam = (jnp.exp(jnp.sum(lq1_ref[...] * lk1_ref[...], keepdims=True))
           - jnp.exp(jnp.sum(lq2_ref[...] * lk2_ref[...], keepdims=True)) + lam_init)

    q = q_ref[0]
    lane = lax.broadcasted_iota(I32, q.shape, 1)
    zero = jnp.zeros_like(q)
    qs = jnp.concatenate([jnp.where(lane < HEAD_DIM, q, zero),
                          jnp.where(lane >= HEAD_DIM, q, zero)], axis=0)

    q0 = qi * TQ
    col = lax.broadcasted_iota(I32, (1, TQ), 1)

    def step(j, carry, masked):
        m, l, acc = carry
        k0 = pl.multiple_of(j * TQ, TQ)
        kj = k_ref[0, pl.ds(k0, TQ), :]
        vj = v_ref[0, pl.ds(k0, TQ), :]
        s = lax.dot_general(qs, kj, (((1,), (1,)), ((), ())), preferred_element_type=F32)
        s = s + (slope * LOG2E) * (col + (k0 - q0)).astype(F32)
        if masked:
            row = lax.broadcasted_iota(I32, (2 * TQ, TQ), 0)
            row = jnp.where(row >= TQ, row - TQ, row)
            cc = lax.broadcasted_iota(I32, (2 * TQ, TQ), 1)
            s = jnp.where(cc <= row, s, NEG)
        m_new = jnp.maximum(m, jnp.max(s, axis=1, keepdims=True))
        alpha = jnp.exp2(m - m_new)
        p = jnp.exp2(s - m_new)
        l = alpha * l + jnp.sum(p, axis=1, keepdims=True)
        acc = alpha * acc + jnp.dot(p.astype(BF16), vj, preferred_element_type=F32)
        return m_new, l, acc

    init = (jnp.full((2 * TQ, 1), NEG, F32), jnp.zeros((2 * TQ, 1), F32),
            jnp.zeros((2 * TQ, V_DIM), F32))
    carry = lax.fori_loop(0, qi, lambda j, c: step(j, c, False), init)
    m, l, acc = step(qi, carry, True)
    o = acc / l
    a = o[:TQ] - lam * o[TQ:]
    ms = jnp.mean(a * a, axis=-1, keepdims=True)
    y = a * lax.rsqrt(ms + EPS) * subg_ref[...] * (1.0 - lam_init)
    o_ref[0] = y.astype(BF16)


def _attention(q, k, v, slopes, lq1, lk1, lq2, lk2, subg, lam_init):
    b, lp, _ = q.shape
    nq = lp // TQ
    vec = lambda n: pl.BlockSpec((1, n), lambda bi, hi, i: (0, 0))
    return pl.pallas_call(
        functools.partial(_attn_kernel, lam_init=lam_init),
        grid=(b, N_HEADS, nq),
        in_specs=[
            pl.BlockSpec(memory_space=pltpu.SMEM),
            vec(HEAD_DIM), vec(HEAD_DIM), vec(HEAD_DIM), vec(HEAD_DIM), vec(V_DIM),
            pl.BlockSpec((1, TQ, V_DIM), lambda bi, hi, i: (bi, i, hi)),
            pl.BlockSpec((1, lp, V_DIM), lambda bi, hi, i: (bi, 0, hi)),
            pl.BlockSpec((1, lp, V_DIM), lambda bi, hi, i: (bi, 0, hi)),
        ],
        out_specs=pl.BlockSpec((1, TQ, V_DIM), lambda bi, hi, i: (bi, i, hi)),
        out_shape=jax.ShapeDtypeStruct((b, lp, ATTN_WIDTH), BF16),
        compiler_params=pltpu.CompilerParams(
            dimension_semantics=("arbitrary", "arbitrary", "arbitrary"),
            vmem_limit_bytes=48 * 1024 * 1024),
        name="diff_attn",
    )(slopes, lq1, lk1, lq2, lk2, subg, q, k, v)


def _group_mean(v, gavg_bf16):
    hi = v.astype(BF16)
    lo = (v - hi.astype(F32)).astype(BF16)
    return (jnp.dot(hi, gavg_bf16, preferred_element_type=F32)
            + jnp.dot(lo, gavg_bf16, preferred_element_type=F32))


def _conv_kernel(u_ref, w_ref, b_ref, gavg_ref, g_ref, beta_ref, o_ref, upad_ref, y_ref):
    lp = u_ref.shape[1]
    upad_ref[0:CONV_PAD, :] = jnp.zeros((CONV_PAD, CONV_WIDTH), F32)
    upad_ref[CONV_PAD:, :] = u_ref[0]

    def chunk(c, _):
        base = pl.multiple_of(c * CONV_ROWS, CONV_ROWS)
        for lb in range(CONV_WIDTH // LANES):
            ls = slice(lb * LANES, (lb + 1) * LANES)
            acc = jnp.zeros((CONV_ROWS, LANES), F32) + b_ref[:, ls]
            win = upad_ref[pl.ds(base, CONV_ROWS + CONV_PAD), ls]
            shifted = [win] + [jnp.roll(win, -rho, axis=0) for rho in range(1, SUBLANES)]
            for t in range(CONV_K):
                off = CONV_PAD - (CONV_K - 1) + t
                rho = off % SUBLANES
                acc = acc + w_ref[t:t + 1, ls] * shifted[rho][off - rho:off - rho + CONV_ROWS, :]
            y_ref[:, ls] = acc
        y = y_ref[...]
        mu = _group_mean(y, gavg_ref[...])
        d = y - mu
        var = _group_mean(d * d, gavg_ref[...])
        yn = d * lax.rsqrt(var + EPS) * g_ref[...] + beta_ref[...]
        o_ref[0, pl.ds(base, CONV_ROWS), :] = (yn * jax.nn.sigmoid(yn)).astype(BF16)
        return 0

    lax.fori_loop(0, lp // CONV_ROWS, chunk, 0)


def _conformer_conv(u, conv_w, conv_b, gavg, gn_g, gn_b):
    b, lp, c = u.shape
    full = lambda shape: pl.BlockSpec(shape, lambda bi: (0,) * len(shape))
    return pl.pallas_call(
        _conv_kernel,
        grid=(b,),
        in_specs=[
            pl.BlockSpec((1, lp, c), lambda bi: (bi, 0, 0)),
            full((CONV_K, c)), full((1, c)), full((c, c)), full((1, c)), full((1, c)),
        ],
        out_specs=pl.BlockSpec((1, lp, c), lambda bi: (bi, 0, 0)),
        out_shape=jax.ShapeDtypeStruct((b, lp, c), BF16),
        scratch_shapes=[pltpu.VMEM((lp + CONV_PAD, c), F32), pltpu.VMEM((CONV_ROWS, c), F32)],
        compiler_params=pltpu.CompilerParams(
            dimension_semantics=("arbitrary",), vmem_limit_bytes=56 * 1024 * 1024),
        name="conformer_conv",
    )(u, conv_w, conv_b, gavg, gn_g, gn_b)


def _store_row_tiles(ref, val):
    for k in range(ROW_TILES):
        ref[:, k, :] = val[:, k * LANES:(k + 1) * LANES]


def _outproj_kernel(a_ref, c_ref, h_ref, wo_ref, g_ref, wq_ref, h1_ref, xn_ref, qp_ref):
    mix = (jnp.dot(a_ref[...], wo_ref[0:ATTN_WIDTH, :], preferred_element_type=F32)
           + jnp.dot(c_ref[...], wo_ref[ATTN_WIDTH:, :], preferred_element_type=F32))
    h1 = h_ref[...] + mix
    h1_ref[...] = h1
    ms = jnp.mean(h1 * h1, axis=-1, keepdims=True)
    xn = h1 * lax.rsqrt(ms + EPS) * g_ref[...]
    _store_row_tiles(xn_ref, xn)
    qp = jnp.dot(xn.astype(BF16), wq_ref[...], preferred_element_type=F32)
    for hp in range(2 * PEER_HEADS):
        qp_ref[hp] = qp[:, hp * PEER_KEYS:(hp + 1) * PEER_KEYS].astype(BF16)


def _outproj(attn2d, conv2d, h2d, wo_bf16, g2, wq_bf16):
    t = h2d.shape[0]
    nq = wq_bf16.shape[1]
    return pl.pallas_call(
        _outproj_kernel,
        grid=(t // TM,),
        in_specs=[
            pl.BlockSpec((TM, ATTN_WIDTH), lambda i: (i, 0)),
            pl.BlockSpec((TM, CONV_WIDTH), lambda i: (i, 0)),
            pl.BlockSpec((TM, D_MODEL), lambda i: (i, 0)),
            pl.BlockSpec((D_MODEL, D_MODEL), lambda i: (0, 0)),
            pl.BlockSpec((1, D_MODEL), lambda i: (0, 0)),
            pl.BlockSpec((D_MODEL, nq), lambda i: (0, 0)),
        ],
        out_specs=[
            pl.BlockSpec((TM, D_MODEL), lambda i: (i, 0)),
            pl.BlockSpec((TM, ROW_TILES, LANES), lambda i: (i, 0, 0)),
            pl.BlockSpec((2 * PEER_HEADS, TM, PEER_KEYS), lambda i: (0, i, 0)),
        ],
        out_shape=[
            jax.ShapeDtypeStruct((t, D_MODEL), F32),
            jax.ShapeDtypeStruct((t, ROW_TILES, LANES), F32),
            jax.ShapeDtypeStruct((2 * PEER_HEADS, t, PEER_KEYS), BF16),
        ],
        compiler_params=pltpu.CompilerParams(
            dimension_semantics=("arbitrary",), vmem_limit_bytes=48 * 1024 * 1024),
        name="outproj_peerq",
    )(attn2d, conv2d, h2d, wo_bf16, g2, wq_bf16)


def _merge_network(n):
    pairs = []

    def merge(lo, hi, r):
        step = r * 2
        if step < hi - lo:
            merge(lo, hi, step)
            merge(lo + r, hi, step)
            pairs.extend((i, i + r) for i in range(lo + r, hi - r, step))
        else:
            pairs.append((lo, lo + r))

    def sort(lo, hi):
        if hi - lo >= 1:
            mid = lo + (hi - lo) // 2
            sort(lo, mid)
            sort(mid + 1, hi)
            merge(lo, hi, 1)

    sort(0, n - 1)
    return tuple(pairs)


SORT16 = _merge_network(PEER_KEYS // SUBLANES)


def _sort_lists(vals, ids):
    v, d = list(vals), list(ids)
    for i, j in SORT16:
        swap = (v[j] > v[i]) | ((v[j] == v[i]) & (d[j] < d[i]))
        v[i], v[j] = jnp.where(swap, v[j], v[i]), jnp.where(swap, v[i], v[j])
        d[i], d[j] = jnp.where(swap, d[j], d[i]), jnp.where(swap, d[i], d[j])
    return v, d


def _row_writer(val_ref, idx_ref):
    def emit(r, val, idx):
        val_ref[r:r + 1, :] = val
        idx_ref[r:r + 1, :] = idx
    return emit


def _top_keys(st, emit):
    tokens = st.shape[1]
    sub = lax.broadcasted_iota(I32, (SUBLANES, tokens), 0)
    n = st.shape[0] // SUBLANES
    v, d = _sort_lists([st[k * SUBLANES:(k + 1) * SUBLANES, :] for k in range(n)],
                       [sub + k * SUBLANES for k in range(n)])
    for r in range(PEER_TOPK):
        m = jnp.max(v[0], axis=0, keepdims=True)
        imin = jnp.min(jnp.where(v[0] == m, d[0], st.shape[0]), axis=0, keepdims=True)
        emit(r, m, imin)
        sel = d[0] == imin
        for k in range(PEER_TOPK - 1 - r):
            v[k] = jnp.where(sel, v[k + 1], v[k])
            d[k] = jnp.where(sel, d[k + 1], d[k])


def _top_sums(s1, s2, emit):
    tokens = s1.shape[1]
    sub = lax.broadcasted_iota(I32, (SUBLANES, tokens), 0)
    big = PEER_TOPK * PEER_TOPK
    v = [s1[0:SUBLANES, :] + s2[b:b + 1, :] for b in range(PEER_TOPK)]
    ptr = jnp.zeros((SUBLANES, tokens), I32)
    e = s1[SUBLANES:, :] + s2[0:1, :]
    eid = (sub + SUBLANES) * PEER_TOPK
    for r in range(PEER_TOPK):
        hid = sub * PEER_TOPK + ptr
        m = jnp.max(jnp.maximum(v[0], e), axis=0, keepdims=True)
        cand = jnp.minimum(jnp.where(v[0] == m, hid, big), jnp.where(e == m, eid, big))
        imin = jnp.min(cand, axis=0, keepdims=True)
        emit(r, m, imin)
        sel = hid == imin
        e = jnp.where(eid == imin, NEG, e)
        for k in range(PEER_TOPK - 1 - r):
            v[k] = jnp.where(sel, v[k + 1], v[k])
        ptr = jnp.where(sel, ptr + 1, ptr)


def _topk_kernel(qp_ref, keys_ref, r_ref, sh_ref, gate_ref,
                 s1_ref, i1_ref, s2_ref, i2_ref, ct_ref, ci_ref, e_ref, g_ref):
    def head(h, slot):
        s1_s, i1_s, s2_s, i2_s, ct_s, ci_s = (
            ref.at[slot] for ref in (s1_ref, i1_ref, s2_ref, i2_ref, ct_ref, ci_ref))
        for p, (sv, si) in enumerate(((s1_s, i1_s), (s2_s, i2_s))):
            hp = 2 * h + p
            st = lax.dot_general(keys_ref[hp], qp_ref[hp], (((1,), (1,)), ((), ())),
                                 preferred_element_type=F32)
            _top_keys(st, _row_writer(sv, si))
        _top_sums(s1_s[...], s2_s[...], _row_writer(ct_s, ci_s))
        ct = ct_s[...]
        ci = ci_s[...]
        hi = ci >> 4
        lo = ci & (PEER_TOPK - 1)
        i1 = i1_s[...]
        i2 = i2_s[...]
        e1 = jnp.zeros_like(ci)
        e2 = jnp.zeros_like(ci)
        for a in range(PEER_TOPK):
            e1 = jnp.where(hi == a, i1[a:a + 1, :], e1)
            e2 = jnp.where(lo == a, i2[a:a + 1, :], e2)
        e = e1 * PEER_KEYS + e2
        ex = jnp.exp(ct - jnp.max(ct, axis=0, keepdims=True))
        gate = ex / jnp.sum(ex, axis=0, keepdims=True)
        row0 = pl.multiple_of(h * PEER_TOPK, PEER_TOPK)
        e_ref[pl.ds(row0, PEER_TOPK), :] = e
        g_ref[pl.ds(row0, PEER_TOPK), :] = gate

    def heads(g, _):
        for slot in range(TOPK_HEADS_PER_TRIP):
            head(g * TOPK_HEADS_PER_TRIP + slot, slot)
        return 0

    lax.fori_loop(0, PEER_HEADS // TOPK_HEADS_PER_TRIP, heads, 0)
    e = e_ref[...]
    sh_ref[...] = ((e >> 13) << 4).astype(F32).T
    gate_ref[...] = g_ref[...].T
    e_ref[...] = (e & (HALF_EXPERTS - 1)) * ROW_TILES
    for k in range(OFF_STREAMS):
        r_ref[0, k] = e_ref[pl.ds(k, PEER_PAIRS // OFF_STREAMS, stride=OFF_STREAMS), :]


def _real_tile_start(i, tile, lp, seq_real):
    per_batch = seq_real // tile
    return pl.multiple_of((i // per_batch) * lp + N_META + (i % per_batch) * tile, N_META)


def _peer_topk(qp, keys_bf16, lp, seq_real):
    n_real = (qp.shape[1] // lp) * seq_real
    tt = TOPK_TOKENS
    sc = lambda dt: pltpu.VMEM((TOPK_HEADS_PER_TRIP, PEER_TOPK, tt), dt)
    qp_block = tuple(pl.Element(d) for d in (2 * PEER_HEADS, tt, PEER_KEYS))
    return pl.pallas_call(
        _topk_kernel,
        grid=(n_real // tt,),
        in_specs=[
            pl.BlockSpec(qp_block, lambda i: (0, _real_tile_start(i, tt, lp, seq_real), 0)),
            pl.BlockSpec((2 * PEER_HEADS, PEER_KEYS, PEER_KEYS), lambda i: (0, 0, 0)),
        ],
        out_specs=[
            pl.BlockSpec((1, OFF_STREAMS, PEER_PAIRS // OFF_STREAMS, tt), lambda i: (i, 0, 0, 0)),
            pl.BlockSpec((tt, PEER_PAIRS), lambda i: (i, 0)),
            pl.BlockSpec((tt, PEER_PAIRS), lambda i: (i, 0)),
        ],
        out_shape=[
            jax.ShapeDtypeStruct((n_real // tt, OFF_STREAMS, PEER_PAIRS // OFF_STREAMS, tt), I32),
            jax.ShapeDtypeStruct((n_real, PEER_PAIRS), F32),
            jax.ShapeDtypeStruct((n_real, PEER_PAIRS), F32),
        ],
        scratch_shapes=[sc(F32), sc(I32), sc(F32), sc(I32), sc(F32), sc(I32),
                        pltpu.VMEM((PEER_PAIRS, tt), I32), pltpu.VMEM((PEER_PAIRS, tt), F32)],
        compiler_params=pltpu.CompilerParams(dimension_semantics=("arbitrary",)),
        name="peer_topk",
    )(qp, keys_bf16)


def _pack_kernel(hi_ref, lo_ref, o_ref):
    hi = pltpu.bitcast(hi_ref[...].astype(BF16).astype(F32), U32)
    lo = pltpu.bitcast(lo_ref[...].astype(BF16).astype(F32), U32)
    _store_row_tiles(o_ref, hi | (lo >> 16))


def _pack_table(tab):
    rows = 512
    nb = HALF_EXPERTS // rows
    packed = pl.pallas_call(
        _pack_kernel,
        grid=(nb,),
        in_specs=[pl.BlockSpec((rows, D_MODEL), lambda i: (i, 0)),
                  pl.BlockSpec((rows, D_MODEL), lambda i: (i + nb, 0))],
        out_specs=pl.BlockSpec((rows, ROW_TILES, LANES), lambda i: (i, 0, 0)),
        out_shape=jax.ShapeDtypeStruct((HALF_EXPERTS, ROW_TILES, LANES), U32),
        compiler_params=pltpu.CompilerParams(dimension_semantics=("arbitrary",)),
        name="pack_table",
    )(tab, tab)
    return packed.reshape(HALF_EXPERTS * ROW_TILES, LANES)


def _expert_row(tab_ref, off, shv):
    word = tab_ref[pl.ds(pl.multiple_of(off, SUBLANES), SUBLANES), :]
    return pltpu.bitcast(jnp.left_shift(word, shv) & jnp.uint32(0xFFFF0000), F32)


def _rows_to_lanes(row):
    return jnp.transpose(jnp.broadcast_to(row, (PEER_PAIRS, LANES)))


def _shift_rows(shf_row):
    ri = lax.broadcasted_iota(I32, (PEER_PAIRS, PEER_PAIRS), 0)
    ci = lax.broadcasted_iota(I32, (PEER_PAIRS, PEER_PAIRS), 1)
    diag = jnp.where(ri == ci, shf_row, 0.0).astype(BF16)
    rep = jnp.dot(diag, jnp.ones((PEER_PAIRS, LANES), BF16), preferred_element_type=F32)
    return pltpu.bitcast(rep, U32) >> 26


def _fold8(prods):
    sub = lax.broadcasted_iota(I32, (SUBLANES, LANES), 0)
    cur = prods
    for sh in (1, 2, 4):
        keep = (sub & sh) == 0
        nxt = []
        for k in range(0, len(cur), 2):
            a = jnp.where(keep, cur[k], cur[k + 1])
            b = jnp.where(keep, cur[k + 1], cur[k])
            nxt.append(a + pltpu.roll(b, sh, axis=0))
        cur = nxt
    return cur[0]


def _peer_act_kernel(*refs):
    off_refs = refs[:OFF_STREAMS]
    shf_ref, x_ref, gate_ref, tab_ref, w_ref, shb_ref, a_ref = refs[OFF_STREAMS:]

    def prep(t, slot):
        shb_ref[slot] = _shift_rows(shf_ref[pl.ds(jnp.minimum(t, PEER_TB - 1), 1), :])

    pair_slot = (lax.broadcasted_iota(I32, (SUBLANES, LANES), 1)
                 - lax.broadcasted_iota(I32, (SUBLANES, LANES), 0))

    def pairs(t, slot):
        xt = x_ref[t]
        t0 = _offset_index(t)
        spread = jnp.zeros((SUBLANES, LANES), F32)
        for g in range(PEER_PAIRS // SUBLANES):
            prods = []
            for jj in range(g * SUBLANES, (g + 1) * SUBLANES):
                off = off_refs[jj % OFF_STREAMS][t0 + (jj // OFF_STREAMS) * TOPK_TOKENS]
                f = _expert_row(tab_ref, off, shb_ref[slot, jj:jj + 1, :])
                prods.append(f * xt)
            sums = jnp.sum(_fold8(prods), axis=1, keepdims=True)
            spread = jnp.where(pair_slot == g * SUBLANES, sums, spread)
        a_ref[pl.ds(t, 1), :] = jnp.sum(spread, axis=0, keepdims=True)

    prep(0, 0)

    def tokens(i, _):
        for k in range(ACT_TOKENS_PER_TRIP):
            t = i * ACT_TOKENS_PER_TRIP + k
            prep(t + 1, (k + 1) % 2)
            pairs(t, k % 2)
        return 0

    lax.fori_loop(0, PEER_TB // ACT_TOKENS_PER_TRIP, tokens, 0)
    a = a_ref[...]
    act = 0.5 * a * (1.0 + lax.erf(a * (2.0 ** -0.5)))
    w_ref[...] = gate_ref[...] * act


def _peer_specs(lp, seq_real):
    per_tile = TOPK_TOKENS // PEER_TB
    stream_len = (PEER_PAIRS // OFF_STREAMS) * TOPK_TOKENS
    offsets = [pl.BlockSpec((stream_len,), lambda i, k=k: ((i // per_tile) * OFF_STREAMS + k,),
                            memory_space=pltpu.SMEM) for k in range(OFF_STREAMS)]
    rows = lambda *tail: pl.BlockSpec(
        (pl.Element(PEER_TB),) + tuple(pl.Element(d) for d in tail),
        lambda i: (_real_tile_start(i, PEER_TB, lp, seq_real),) + (0,) * len(tail))
    return offsets, rows


def _offset_index(t):
    return (pl.program_id(0) % (TOPK_TOKENS // PEER_TB)) * PEER_TB + t


def _peer_act(off, shf, x3, gate, tab, lp, seq_real):
    n_real = shf.shape[0]
    tile = lambda: pl.BlockSpec((PEER_TB, PEER_PAIRS), lambda i: (i, 0))
    offsets, rows = _peer_specs(lp, seq_real)
    return pl.pallas_call(
        _peer_act_kernel,
        grid=(n_real // PEER_TB,),
        in_specs=offsets + [
            tile(),
            rows(ROW_TILES, LANES),
            tile(),
            pl.BlockSpec((HALF_EXPERTS * ROW_TILES, LANES), lambda i: (0, 0),
                         pipeline_mode=pl.Buffered(1)),
        ],
        out_specs=pl.BlockSpec((PEER_TB, PEER_PAIRS), lambda i: (i, 0)),
        out_shape=jax.ShapeDtypeStruct((n_real, PEER_PAIRS), F32),
        scratch_shapes=[pltpu.VMEM((2, PEER_PAIRS, LANES), U32),
                        pltpu.VMEM((PEER_TB, PEER_PAIRS), F32)],
        compiler_params=pltpu.CompilerParams(
            dimension_semantics=("arbitrary",), vmem_limit_bytes=VMEM_TABLE_LIMIT),
        name="peer_act",
    )(*off, shf, x3, gate, tab)


def _peer_out_kernel(*refs):
    off_refs = refs[:OFF_STREAMS]
    shf_ref, w_ref, h_ref, g_ref, tab_ref, o_ref, shb_ref, wb_ref, ffn_ref = refs[OFF_STREAMS:]
    n_acc = 4

    def prep(t, _):
        r0 = pl.multiple_of(t * PEER_PAIRS, PEER_PAIRS)
        shb_ref[pl.ds(r0, PEER_PAIRS), :] = _shift_rows(shf_ref[pl.ds(t, 1), :])
        wb_ref[pl.ds(r0, PEER_PAIRS), :] = _rows_to_lanes(w_ref[pl.ds(t, 1), :])
        return 0

    lax.fori_loop(0, PEER_TB, prep, 0, unroll=PREP_UNROLL)

    def token(t, _):
        j0 = pl.multiple_of(t * PEER_PAIRS, PEER_PAIRS)
        m0 = _offset_index(t)
        accs = [jnp.zeros((SUBLANES, LANES), F32) for _ in range(n_acc)]
        for jj in range(PEER_PAIRS):
            off = off_refs[jj % OFF_STREAMS][m0 + (jj // OFF_STREAMS) * TOPK_TOKENS]
            f = _expert_row(tab_ref, off, shb_ref[pl.ds(j0 + jj, 1), :])
            accs[jj % n_acc] = accs[jj % n_acc] + wb_ref[pl.ds(j0 + jj, 1), :] * f
        ffn_ref[t] = (accs[0] + accs[1]) + (accs[2] + accs[3])
        return 0

    lax.fori_loop(0, PEER_TB, token, 0, unroll=4)
    ffn = jnp.concatenate([ffn_ref[:, k, :] for k in range(ROW_TILES)], axis=1)
    y = h_ref[...] + ffn
    ms = jnp.mean(y * y, axis=-1, keepdims=True)
    o_ref[...] = y * lax.rsqrt(ms + EPS) * g_ref[...]


def _peer_out(off, shf, w, h2d, g, tab, lp, seq_real):
    n_real = w.shape[0]
    offsets, rows = _peer_specs(lp, seq_real)
    return pl.pallas_call(
        _peer_out_kernel,
        grid=(n_real // PEER_TB,),
        in_specs=offsets + [
            pl.BlockSpec((PEER_TB, PEER_PAIRS), lambda i: (i, 0)),
            pl.BlockSpec((PEER_TB, PEER_PAIRS), lambda i: (i, 0)),
            rows(D_MODEL),
            pl.BlockSpec((1, D_MODEL), lambda i: (0, 0)),
            pl.BlockSpec((HALF_EXPERTS * ROW_TILES, LANES), lambda i: (0, 0),
                         pipeline_mode=pl.Buffered(1)),
        ],
        out_specs=pl.BlockSpec((PEER_TB, D_MODEL), lambda i: (i, 0)),
        out_shape=jax.ShapeDtypeStruct((n_real, D_MODEL), F32),
        scratch_shapes=[pltpu.VMEM((PEER_TB * PEER_PAIRS, LANES), U32),
                        pltpu.VMEM((PEER_TB * PEER_PAIRS, LANES), F32),
                        pltpu.VMEM((PEER_TB, ROW_TILES, LANES), F32)],
        compiler_params=pltpu.CompilerParams(
            dimension_semantics=("arbitrary",), vmem_limit_bytes=VMEM_TABLE_LIMIT),
        name="peer_out",
    )(*off, shf, w, h2d, g, tab)


def kernel(x, meta_tokens, norm1_g, w_in, lambda_q1, lambda_k1, lambda_q2, lambda_k2,
           attn_subln_g, conv_w, conv_b, conv_norm_g, conv_norm_b, w_out, norm2_g,
           peer_wq, peer_subkeys, peer_u, peer_v, final_norm_g):
    b, s, _ = x.shape
    seq = N_META + s
    lp = ((seq + Q_BLOCK - 1) // Q_BLOCK) * Q_BLOCK
    t = b * lp
    assert lp % TQ == 0 and lp % CONV_ROWS == 0 and t % TM == 0
    assert s % TOPK_TOKENS == 0 and TOPK_TOKENS % PEER_TB == 0

    meta = jnp.broadcast_to(meta_tokens[None].astype(x.dtype), (b, N_META, D_MODEL))
    h = jnp.concatenate([meta, x, jnp.zeros((b, lp - seq, D_MODEL), x.dtype)], axis=1)
    h2d = h.reshape(t, D_MODEL)

    lam_init = 0.8 - 0.6 * math.exp(-0.3 * 0)
    slopes = jnp.asarray([2.0 ** (-8.0 * (i + 1) / N_HEADS) for i in range(N_HEADS)], F32)
    group = jnp.arange(CONV_WIDTH) // CONV_GROUP
    gavg = ((group[:, None] == group[None, :]).astype(F32) * (1.0 / CONV_GROUP)).astype(BF16)

    q, k, v, u = _inproj(h2d, norm1_g[0][None], w_in[0].astype(BF16))
    attn = _attention(q.reshape(b, lp, -1), k.reshape(b, lp, -1), v.reshape(b, lp, -1), slopes,
                      lambda_q1[0][None], lambda_k1[0][None], lambda_q2[0][None],
                      lambda_k2[0][None], attn_subln_g[0][None], lam_init)
    conv = _conformer_conv(u.reshape(b, lp, -1), conv_w[0], conv_b[0][None], gavg,
                           conv_norm_g[0][None], conv_norm_b[0][None])
    h1, xn2, qp = _outproj(attn.reshape(t, -1), conv.reshape(t, -1), h2d,
                           w_out[0].astype(BF16), norm2_g[0][None], peer_wq[0].astype(BF16))
    keys = peer_subkeys[0].reshape(2 * PEER_HEADS, PEER_KEYS, PEER_KEYS).astype(BF16)
    off, shf, gate = _peer_topk(qp, keys, lp, s)
    off = [off.reshape(-1)] * OFF_STREAMS
    w = _peer_act(off, shf, xn2, gate, _pack_table(peer_u[0]), lp, s)
    out = _peer_out(off, shf, w, h1, final_norm_g[None], _pack_table(peer_v[0]), lp, s)
    return out.reshape(b, s, D_MODEL)
```

```python
import functools
import math

import jax
import jax.numpy as jnp
from jax import lax
from jax.experimental import pallas as pl
from jax.experimental.pallas import tpu as pltpu

F32 = jnp.float32
BF16 = jnp.bfloat16
I32 = jnp.int32
U32 = jnp.uint32

D_MODEL = 1024
N_META = 16
Q_BLOCK = 128
ATTN_WIDTH = 512
CONV_WIDTH = 512
N_HEADS = 4
HEAD_DIM = 64
V_DIM = 128
CONV_K = 31
CONV_GROUP = 64
PEER_HEADS = 8
PEER_KEYS = 128
PEER_TOPK = 16
PEER_PAIRS = PEER_HEADS * PEER_TOPK
N_EXPERTS = PEER_KEYS * PEER_KEYS
HALF_EXPERTS = N_EXPERTS // 2
EPS = 1e-6
NEG = -1e30
LOG2E = math.log2(math.e)

LANES = 128
SUBLANES = 8
ROW_TILES = D_MODEL // LANES

TM = 512
TQ = 384
CONV_ROWS = 128
CONV_PAD = 32
TOPK_TOKENS = 128
TOPK_HEADS_PER_TRIP = 8
PEER_TB = 128
ACT_TOKENS_PER_TRIP = 32
OFF_STREAMS = 8
PREP_UNROLL = 8
VMEM_TABLE_LIMIT = 52 * 1024 * 1024


def _inproj_kernel(h_ref, g_ref, w_ref, q_ref, k_ref, v_ref, u_ref):
    x = h_ref[...]
    ms = jnp.mean(x * x, axis=-1, keepdims=True)
    xn = (x * lax.rsqrt(ms + EPS) * g_ref[...]).astype(BF16)
    proj = jnp.dot(xn, w_ref[...], preferred_element_type=F32)
    q_ref[...] = (proj[:, 0:ATTN_WIDTH] * (HEAD_DIM ** -0.5 * LOG2E)).astype(BF16)
    k_ref[...] = proj[:, ATTN_WIDTH:2 * ATTN_WIDTH].astype(BF16)
    v_ref[...] = proj[:, 2 * ATTN_WIDTH:3 * ATTN_WIDTH].astype(BF16)
    ga = proj[:, 3 * ATTN_WIDTH:3 * ATTN_WIDTH + CONV_WIDTH]
    gg = proj[:, 3 * ATTN_WIDTH + CONV_WIDTH:]
    u_ref[...] = ga * jax.nn.sigmoid(gg)


def _inproj(h2d, g, w_bf16):
    t = h2d.shape[0]
    n_cols = w_bf16.shape[1]
    return pl.pallas_call(
        _inproj_kernel,
        grid=(t // TM,),
        in_specs=[
            pl.BlockSpec((TM, D_MODEL), lambda i: (i, 0)),
            pl.BlockSpec((1, D_MODEL), lambda i: (0, 0)),
            pl.BlockSpec((D_MODEL, n_cols), lambda i: (0, 0)),
        ],
        out_specs=[
            pl.BlockSpec((TM, ATTN_WIDTH), lambda i: (i, 0)),
            pl.BlockSpec((TM, ATTN_WIDTH), lambda i: (i, 0)),
            pl.BlockSpec((TM, ATTN_WIDTH), lambda i: (i, 0)),
            pl.BlockSpec((TM, CONV_WIDTH), lambda i: (i, 0)),
        ],
        out_shape=[
            jax.ShapeDtypeStruct((t, ATTN_WIDTH), BF16),
            jax.ShapeDtypeStruct((t, ATTN_WIDTH), BF16),
            jax.ShapeDtypeStruct((t, ATTN_WIDTH), BF16),
            jax.ShapeDtypeStruct((t, CONV_WIDTH), F32),
        ],
        compiler_params=pltpu.CompilerParams(
            dimension_semantics=("arbitrary",), vmem_limit_bytes=48 * 1024 * 1024),
        name="inproj",
    )(h2d, g, w_bf16)


def _attn_kernel(slopes_ref, lq1_ref, lk1_ref, lq2_ref, lk2_ref, subg_ref,
                 q_ref, k_ref, v_ref, o_ref, *, lam_init):
    hd = pl.program_id(1)
    qi = pl.program_id(2)
    slope = slopes_ref[hd]
    lam = (jnp.exp(jnp.sum(lq1_ref[...] * lk1_ref[...], keepdims=True))
           - jnp.exp(jnp.sum(lq2_ref[...] * lk2_ref[...], keepdims=True)) + lam_init)

    q = q_ref[0]
    lane = lax.broadcasted_iota(I32, q.shape, 1)
    zero = jnp.zeros_like(q)
    qs = jnp.concatenate([jnp.where(lane < HEAD_DIM, q, zero),
                          jnp.where(lane >= HEAD_DIM, q, zero)], axis=0)

    q0 = qi * TQ
    col = lax.broadcasted_iota(I32, (1, TQ), 1)

    def step(j, carry, masked):
        m, l, acc = carry
        k0 = pl.multiple_of(j * TQ, TQ)
        kj = k_ref[0, pl.ds(k0, TQ), :]
        vj = v_ref[0, pl.ds(k0, TQ), :]
        s = lax.dot_general(qs, kj, (((1,), (1,)), ((), ())), preferred_element_type=F32)
        s = s + (slope * LOG2E) * (col + (k0 - q0)).astype(F32)
        if masked:
            row = lax.broadcasted_iota(I32, (2 * TQ, TQ), 0)
            row = jnp.where(row >= TQ, row - TQ, row)
            cc = lax.broadcasted_iota(I32, (2 * TQ, TQ), 1)
            s = jnp.where(cc <= row, s, NEG)
        m_new = jnp.maximum(m, jnp.max(s, axis=1, keepdims=True))
        alpha = jnp.exp2(m - m_new)
        p = jnp.exp2(s - m_new)
        l = alpha * l + jnp.sum(p, axis=1, keepdims=True)
        acc = alpha * acc + jnp.dot(p.astype(BF16), vj, preferred_element_type=F32)
        return m_new, l, acc

    init = (jnp.full((2 * TQ, 1), NEG, F32), jnp.zeros((2 * TQ, 1), F32),
            jnp.zeros((2 * TQ, V_DIM), F32))
    carry = lax.fori_loop(0, qi, lambda j, c: step(j, c, False), init)
    m, l, acc = step(qi, carry, True)
    o = acc / l
    a = o[:TQ] - lam * o[TQ:]
    ms = jnp.mean(a * a, axis=-1, keepdims=True)
    y = a * lax.rsqrt(ms + EPS) * subg_ref[...] * (1.0 - lam_init)
    o_ref[0] = y.astype(BF16)


def _attention(q, k, v, slopes, lq1, lk1, lq2, lk2, subg, lam_init):
    b, lp, _ = q.shape
    nq = lp // TQ
    vec = lambda n: pl.BlockSpec((1, n), lambda bi, hi, i: (0, 0))
    return pl.pallas_call(
        functools.partial(_attn_kernel, lam_init=lam_init),
        grid=(b, N_HEADS, nq),
        in_specs=[
            pl.BlockSpec(memory_space=pltpu.SMEM),
            vec(HEAD_DIM), vec(HEAD_DIM), vec(HEAD_DIM), vec(HEAD_DIM), vec(V_DIM),
            pl.BlockSpec((1, TQ, V_DIM), lambda bi, hi, i: (bi, i, hi)),
            pl.BlockSpec((1, lp, V_DIM), lambda bi, hi, i: (bi, 0, hi)),
            pl.BlockSpec((1, lp, V_DIM), lambda bi, hi, i: (bi, 0, hi)),
        ],
        out_specs=pl.BlockSpec((1, TQ, V_DIM), lambda bi, hi, i: (bi, i, hi)),
        out_shape=jax.ShapeDtypeStruct((b, lp, ATTN_WIDTH), BF16),
        compiler_params=pltpu.CompilerParams(
            dimension_semantics=("arbitrary", "arbitrary", "arbitrary"),
            vmem_limit_bytes=48 * 1024 * 1024),
        name="diff_attn",
    )(slopes, lq1, lk1, lq2, lk2, subg, q, k, v)


def _group_mean(v, gavg_bf16):
    hi = v.astype(BF16)
    lo = (v - hi.astype(F32)).astype(BF16)
    return (jnp.dot(hi, gavg_bf16, preferred_element_type=F32)
            + jnp.dot(lo, gavg_bf16, preferred_element_type=F32))


def _conv_kernel(u_ref, w_ref, b_ref, gavg_ref, g_ref, beta_ref, o_ref, upad_ref, y_ref):
    lp = u_ref.shape[1]
    upad_ref[0:CONV_PAD, :] = jnp.zeros((CONV_PAD, CONV_WIDTH), F32)
    upad_ref[CONV_PAD:, :] = u_ref[0]

    def chunk(c, _):
        base = pl.multiple_of(c * CONV_ROWS, CONV_ROWS)
        for lb in range(CONV_WIDTH // LANES):
            ls = slice(lb * LANES, (lb + 1) * LANES)
            acc = jnp.zeros((CONV_ROWS, LANES), F32) + b_ref[:, ls]
            win = upad_ref[pl.ds(base, CONV_ROWS + CONV_PAD), ls]
            shifted = [win] + [jnp.roll(win, -rho, axis=0) for rho in range(1, SUBLANES)]
            for t in range(CONV_K):
                off = CONV_PAD - (CONV_K - 1) + t
                rho = off % SUBLANES
                acc = acc + w_ref[t:t + 1, ls] * shifted[rho][off - rho:off - rho + CONV_ROWS, :]
            y_ref[:, ls] = acc
        y = y_ref[...]
        mu = _group_mean(y, gavg_ref[...])
        d = y - mu
        var = _group_mean(d * d, gavg_ref[...])
        yn = d * lax.rsqrt(var + EPS) * g_ref[...] + beta_ref[...]
        o_ref[0, pl.ds(base, CONV_ROWS), :] = (yn * jax.nn.sigmoid(yn)).astype(BF16)
        return 0

    lax.fori_loop(0, lp // CONV_ROWS, chunk, 0)


def _conformer_conv(u, conv_w, conv_b, gavg, gn_g, gn_b):
    b, lp, c = u.shape
    full = lambda shape: pl.BlockSpec(shape, lambda bi: (0,) * len(shape))
    return pl.pallas_call(
        _conv_kernel,
        grid=(b,),
        in_specs=[
            pl.BlockSpec((1, lp, c), lambda bi: (bi, 0, 0)),
            full((CONV_K, c)), full((1, c)), full((c, c)), full((1, c)), full((1, c)),
        ],
        out_specs=pl.BlockSpec((1, lp, c), lambda bi: (bi, 0, 0)),
        out_shape=jax.ShapeDtypeStruct((b, lp, c), BF16),
        scratch_shapes=[pltpu.VMEM((lp + CONV_PAD, c), F32), pltpu.VMEM((CONV_ROWS, c), F32)],
        compiler_params=pltpu.CompilerParams(
            dimension_semantics=("arbitrary",), vmem_limit_bytes=56 * 1024 * 1024),
        name="conformer_conv",
    )(u, conv_w, conv_b, gavg, gn_g, gn_b)


def _store_row_tiles(ref, val):
    for k in range(ROW_TILES):
        ref[:, k, :] = val[:, k * LANES:(k + 1) * LANES]


def _outproj_kernel(a_ref, c_ref, h_ref, wo_ref, g_ref, wq_ref, h1_ref, xn_ref, qp_ref):
    mix = (jnp.dot(a_ref[...], wo_ref[0:ATTN_WIDTH, :], preferred_element_type=F32)
           + jnp.dot(c_ref[...], wo_ref[ATTN_WIDTH:, :], preferred_element_type=F32))
    h1 = h_ref[...] + mix
    h1_ref[...] = h1
    ms = jnp.mean(h1 * h1, axis=-1, keepdims=True)
    xn = h1 * lax.rsqrt(ms + EPS) * g_ref[...]
    _store_row_tiles(xn_ref, xn)
    qp = jnp.dot(xn.astype(BF16), wq_ref[...], preferred_element_type=F32)
    for hp in range(2 * PEER_HEADS):
        qp_ref[hp] = qp[:, hp * PEER_KEYS:(hp + 1) * PEER_KEYS].astype(BF16)


def _outproj(attn2d, conv2d, h2d, wo_bf16, g2, wq_bf16):
    t = h2d.shape[0]
    nq = wq_bf16.shape[1]
    return pl.pallas_call(
        _outproj_kernel,
        grid=(t // TM,),
        in_specs=[
            pl.BlockSpec((TM, ATTN_WIDTH), lambda i: (i, 0)),
            pl.BlockSpec((TM, CONV_WIDTH), lambda i: (i, 0)),
            pl.BlockSpec((TM, D_MODEL), lambda i: (i, 0)),
            pl.BlockSpec((D_MODEL, D_MODEL), lambda i: (0, 0)),
            pl.BlockSpec((1, D_MODEL), lambda i: (0, 0)),
            pl.BlockSpec((D_MODEL, nq), lambda i: (0, 0)),
        ],
        out_specs=[
            pl.BlockSpec((TM, D_MODEL), lambda i: (i, 0)),
            pl.BlockSpec((TM, ROW_TILES, LANES), lambda i: (i, 0, 0)),
            pl.BlockSpec((2 * PEER_HEADS, TM, PEER_KEYS), lambda i: (0, i, 0)),
        ],
        out_shape=[
            jax.ShapeDtypeStruct((t, D_MODEL), F32),
            jax.ShapeDtypeStruct((t, ROW_TILES, LANES), F32),
            jax.ShapeDtypeStruct((2 * PEER_HEADS, t, PEER_KEYS), BF16),
        ],
        compiler_params=pltpu.CompilerParams(
            dimension_semantics=("arbitrary",), vmem_limit_bytes=48 * 1024 * 1024),
        name="outproj_peerq",
    )(attn2d, conv2d, h2d, wo_bf16, g2, wq_bf16)


def _merge_network(n):
    pairs = []

    def merge(lo, hi, r):
        step = r * 2
        if step < hi - lo:
            merge(lo, hi, step)
            merge(lo + r, hi, step)
            pairs.extend((i, i + r) for i in range(lo + r, hi - r, step))
        else:
            pairs.append((lo, lo + r))

    def sort(lo, hi):
        if hi - lo >= 1:
            mid = lo + (hi - lo) // 2
            sort(lo, mid)
            sort(mid + 1, hi)
            merge(lo, hi, 1)

    sort(0, n - 1)
    return tuple(pairs)


SORT16 = _merge_network(PEER_KEYS // SUBLANES)


def _sort_lists(vals, ids):
    v, d = list(vals), list(ids)
    for i, j in SORT16:
        swap = (v[j] > v[i]) | ((v[j] == v[i]) & (d[j] < d[i]))
        v[i], v[j] = jnp.where(swap, v[j], v[i]), jnp.where(swap, v[i], v[j])
        d[i], d[j] = jnp.where(swap, d[j], d[i]), jnp.where(swap, d[i], d[j])
    return v, d


def _row_writer(val_ref, idx_ref):
    def emit(r, val, idx):
        val_ref[r:r + 1, :] = val
        idx_ref[r:r + 1, :] = idx
    return emit


def _top_keys(st, emit):
    tokens = st.shape[1]
    sub = lax.broadcasted_iota(I32, (SUBLANES, tokens), 0)
    n = st.shape[0] // SUBLANES
    v, d = _sort_lists([st[k * SUBLANES:(k + 1) * SUBLANES, :] for k in range(n)],
                       [sub + k * SUBLANES for k in range(n)])
    for r in range(PEER_TOPK):
        m = jnp.max(v[0], axis=0, keepdims=True)
        imin = jnp.min(jnp.where(v[0] == m, d[0], st.shape[0]), axis=0, keepdims=True)
        emit(r, m, imin)
        sel = d[0] == imin
        for k in range(PEER_TOPK - 1 - r):
            v[k] = jnp.where(sel, v[k + 1], v[k])
            d[k] = jnp.where(sel, d[k + 1], d[k])


def _top_sums(s1, s2, emit):
    tokens = s1.shape[1]
    sub = lax.broadcasted_iota(I32, (SUBLANES, tokens), 0)
    big = PEER_TOPK * PEER_TOPK
    v = [s1[0:SUBLANES, :] + s2[b:b + 1, :] for b in range(PEER_TOPK)]
    ptr = jnp.zeros((SUBLANES, tokens), I32)
    e = s1[SUBLANES:, :] + s2[0:1, :]
    eid = (sub + SUBLANES) * PEER_TOPK
    for r in range(PEER_TOPK):
        hid = sub * PEER_TOPK + ptr
        m = jnp.max(jnp.maximum(v[0], e), axis=0, keepdims=True)
        cand = jnp.minimum(jnp.where(v[0] == m, hid, big), jnp.where(e == m, eid, big))
        imin = jnp.min(cand, axis=0, keepdims=True)
        emit(r, m, imin)
        sel = hid == imin
        e = jnp.where(eid == imin, NEG, e)
        for k in range(PEER_TOPK - 1 - r):
            v[k] = jnp.where(sel, v[k + 1], v[k])
        ptr = jnp.where(sel, ptr + 1, ptr)


def _topk_kernel(qp_ref, keys_ref, r_ref, sh_ref, gate_ref,
                 s1_ref, i1_ref, s2_ref, i2_ref, ct_ref, ci_ref, e_ref, g_ref):
    def head(h, slot):
        s1_s, i1_s, s2_s, i2_s, ct_s, ci_s = (
            ref.at[slot] for ref in (s1_ref, i1_ref, s2_ref, i2_ref, ct_ref, ci_ref))
        for p, (sv, si) in enumerate(((s1_s, i1_s), (s2_s, i2_s))):
            hp = 2 * h + p
            st = lax.dot_general(keys_ref[hp], qp_ref[hp], (((1,), (1,)), ((), ())),
                                 preferred_element_type=F32)
            _top_keys(st, _row_writer(sv, si))
        _top_sums(s1_s[...], s2_s[...], _row_writer(ct_s, ci_s))
        ct = ct_s[...]
        ci = ci_s[...]
        hi = ci >> 4
        lo = ci & (PEER_TOPK - 1)
        i1 = i1_s[...]
        i2 = i2_s[...]
        e1 = jnp.zeros_like(ci)
        e2 = jnp.zeros_like(ci)
        for a in range(PEER_TOPK):
            e1 = jnp.where(hi == a, i1[a:a + 1, :], e1)
            e2 = jnp.where(lo == a, i2[a:a + 1, :], e2)
        e = e1 * PEER_KEYS + e2
        ex = jnp.exp(ct - jnp.max(ct, axis=0, keepdims=True))
        gate = ex / jnp.sum(ex, axis=0, keepdims=True)
        row0 = pl.multiple_of(h * PEER_TOPK, PEER_TOPK)
        e_ref[pl.ds(row0, PEER_TOPK), :] = e
        g_ref[pl.ds(row0, PEER_TOPK), :] = gate

    def heads(g, _):
        for slot in range(TOPK_HEADS_PER_TRIP):
            head(g * TOPK_HEADS_PER_TRIP + slot, slot)
        return 0

    lax.fori_loop(0, PEER_HEADS // TOPK_HEADS_PER_TRIP, heads, 0)
    e = e_ref[...]
    sh_ref[...] = ((e >> 13) << 4).astype(F32).T
    gate_ref[...] = g_ref[...].T
    e_ref[...] = (e & (HALF_EXPERTS - 1)) * ROW_TILES
    for k in range(OFF_STREAMS):
        r_ref[0, k] = e_ref[pl.ds(k, PEER_PAIRS // OFF_STREAMS, stride=OFF_STREAMS), :]


def _real_tile_start(i, tile, lp, seq_real):
    per_batch = seq_real // tile
    return pl.multiple_of((i // per_batch) * lp + N_META + (i % per_batch) * tile, N_META)


def _peer_topk(qp, keys_bf16, lp, seq_real):
    n_real = (qp.shape[1] // lp) * seq_real
    tt = TOPK_TOKENS
    sc = lambda dt: pltpu.VMEM((TOPK_HEADS_PER_TRIP, PEER_TOPK, tt), dt)
    qp_block = tuple(pl.Element(d) for d in (2 * PEER_HEADS, tt, PEER_KEYS))
    return pl.pallas_call(
        _topk_kernel,
        grid=(n_real // tt,),
        in_specs=[
            pl.BlockSpec(qp_block, lambda i: (0, _real_tile_start(i, tt, lp, seq_real), 0)),
            pl.BlockSpec((2 * PEER_HEADS, PEER_KEYS, PEER_KEYS), lambda i: (0, 0, 0)),
        ],
        out_specs=[
            pl.BlockSpec((1, OFF_STREAMS, PEER_PAIRS // OFF_STREAMS, tt), lambda i: (i, 0, 0, 0)),
            pl.BlockSpec((tt, PEER_PAIRS), lambda i: (i, 0)),
            pl.BlockSpec((tt, PEER_PAIRS), lambda i: (i, 0)),
        ],
        out_shape=[
            jax.ShapeDtypeStruct((n_real // tt, OFF_STREAMS, PEER_PAIRS // OFF_STREAMS, tt), I32),
            jax.ShapeDtypeStruct((n_real, PEER_PAIRS), F32),
            jax.ShapeDtypeStruct((n_real, PEER_PAIRS), F32),
        ],
        scratch_shapes=[sc(F32), sc(I32), sc(F32), sc(I32), sc(F32), sc(I32),
                        pltpu.VMEM((PEER_PAIRS, tt), I32), pltpu.VMEM((PEER_PAIRS, tt), F32)],
        compiler_params=pltpu.CompilerParams(dimension_semantics=("arbitrary",)),
        name="peer_topk",
    )(qp, keys_bf16)


def _pack_kernel(hi_ref, lo_ref, o_ref):
    hi = pltpu.bitcast(hi_ref[...].astype(BF16).astype(F32), U32)
    lo = pltpu.bitcast(lo_ref[...].astype(BF16).astype(F32), U32)
    _store_row_tiles(o_ref, hi | (lo >> 16))


def _pack_table(tab):
    rows = 512
    nb = HALF_EXPERTS // rows
    packed = pl.pallas_call(
        _pack_kernel,
        grid=(nb,),
        in_specs=[pl.BlockSpec((rows, D_MODEL), lambda i: (i, 0)),
                  pl.BlockSpec((rows, D_MODEL), lambda i: (i + nb, 0))],
        out_specs=pl.BlockSpec((rows, ROW_TILES, LANES), lambda i: (i, 0, 0)),
        out_shape=jax.ShapeDtypeStruct((HALF_EXPERTS, ROW_TILES, LANES), U32),
        compiler_params=pltpu.CompilerParams(dimension_semantics=("arbitrary",)),
        name="pack_table",
    )(tab, tab)
    return packed.reshape(HALF_EXPERTS * ROW_TILES, LANES)


def _expert_row(tab_ref, off, shv):
    word = tab_ref[pl.ds(pl.multiple_of(off, SUBLANES), SUBLANES), :]
    return pltpu.bitcast(jnp.left_shift(word, shv) & jnp.uint32(0xFFFF0000), F32)


def _rows_to_lanes(row):
    return jnp.transpose(jnp.broadcast_to(row, (PEER_PAIRS, LANES)))


def _shift_rows(shf_row):
    rep = _rows_to_lanes(shf_row)
    return pltpu.bitcast(rep, U32) >> 26


def _fold8(prods):
    sub = lax.broadcasted_iota(I32, (SUBLANES, LANES), 0)
    cur = prods
    for sh in (1, 2, 4):
        keep = (sub & sh) == 0
        nxt = []
        for k in range(0, len(cur), 2):
            a = jnp.where(keep, cur[k], cur[k + 1])
            b = jnp.where(keep, cur[k + 1], cur[k])
            nxt.append(a + pltpu.roll(b, sh, axis=0))
        cur = nxt
    return cur[0]


def _peer_act_kernel(*refs):
    off_refs = refs[:OFF_STREAMS]
    shf_ref, x_ref, gate_ref, tab_ref, w_ref, shb_ref, a_ref = refs[OFF_STREAMS:]

    def prep(t, slot):
        shb_ref[slot] = _shift_rows(shf_ref[pl.ds(jnp.minimum(t, PEER_TB - 1), 1), :])

    pair_slot = (lax.broadcasted_iota(I32, (SUBLANES, LANES), 1)
                 - lax.broadcasted_iota(I32, (SUBLANES, LANES), 0))

    def pairs(t, slot):
        xt = x_ref[t]
        t0 = _offset_index(t)
        spread = jnp.zeros((SUBLANES, LANES), F32)
        for g in range(PEER_PAIRS // SUBLANES):
            prods = []
            for jj in range(g * SUBLANES, (g + 1) * SUBLANES):
                off = off_refs[jj % OFF_STREAMS][t0 + (jj // OFF_STREAMS) * TOPK_TOKENS]
                f = _expert_row(tab_ref, off, shb_ref[slot, jj:jj + 1, :])
                prods.append(f * xt)
            sums = jnp.sum(_fold8(prods), axis=1, keepdims=True)
            spread = jnp.where(pair_slot == g * SUBLANES, sums, spread)
        a_ref[pl.ds(t, 1), :] = jnp.sum(spread, axis=0, keepdims=True)

    prep(0, 0)

    def tokens(i, _):
        for k in range(ACT_TOKENS_PER_TRIP):
            t = i * ACT_TOKENS_PER_TRIP + k
            prep(t + 1, (k + 1) % 2)
            pairs(t, k % 2)
        return 0

    lax.fori_loop(0, PEER_TB // ACT_TOKENS_PER_TRIP, tokens, 0)
    a = a_ref[...]
    act = 0.5 * a * (1.0 + lax.erf(a * (2.0 ** -0.5)))
    w_ref[...] = gate_ref[...] * act


def _peer_specs(lp, seq_real):
    per_tile = TOPK_TOKENS // PEER_TB
    stream_len = (PEER_PAIRS // OFF_STREAMS) * TOPK_TOKENS
    offsets = [pl.BlockSpec((stream_len,), lambda i, k=k: ((i // per_tile) * OFF_STREAMS + k,),
                            memory_space=pltpu.SMEM) for k in range(OFF_STREAMS)]
    rows = lambda *tail: pl.BlockSpec(
        (pl.Element(PEER_TB),) + tuple(pl.Element(d) for d in tail),
        lambda i: (_real_tile_start(i, PEER_TB, lp, seq_real),) + (0,) * len(tail))
    return offsets, rows


def _offset_index(t):
    return (pl.program_id(0) % (TOPK_TOKENS // PEER_TB)) * PEER_TB + t


def _peer_act(off, shf, x3, gate, tab, lp, seq_real):
    n_real = shf.shape[0]
    tile = lambda: pl.BlockSpec((PEER_TB, PEER_PAIRS), lambda i: (i, 0))
    offsets, rows = _peer_specs(lp, seq_real)
    return pl.pallas_call(
        _peer_act_kernel,
        grid=(n_real // PEER_TB,),
        in_specs=offsets + [
            tile(),
            rows(ROW_TILES, LANES),
            tile(),
            pl.BlockSpec((HALF_EXPERTS * ROW_TILES, LANES), lambda i: (0, 0),
                         pipeline_mode=pl.Buffered(1)),
        ],
        out_specs=pl.BlockSpec((PEER_TB, PEER_PAIRS), lambda i: (i, 0)),
        out_shape=jax.ShapeDtypeStruct((n_real, PEER_PAIRS), F32),
        scratch_shapes=[pltpu.VMEM((2, PEER_PAIRS, LANES), U32),
                        pltpu.VMEM((PEER_TB, PEER_PAIRS), F32)],
        compiler_params=pltpu.CompilerParams(
            dimension_semantics=("arbitrary",), vmem_limit_bytes=VMEM_TABLE_LIMIT),
        name="peer_act",
    )(*off, shf, x3, gate, tab)


def _peer_out_kernel(*refs):
    off_refs = refs[:OFF_STREAMS]
    shf_ref, w_ref, h_ref, g_ref, tab_ref, o_ref, shb_ref, wb_ref, ffn_ref = refs[OFF_STREAMS:]
    n_acc = 4

    def prep(t, _):
        r0 = pl.multiple_of(t * PEER_PAIRS, PEER_PAIRS)
        shb_ref[pl.ds(r0, PEER_PAIRS), :] = _shift_rows(shf_ref[pl.ds(t, 1), :])
        wb_ref[pl.ds(r0, PEER_PAIRS), :] = _rows_to_lanes(w_ref[pl.ds(t, 1), :])
        return 0

    lax.fori_loop(0, PEER_TB, prep, 0, unroll=PREP_UNROLL)

    def token(t, _):
        j0 = pl.multiple_of(t * PEER_PAIRS, PEER_PAIRS)
        m0 = _offset_index(t)
        accs = [jnp.zeros((SUBLANES, LANES), F32) for _ in range(n_acc)]
        for jj in range(PEER_PAIRS):
            off = off_refs[jj % OFF_STREAMS][m0 + (jj // OFF_STREAMS) * TOPK_TOKENS]
            f = _expert_row(tab_ref, off, shb_ref[pl.ds(j0 + jj, 1), :])
            accs[jj % n_acc] = accs[jj % n_acc] + wb_ref[pl.ds(j0 + jj, 1), :] * f
        ffn_ref[t] = (accs[0] + accs[1]) + (accs[2] + accs[3])
        return 0

    lax.fori_loop(0, PEER_TB, token, 0, unroll=4)
    ffn = jnp.concatenate([ffn_ref[:, k, :] for k in range(ROW_TILES)], axis=1)
    y = h_ref[...] + ffn
    ms = jnp.mean(y * y, axis=-1, keepdims=True)
    o_ref[...] = y * lax.rsqrt(ms + EPS) * g_ref[...]


def _peer_out(off, shf, w, h2d, g, tab, lp, seq_real):
    n_real = w.shape[0]
    offsets, rows = _peer_specs(lp, seq_real)
    return pl.pallas_call(
        _peer_out_kernel,
        grid=(n_real // PEER_TB,),
        in_specs=offsets + [
            pl.BlockSpec((PEER_TB, PEER_PAIRS), lambda i: (i, 0)),
            pl.BlockSpec((PEER_TB, PEER_PAIRS), lambda i: (i, 0)),
            rows(D_MODEL),
            pl.BlockSpec((1, D_MODEL), lambda i: (0, 0)),
            pl.BlockSpec((HALF_EXPERTS * ROW_TILES, LANES), lambda i: (0, 0),
                         pipeline_mode=pl.Buffered(1)),
        ],
        out_specs=pl.BlockSpec((PEER_TB, D_MODEL), lambda i: (i, 0)),
        out_shape=jax.ShapeDtypeStruct((n_real, D_MODEL), F32),
        scratch_shapes=[pltpu.VMEM((PEER_TB * PEER_PAIRS, LANES), U32),
                        pltpu.VMEM((PEER_TB * PEER_PAIRS, LANES), F32),
                        pltpu.VMEM((PEER_TB, ROW_TILES, LANES), F32)],
        compiler_params=pltpu.CompilerParams(
            dimension_semantics=("arbitrary",), vmem_limit_bytes=VMEM_TABLE_LIMIT),
        name="peer_out",
    )(*off, shf, w, h2d, g, tab)


def kernel(x, meta_tokens, norm1_g, w_in, lambda_q1, lambda_k1, lambda_q2, lambda_k2,
           attn_subln_g, conv_w, conv_b, conv_norm_g, conv_norm_b, w_out, norm2_g,
           peer_wq, peer_subkeys, peer_u, peer_v, final_norm_g):
    b, s, _ = x.shape
    seq = N_META + s
    lp = ((seq + Q_BLOCK - 1) // Q_BLOCK) * Q_BLOCK
    t = b * lp
    assert lp % TQ == 0 and lp % CONV_ROWS == 0 and t % TM == 0
    assert s % TOPK_TOKENS == 0 and TOPK_TOKENS % PEER_TB == 0

    meta = jnp.broadcast_to(meta_tokens[None].astype(x.dtype), (b, N_META, D_MODEL))
    h = jnp.concatenate([meta, x, jnp.zeros((b, lp - seq, D_MODEL), x.dtype)], axis=1)
    h2d = h.reshape(t, D_MODEL)

    lam_init = 0.8 - 0.6 * math.exp(-0.3 * 0)
    slopes = jnp.asarray([2.0 ** (-8.0 * (i + 1) / N_HEADS) for i in range(N_HEADS)], F32)
    group = jnp.arange(CONV_WIDTH) // CONV_GROUP
    gavg = ((group[:, None] == group[None, :]).astype(F32) * (1.0 / CONV_GROUP)).astype(BF16)

    q, k, v, u = _inproj(h2d, norm1_g[0][None], w_in[0].astype(BF16))
    attn = _attention(q.reshape(b, lp, -1), k.reshape(b, lp, -1), v.reshape(b, lp, -1), slopes,
                      lambda_q1[0][None], lambda_k1[0][None], lambda_q2[0][None],
                      lambda_k2[0][None], attn_subln_g[0][None], lam_init)
    conv = _conformer_conv(u.reshape(b, lp, -1), conv_w[0], conv_b[0][None], gavg,
                           conv_norm_g[0][None], conv_norm_b[0][None])
    h1, xn2, qp = _outproj(attn.reshape(t, -1), conv.reshape(t, -1), h2d,
                           w_out[0].astype(BF16), norm2_g[0][None], peer_wq[0].astype(BF16))
    keys = peer_subkeys[0].reshape(2 * PEER_HEADS, PEER_KEYS, PEER_KEYS).astype(BF16)
    off, shf, gate = _peer_topk(qp, keys, lp, s)
    off = [off.reshape(-1)] * OFF_STREAMS
    w = _peer_act(off, shf, xn2, gate, _pack_table(peer_u[0]), lp, s)
    out = _peer_out(off, shf, w, h1, final_norm_g[None], _pack_table(peer_v[0]), lp, s)
    return out.reshape(b, s, D_MODEL)
```

```python
import functools
import math

import jax
import jax.numpy as jnp
from jax import lax
from jax.experimental import pallas as pl
from jax.experimental.pallas import tpu as pltpu

F32 = jnp.float32
BF16 = jnp.bfloat16
I32 = jnp.int32
U32 = jnp.uint32

D_MODEL = 1024
N_META = 16
Q_BLOCK = 128
ATTN_WIDTH = 512
CONV_WIDTH = 512
N_HEADS = 4
HEAD_DIM = 64
V_DIM = 128
CONV_K = 31
CONV_GROUP = 64
PEER_HEADS = 8
PEER_KEYS = 128
PEER_TOPK = 16
PEER_PAIRS = PEER_HEADS * PEER_TOPK
N_EXPERTS = PEER_KEYS * PEER_KEYS
HALF_EXPERTS = N_EXPERTS // 2
EPS = 1e-6
NEG = -1e30
LOG2E = math.log2(math.e)

LANES = 128
SUBLANES = 8
ROW_TILES = D_MODEL // LANES

TM = 512
TQ = 384
CONV_ROWS = 128
CONV_PAD = 32
TOPK_TOKENS = 128
TOPK_HEADS_PER_TRIP = 8
PEER_TB = 128
ACT_TOKENS_PER_TRIP = 32
OFF_STREAMS = 8
PREP_UNROLL = 8
VMEM_TABLE_LIMIT = 52 * 1024 * 1024


def _inproj_kernel(h_ref, g_ref, w_ref, q_ref, k_ref, v_ref, u_ref):
    x = h_ref[...]
    ms = jnp.mean(x * x, axis=-1, keepdims=True)
    xn = (x * lax.rsqrt(ms + EPS) * g_ref[...]).astype(BF16)
    proj = jnp.dot(xn, w_ref[...], preferred_element_type=F32)
    q_ref[...] = (proj[:, 0:ATTN_WIDTH] * (HEAD_DIM ** -0.5 * LOG2E)).astype(BF16)
    k_ref[...] = proj[:, ATTN_WIDTH:2 * ATTN_WIDTH].astype(BF16)
    v_ref[...] = proj[:, 2 * ATTN_WIDTH:3 * ATTN_WIDTH].astype(BF16)
    ga = proj[:, 3 * ATTN_WIDTH:3 * ATTN_WIDTH + CONV_WIDTH]
    gg = proj[:, 3 * ATTN_WIDTH + CONV_WIDTH:]
    u_ref[...] = ga * jax.nn.sigmoid(gg)


def _inproj(h2d, g, w_bf16):
    t = h2d.shape[0]
    n_cols = w_bf16.shape[1]
    return pl.pallas_call(
        _inproj_kernel,
        grid=(t // TM,),
        in_specs=[
            pl.BlockSpec((TM, D_MODEL), lambda i: (i, 0)),
            pl.BlockSpec((1, D_MODEL), lambda i: (0, 0)),
            pl.BlockSpec((D_MODEL, n_cols), lambda i: (0, 0)),
        ],
        out_specs=[
            pl.BlockSpec((TM, ATTN_WIDTH), lambda i: (i, 0)),
            pl.BlockSpec((TM, ATTN_WIDTH), lambda i: (i, 0)),
            pl.BlockSpec((TM, ATTN_WIDTH), lambda i: (i, 0)),
            pl.BlockSpec((TM, CONV_WIDTH), lambda i: (i, 0)),
        ],
        out_shape=[
            jax.ShapeDtypeStruct((t, ATTN_WIDTH), BF16),
            jax.ShapeDtypeStruct((t, ATTN_WIDTH), BF16),
            jax.ShapeDtypeStruct((t, ATTN_WIDTH), BF16),
            jax.ShapeDtypeStruct((t, CONV_WIDTH), F32),
        ],
        compiler_params=pltpu.CompilerParams(
            dimension_semantics=("arbitrary",), vmem_limit_bytes=48 * 1024 * 1024),
        name="inproj",
    )(h2d, g, w_bf16)


def _attn_kernel(slopes_ref, lq1_ref, lk1_ref, lq2_ref, lk2_ref, subg_ref,
                 q_ref, k_ref, v_ref, o_ref, *, lam_init):
    hd = pl.program_id(1)
    qi = pl.program_id(2)
    slope = slopes_ref[hd]
    lam = (jnp.exp(jnp.sum(lq1_ref[...] * lk1_ref[...], keepdims=True))
           - jnp.exp(jnp.sum(lq2_ref[...] * lk2_ref[...], keepdims=True)) + lam_init)

    q = q_ref[0]
    lane = lax.broadcasted_iota(I32, q.shape, 1)
    zero = jnp.zeros_like(q)
    qs = jnp.concatenate([jnp.where(lane < HEAD_DIM, q, zero),
                          jnp.where(lane >= HEAD_DIM, q, zero)], axis=0)

    q0 = qi * TQ
    col = lax.broadcasted_iota(I32, (1, TQ), 1)

    def step(j, carry, masked):
        m, l, acc = carry
        k0 = pl.multiple_of(j * TQ, TQ)
        kj = k_ref[0, pl.ds(k0, TQ), :]
        vj = v_ref[0, pl.ds(k0, TQ), :]
        s = lax.dot_general(qs, kj, (((1,), (1,)), ((), ())), preferred_element_type=F32)
        s = s + (slope * LOG2E) * (col + (k0 - q0)).astype(F32)
        if masked:
            row = lax.broadcasted_iota(I32, (2 * TQ, TQ), 0)
            row = jnp.where(row >= TQ, row - TQ, row)
            cc = lax.broadcasted_iota(I32, (2 * TQ, TQ), 1)
            s = jnp.where(cc <= row, s, NEG)
        m_new = jnp.maximum(m, jnp.max(s, axis=1, keepdims=True))
        alpha = jnp.exp2(m - m_new)
        p = jnp.exp2(s - m_new)
        l = alpha * l + jnp.sum(p, axis=1, keepdims=True)
        acc = alpha * acc + jnp.dot(p.astype(BF16), vj, preferred_element_type=F32)
        return m_new, l, acc

    init = (jnp.full((2 * TQ, 1), NEG, F32), jnp.zeros((2 * TQ, 1), F32),
            jnp.zeros((2 * TQ, V_DIM), F32))
    carry = lax.fori_loop(0, qi, lambda j, c: step(j, c, False), init)
    m, l, acc = step(qi, carry, True)
    o = acc / l
    a = o[:TQ] - lam * o[TQ:]
    ms = jnp.mean(a * a, axis=-1, keepdims=True)
    y = a * lax.rsqrt(ms + EPS) * subg_ref[...] * (1.0 - lam_init)
    o_ref[0] = y.astype(BF16)


def _attention(q, k, v, slopes, lq1, lk1, lq2, lk2, subg, lam_init):
    b, lp, _ = q.shape
    nq = lp // TQ
    vec = lambda n: pl.BlockSpec((1, n), lambda bi, hi, i: (0, 0))
    return pl.pallas_call(
        functools.partial(_attn_kernel, lam_init=lam_init),
        grid=(b, N_HEADS, nq),
        in_specs=[
            pl.BlockSpec(memory_space=pltpu.SMEM),
            vec(HEAD_DIM), vec(HEAD_DIM), vec(HEAD_DIM), vec(HEAD_DIM), vec(V_DIM),
            pl.BlockSpec((1, TQ, V_DIM), lambda bi, hi, i: (bi, i, hi)),
            pl.BlockSpec((1, lp, V_DIM), lambda bi, hi, i: (bi, 0, hi)),
            pl.BlockSpec((1, lp, V_DIM), lambda bi, hi, i: (bi, 0, hi)),
        ],
        out_specs=pl.BlockSpec((1, TQ, V_DIM), lambda bi, hi, i: (bi, i, hi)),
        out_shape=jax.ShapeDtypeStruct((b, lp, ATTN_WIDTH), BF16),
        compiler_params=pltpu.CompilerParams(
            dimension_semantics=("arbitrary", "arbitrary", "arbitrary"),
            vmem_limit_bytes=48 * 1024 * 1024),
        name="diff_attn",
    )(slopes, lq1, lk1, lq2, lk2, subg, q, k, v)


def _group_mean(v, gavg_bf16):
    hi = v.astype(BF16)
    lo = (v - hi.astype(F32)).astype(BF16)
    return (jnp.dot(hi, gavg_bf16, preferred_element_type=F32)
            + jnp.dot(lo, gavg_bf16, preferred_element_type=F32))


def _conv_kernel(u_ref, w_ref, b_ref, gavg_ref, g_ref, beta_ref, o_ref, upad_ref, y_ref):
    lp = u_ref.shape[1]
    upad_ref[0:CONV_PAD, :] = jnp.zeros((CONV_PAD, CONV_WIDTH), F32)
    upad_ref[CONV_PAD:, :] = u_ref[0]

    def chunk(c, _):
        base = pl.multiple_of(c * CONV_ROWS, CONV_ROWS)
        for lb in range(CONV_WIDTH // LANES):
            ls = slice(lb * LANES, (lb + 1) * LANES)
            acc = jnp.zeros((CONV_ROWS, LANES), F32) + b_ref[:, ls]
            win = upad_ref[pl.ds(base, CONV_ROWS + CONV_PAD), ls]
            shifted = [win] + [jnp.roll(win, -rho, axis=0) for rho in range(1, SUBLANES)]
            for t in range(CONV_K):
                off = CONV_PAD - (CONV_K - 1) + t
                rho = off % SUBLANES
                acc = acc + w_ref[t:t + 1, ls] * shifted[rho][off - rho:off - rho + CONV_ROWS, :]
            y_ref[:, ls] = acc
        y = y_ref[...]
        mu = _group_mean(y, gavg_ref[...])
        d = y - mu
        var = _group_mean(d * d, gavg_ref[...])
        yn = d * lax.rsqrt(var + EPS) * g_ref[...] + beta_ref[...]
        o_ref[0, pl.ds(base, CONV_ROWS), :] = (yn * jax.nn.sigmoid(yn)).astype(BF16)
        return 0

    lax.fori_loop(0, lp // CONV_ROWS, chunk, 0)


def _conformer_conv(u, conv_w, conv_b, gavg, gn_g, gn_b):
    b, lp, c = u.shape
    full = lambda shape: pl.BlockSpec(shape, lambda bi: (0,) * len(shape))
    return pl.pallas_call(
        _conv_kernel,
        grid=(b,),
        in_specs=[
            pl.BlockSpec((1, lp, c), lambda bi: (bi, 0, 0)),
            full((CONV_K, c)), full((1, c)), full((c, c)), full((1, c)), full((1, c)),
        ],
        out_specs=pl.BlockSpec((1, lp, c), lambda bi: (bi, 0, 0)),
        out_shape=jax.ShapeDtypeStruct((b, lp, c), BF16),
        scratch_shapes=[pltpu.VMEM((lp + CONV_PAD, c), F32), pltpu.VMEM((CONV_ROWS, c), F32)],
        compiler_params=pltpu.CompilerParams(
            dimension_semantics=("arbitrary",), vmem_limit_bytes=56 * 1024 * 1024),
        name="conformer_conv",
    )(u, conv_w, conv_b, gavg, gn_g, gn_b)


def _store_row_tiles(ref, val):
    for k in range(ROW_TILES):
        ref[:, k, :] = val[:, k * LANES:(k + 1) * LANES]


def _outproj_kernel(a_ref, c_ref, h_ref, wo_ref, g_ref, wq_ref, h1_ref, xn_ref, qp_ref):
    mix = (jnp.dot(a_ref[...], wo_ref[0:ATTN_WIDTH, :], preferred_element_type=F32)
           + jnp.dot(c_ref[...], wo_ref[ATTN_WIDTH:, :], preferred_element_type=F32))
    h1 = h_ref[...] + mix
    h1_ref[...] = h1
    ms = jnp.mean(h1 * h1, axis=-1, keepdims=True)
    xn = h1 * lax.rsqrt(ms + EPS) * g_ref[...]
    _store_row_tiles(xn_ref, xn)
    qp = jnp.dot(xn.astype(BF16), wq_ref[...], preferred_element_type=F32)
    for hp in range(2 * PEER_HEADS):
        qp_ref[hp] = qp[:, hp * PEER_KEYS:(hp + 1) * PEER_KEYS].astype(BF16)


def _outproj(attn2d, conv2d, h2d, wo_bf16, g2, wq_bf16):
    t = h2d.shape[0]
    nq = wq_bf16.shape[1]
    return pl.pallas_call(
        _outproj_kernel,
        grid=(t // TM,),
        in_specs=[
            pl.BlockSpec((TM, ATTN_WIDTH), lambda i: (i, 0)),
            pl.BlockSpec((TM, CONV_WIDTH), lambda i: (i, 0)),
            pl.BlockSpec((TM, D_MODEL), lambda i: (i, 0)),
            pl.BlockSpec((D_MODEL, D_MODEL), lambda i: (0, 0)),
            pl.BlockSpec((1, D_MODEL), lambda i: (0, 0)),
            pl.BlockSpec((D_MODEL, nq), lambda i: (0, 0)),
        ],
        out_specs=[
            pl.BlockSpec((TM, D_MODEL), lambda i: (i, 0)),
            pl.BlockSpec((TM, ROW_TILES, LANES), lambda i: (i, 0, 0)),
            pl.BlockSpec((2 * PEER_HEADS, TM, PEER_KEYS), lambda i: (0, i, 0)),
        ],
        out_shape=[
            jax.ShapeDtypeStruct((t, D_MODEL), F32),
            jax.ShapeDtypeStruct((t, ROW_TILES, LANES), F32),
            jax.ShapeDtypeStruct((2 * PEER_HEADS, t, PEER_KEYS), BF16),
        ],
        compiler_params=pltpu.CompilerParams(
            dimension_semantics=("arbitrary",), vmem_limit_bytes=48 * 1024 * 1024),
        name="outproj_peerq",
    )(attn2d, conv2d, h2d, wo_bf16, g2, wq_bf16)


def _merge_network(n):
    pairs = []

    def merge(lo, hi, r):
        step = r * 2
        if step < hi - lo:
            merge(lo, hi, step)
            merge(lo + r, hi, step)
            pairs.extend((i, i + r) for i in range(lo + r, hi - r, step))
        else:
            pairs.append((lo, lo + r))

    def sort(lo, hi):
        if hi - lo >= 1:
            mid = lo + (hi - lo) // 2
            sort(lo, mid)
            sort(mid + 1, hi)
            merge(lo, hi, 1)

    sort(0, n - 1)
    return tuple(pairs)


SORT16 = _merge_network(PEER_KEYS // SUBLANES)


def _sort_lists(vals, ids):
    v, d = list(vals), list(ids)
    for i, j in SORT16:
        swap = (v[j] > v[i]) | ((v[j] == v[i]) & (d[j] < d[i]))
        v[i], v[j] = jnp.where(swap, v[j], v[i]), jnp.where(swap, v[i], v[j])
        d[i], d[j] = jnp.where(swap, d[j], d[i]), jnp.where(swap, d[i], d[j])
    return v, d


def _row_writer(val_ref, idx_ref):
    def emit(r, val, idx):
        val_ref[r:r + 1, :] = val
        idx_ref[r:r + 1, :] = idx
    return emit


def _top_keys(st, emit):
    tokens = st.shape[1]
    sub = lax.broadcasted_iota(I32, (SUBLANES, tokens), 0)
    n = st.shape[0] // SUBLANES
    v, d = _sort_lists([st[k * SUBLANES:(k + 1) * SUBLANES, :] for k in range(n)],
                       [sub + k * SUBLANES for k in range(n)])
    for r in range(PEER_TOPK):
        m = jnp.max(v[0], axis=0, keepdims=True)
        imin = jnp.min(jnp.where(v[0] == m, d[0], st.shape[0]), axis=0, keepdims=True)
        emit(r, m, imin)
        sel = d[0] == imin
        for k in range(PEER_TOPK - 1 - r):
            v[k] = jnp.where(sel, v[k + 1], v[k])
            d[k] = jnp.where(sel, d[k + 1], d[k])


def _top_sums(s1, s2, emit):
    tokens = s1.shape[1]
    sub = lax.broadcasted_iota(I32, (SUBLANES, tokens), 0)
    big = PEER_TOPK * PEER_TOPK
    v = [s1[0:SUBLANES, :] + s2[b:b + 1, :] for b in range(PEER_TOPK)]
    ptr = jnp.zeros((SUBLANES, tokens), I32)
    e = s1[SUBLANES:, :] + s2[0:1, :]
    eid = (sub + SUBLANES) * PEER_TOPK
    for r in range(PEER_TOPK):
        hid = sub * PEER_TOPK + ptr
        m = jnp.max(jnp.maximum(v[0], e), axis=0, keepdims=True)
        cand = jnp.minimum(jnp.where(v[0] == m, hid, big), jnp.where(e == m, eid, big))
        imin = jnp.min(cand, axis=0, keepdims=True)
        emit(r, m, imin)
        sel = hid == imin
        e = jnp.where(eid == imin, NEG, e)
        for k in range(PEER_TOPK - 1 - r):
            v[k] = jnp.where(sel, v[k + 1], v[k])
        ptr = jnp.where(sel, ptr + 1, ptr)


def _topk_kernel(qp_ref, keys_ref, r_ref, sh_ref, gate_ref,
                 s1_ref, i1_ref, s2_ref, i2_ref, ct_ref, ci_ref, e_ref, g_ref):
    def head(h, slot):
        s1_s, i1_s, s2_s, i2_s, ct_s, ci_s = (
            ref.at[slot] for ref in (s1_ref, i1_ref, s2_ref, i2_ref, ct_ref, ci_ref))
        for p, (sv, si) in enumerate(((s1_s, i1_s), (s2_s, i2_s))):
            hp = 2 * h + p
            st = lax.dot_general(keys_ref[hp], qp_ref[hp], (((1,), (1,)), ((), ())),
                                 preferred_element_type=F32)
            _top_keys(st, _row_writer(sv, si))
        _top_sums(s1_s[...], s2_s[...], _row_writer(ct_s, ci_s))
        ct = ct_s[...]
        ci = ci_s[...]
        hi = ci >> 4
        lo = ci & (PEER_TOPK - 1)
        i1 = i1_s[...]
        i2 = i2_s[...]
        e1 = jnp.zeros_like(ci)
        e2 = jnp.zeros_like(ci)
        for a in range(PEER_TOPK):
            e1 = jnp.where(hi == a, i1[a:a + 1, :], e1)
            e2 = jnp.where(lo == a, i2[a:a + 1, :], e2)
        e = e1 * PEER_KEYS + e2
        ex = jnp.exp(ct - jnp.max(ct, axis=0, keepdims=True))
        gate = ex / jnp.sum(ex, axis=0, keepdims=True)
        row0 = pl.multiple_of(h * PEER_TOPK, PEER_TOPK)
        e_ref[pl.ds(row0, PEER_TOPK), :] = e
        g_ref[pl.ds(row0, PEER_TOPK), :] = gate

    def heads(g, _):
        for slot in range(TOPK_HEADS_PER_TRIP):
            head(g * TOPK_HEADS_PER_TRIP + slot, slot)
        return 0

    lax.fori_loop(0, PEER_HEADS // TOPK_HEADS_PER_TRIP, heads, 0)
    e = e_ref[...]
    sh_ref[...] = ((e >> 13) << 4).astype(F32).T
    gate_ref[...] = g_ref[...].T
    e_ref[...] = (e & (HALF_EXPERTS - 1)) * ROW_TILES
    for k in range(OFF_STREAMS):
        r_ref[0, k] = e_ref[pl.ds(k, PEER_PAIRS // OFF_STREAMS, stride=OFF_STREAMS), :]


def _real_tile_start(i, tile, lp, seq_real):
    per_batch = seq_real // tile
    return pl.multiple_of((i // per_batch) * lp + N_META + (i % per_batch) * tile, N_META)


def _peer_topk(qp, keys_bf16, lp, seq_real):
    n_real = (qp.shape[1] // lp) * seq_real
    tt = TOPK_TOKENS
    sc = lambda dt: pltpu.VMEM((TOPK_HEADS_PER_TRIP, PEER_TOPK, tt), dt)
    qp_block = tuple(pl.Element(d) for d in (2 * PEER_HEADS, tt, PEER_KEYS))
    return pl.pallas_call(
        _topk_kernel,
        grid=(n_real // tt,),
        in_specs=[
            pl.BlockSpec(qp_block, lambda i: (0, _real_tile_start(i, tt, lp, seq_real), 0)),
            pl.BlockSpec((2 * PEER_HEADS, PEER_KEYS, PEER_KEYS), lambda i: (0, 0, 0)),
        ],
        out_specs=[
            pl.BlockSpec((1, OFF_STREAMS, PEER_PAIRS // OFF_STREAMS, tt), lambda i: (i, 0, 0, 0)),
            pl.BlockSpec((tt, PEER_PAIRS), lambda i: (i, 0)),
            pl.BlockSpec((tt, PEER_PAIRS), lambda i: (i, 0)),
        ],
        out_shape=[
            jax.ShapeDtypeStruct((n_real // tt, OFF_STREAMS, PEER_PAIRS // OFF_STREAMS, tt), I32),
            jax.ShapeDtypeStruct((n_real, PEER_PAIRS), F32),
            jax.ShapeDtypeStruct((n_real, PEER_PAIRS), F32),
        ],
        scratch_shapes=[sc(F32), sc(I32), sc(F32), sc(I32), sc(F32), sc(I32),
                        pltpu.VMEM((PEER_PAIRS, tt), I32), pltpu.VMEM((PEER_PAIRS, tt), F32)],
        compiler_params=pltpu.CompilerParams(dimension_semantics=("arbitrary",)),
        name="peer_topk",
    )(qp, keys_bf16)


def _pack_kernel(hi_ref, lo_ref, o_ref):
    hi = pltpu.bitcast(hi_ref[...].astype(BF16).astype(F32), U32)
    lo = pltpu.bitcast(lo_ref[...].astype(BF16).astype(F32), U32)
    _store_row_tiles(o_ref, hi | (lo >> 16))


def _pack_table(tab):
    rows = 512
    nb = HALF_EXPERTS // rows
    packed = pl.pallas_call(
        _pack_kernel,
        grid=(nb,),
        in_specs=[pl.BlockSpec((rows, D_MODEL), lambda i: (i, 0)),
                  pl.BlockSpec((rows, D_MODEL), lambda i: (i + nb, 0))],
        out_specs=pl.BlockSpec((rows, ROW_TILES, LANES), lambda i: (i, 0, 0)),
        out_shape=jax.ShapeDtypeStruct((HALF_EXPERTS, ROW_TILES, LANES), U32),
        compiler_params=pltpu.CompilerParams(dimension_semantics=("arbitrary",)),
        name="pack_table",
    )(tab, tab)
    return packed.reshape(HALF_EXPERTS * ROW_TILES, LANES)


def _expert_row(tab_ref, off, shv):
    word = tab_ref[pl.ds(pl.multiple_of(off, SUBLANES), SUBLANES), :]
    return pltpu.bitcast(jnp.left_shift(word, shv) & jnp.uint32(0xFFFF0000), F32)


def _rows_to_lanes(row):
    return jnp.transpose(jnp.broadcast_to(row, (PEER_PAIRS, LANES)))


def _shift_rows(shf_row):
    rep = _rows_to_lanes(shf_row)
    return pltpu.bitcast(rep, U32) >> 26


def _shift_rows_mxu(shf_row):
    ri = lax.broadcasted_iota(I32, (PEER_PAIRS, PEER_PAIRS), 0)
    ci = lax.broadcasted_iota(I32, (PEER_PAIRS, PEER_PAIRS), 1)
    diag = jnp.where(ri == ci, shf_row, 0.0).astype(BF16)
    rep = jnp.dot(diag, jnp.ones((PEER_PAIRS, LANES), BF16), preferred_element_type=F32)
    return pltpu.bitcast(rep, U32) >> 26


def _fold8(prods):
    sub = lax.broadcasted_iota(I32, (SUBLANES, LANES), 0)
    cur = prods
    for sh in (1, 2, 4):
        keep = (sub & sh) == 0
        nxt = []
        for k in range(0, len(cur), 2):
            a = jnp.where(keep, cur[k], cur[k + 1])
            b = jnp.where(keep, cur[k + 1], cur[k])
            nxt.append(a + pltpu.roll(b, sh, axis=0))
        cur = nxt
    return cur[0]


def _peer_act_kernel(*refs):
    off_refs = refs[:OFF_STREAMS]
    shf_ref, x_ref, gate_ref, tab_ref, w_ref, shb_ref, a_ref = refs[OFF_STREAMS:]

    def prep(t, slot):
        shb_ref[slot] = _shift_rows(shf_ref[pl.ds(jnp.minimum(t, PEER_TB - 1), 1), :])

    pair_slot = (lax.broadcasted_iota(I32, (SUBLANES, LANES), 1)
                 - lax.broadcasted_iota(I32, (SUBLANES, LANES), 0))

    def pairs(t, slot):
        xt = x_ref[t]
        t0 = _offset_index(t)
        spread = jnp.zeros((SUBLANES, LANES), F32)
        for g in range(PEER_PAIRS // SUBLANES):
            prods = []
            for jj in range(g * SUBLANES, (g + 1) * SUBLANES):
                off = off_refs[jj % OFF_STREAMS][t0 + (jj // OFF_STREAMS) * TOPK_TOKENS]
                f = _expert_row(tab_ref, off, shb_ref[slot, jj:jj + 1, :])
                prods.append(f * xt)
            sums = jnp.sum(_fold8(prods), axis=1, keepdims=True)
            spread = jnp.where(pair_slot == g * SUBLANES, sums, spread)
        a_ref[pl.ds(t, 1), :] = jnp.sum(spread, axis=0, keepdims=True)

    prep(0, 0)

    def tokens(i, _):
        for k in range(ACT_TOKENS_PER_TRIP):
            t = i * ACT_TOKENS_PER_TRIP + k
            prep(t + 1, (k + 1) % 2)
            pairs(t, k % 2)
        return 0

    lax.fori_loop(0, PEER_TB // ACT_TOKENS_PER_TRIP, tokens, 0)
    a = a_ref[...]
    act = 0.5 * a * (1.0 + lax.erf(a * (2.0 ** -0.5)))
    w_ref[...] = gate_ref[...] * act


def _peer_specs(lp, seq_real):
    per_tile = TOPK_TOKENS // PEER_TB
    stream_len = (PEER_PAIRS // OFF_STREAMS) * TOPK_TOKENS
    offsets = [pl.BlockSpec((stream_len,), lambda i, k=k: ((i // per_tile) * OFF_STREAMS + k,),
                            memory_space=pltpu.SMEM) for k in range(OFF_STREAMS)]
    rows = lambda *tail: pl.BlockSpec(
        (pl.Element(PEER_TB),) + tuple(pl.Element(d) for d in tail),
        lambda i: (_real_tile_start(i, PEER_TB, lp, seq_real),) + (0,) * len(tail))
    return offsets, rows


def _offset_index(t):
    return (pl.program_id(0) % (TOPK_TOKENS // PEER_TB)) * PEER_TB + t


def _peer_act(off, shf, x3, gate, tab, lp, seq_real):
    n_real = shf.shape[0]
    tile = lambda: pl.BlockSpec((PEER_TB, PEER_PAIRS), lambda i: (i, 0))
    offsets, rows = _peer_specs(lp, seq_real)
    return pl.pallas_call(
        _peer_act_kernel,
        grid=(n_real // PEER_TB,),
        in_specs=offsets + [
            tile(),
            rows(ROW_TILES, LANES),
            tile(),
            pl.BlockSpec((HALF_EXPERTS * ROW_TILES, LANES), lambda i: (0, 0),
                         pipeline_mode=pl.Buffered(1)),
        ],
        out_specs=pl.BlockSpec((PEER_TB, PEER_PAIRS), lambda i: (i, 0)),
        out_shape=jax.ShapeDtypeStruct((n_real, PEER_PAIRS), F32),
        scratch_shapes=[pltpu.VMEM((2, PEER_PAIRS, LANES), U32),
                        pltpu.VMEM((PEER_TB, PEER_PAIRS), F32)],
        compiler_params=pltpu.CompilerParams(
            dimension_semantics=("arbitrary",), vmem_limit_bytes=VMEM_TABLE_LIMIT),
        name="peer_act",
    )(*off, shf, x3, gate, tab)


def _peer_out_kernel(*refs):
    off_refs = refs[:OFF_STREAMS]
    shf_ref, w_ref, h_ref, g_ref, tab_ref, o_ref, shb_ref, wb_ref, ffn_ref = refs[OFF_STREAMS:]
    n_acc = 4

    def prep(t, _):
        r0 = pl.multiple_of(t * PEER_PAIRS, PEER_PAIRS)
        shb_ref[pl.ds(r0, PEER_PAIRS), :] = _shift_rows_mxu(shf_ref[pl.ds(t, 1), :])
        wb_ref[pl.ds(r0, PEER_PAIRS), :] = _rows_to_lanes(w_ref[pl.ds(t, 1), :])
        return 0

    lax.fori_loop(0, PEER_TB, prep, 0, unroll=PREP_UNROLL)

    def token(t, _):
        j0 = pl.multiple_of(t * PEER_PAIRS, PEER_PAIRS)
        m0 = _offset_index(t)
        accs = [jnp.zeros((SUBLANES, LANES), F32) for _ in range(n_acc)]
        for jj in range(PEER_PAIRS):
            off = off_refs[jj % OFF_STREAMS][m0 + (jj // OFF_STREAMS) * TOPK_TOKENS]
            f = _expert_row(tab_ref, off, shb_ref[pl.ds(j0 + jj, 1), :])
            accs[jj % n_acc] = accs[jj % n_acc] + wb_ref[pl.ds(j0 + jj, 1), :] * f
        ffn_ref[t] = (accs[0] + accs[1]) + (accs[2] + accs[3])
        return 0

    lax.fori_loop(0, PEER_TB, token, 0, unroll=4)
    ffn = jnp.concatenate([ffn_ref[:, k, :] for k in range(ROW_TILES)], axis=1)
    y = h_ref[...] + ffn
    ms = jnp.mean(y * y, axis=-1, keepdims=True)
    o_ref[...] = y * lax.rsqrt(ms + EPS) * g_ref[...]


def _peer_out(off, shf, w, h2d, g, tab, lp, seq_real):
    n_real = w.shape[0]
    offsets, rows = _peer_specs(lp, seq_real)
    return pl.pallas_call(
        _peer_out_kernel,
        grid=(n_real // PEER_TB,),
        in_specs=offsets + [
            pl.BlockSpec((PEER_TB, PEER_PAIRS), lambda i: (i, 0)),
            pl.BlockSpec((PEER_TB, PEER_PAIRS), lambda i: (i, 0)),
            rows(D_MODEL),
            pl.BlockSpec((1, D_MODEL), lambda i: (0, 0)),
            pl.BlockSpec((HALF_EXPERTS * ROW_TILES, LANES), lambda i: (0, 0),
                         pipeline_mode=pl.Buffered(1)),
        ],
        out_specs=pl.BlockSpec((PEER_TB, D_MODEL), lambda i: (i, 0)),
        out_shape=jax.ShapeDtypeStruct((n_real, D_MODEL), F32),
        scratch_shapes=[pltpu.VMEM((PEER_TB * PEER_PAIRS, LANES), U32),
                        pltpu.VMEM((PEER_TB * PEER_PAIRS, LANES), F32),
                        pltpu.VMEM((PEER_TB, ROW_TILES, LANES), F32)],
        compiler_params=pltpu.CompilerParams(
            dimension_semantics=("arbitrary",), vmem_limit_bytes=VMEM_TABLE_LIMIT),
        name="peer_out",
    )(*off, shf, w, h2d, g, tab)


def kernel(x, meta_tokens, norm1_g, w_in, lambda_q1, lambda_k1, lambda_q2, lambda_k2,
           attn_subln_g, conv_w, conv_b, conv_norm_g, conv_norm_b, w_out, norm2_g,
           peer_wq, peer_subkeys, peer_u, peer_v, final_norm_g):
    b, s, _ = x.shape
    seq = N_META + s
    lp = ((seq + Q_BLOCK - 1) // Q_BLOCK) * Q_BLOCK
    t = b * lp
    assert lp % TQ == 0 and lp % CONV_ROWS == 0 and t % TM == 0
    assert s % TOPK_TOKENS == 0 and TOPK_TOKENS % PEER_TB == 0

    meta = jnp.broadcast_to(meta_tokens[None].astype(x.dtype), (b, N_META, D_MODEL))
    h = jnp.concatenate([meta, x, jnp.zeros((b, lp - seq, D_MODEL), x.dtype)], axis=1)
    h2d = h.reshape(t, D_MODEL)

    lam_init = 0.8 - 0.6 * math.exp(-0.3 * 0)
    slopes = jnp.asarray([2.0 ** (-8.0 * (i + 1) / N_HEADS) for i in range(N_HEADS)], F32)
    group = jnp.arange(CONV_WIDTH) // CONV_GROUP
    gavg = ((group[:, None] == group[None, :]).astype(F32) * (1.0 / CONV_GROUP)).astype(BF16)

    q, k, v, u = _inproj(h2d, norm1_g[0][None], w_in[0].astype(BF16))
    attn = _attention(q.reshape(b, lp, -1), k.reshape(b, lp, -1), v.reshape(b, lp, -1), slopes,
                      lambda_q1[0][None], lambda_k1[0][None], lambda_q2[0][None],
                      lambda_k2[0][None], attn_subln_g[0][None], lam_init)
    conv = _conformer_conv(u.reshape(b, lp, -1), conv_w[0], conv_b[0][None], gavg,
                           conv_norm_g[0][None], conv_norm_b[0][None])
    h1, xn2, qp = _outproj(attn.reshape(t, -1), conv.reshape(t, -1), h2d,
                           w_out[0].astype(BF16), norm2_g[0][None], peer_wq[0].astype(BF16))
    keys = peer_subkeys[0].reshape(2 * PEER_HEADS, PEER_KEYS, PEER_KEYS).astype(BF16)
    off, shf, gate = _peer_topk(qp, keys, lp, s)
    off = [off.reshape(-1)] * OFF_STREAMS
    w = _peer_act(off, shf, xn2, gate, _pack_table(peer_u[0]), lp, s)
    out = _peer_out(off, shf, w, h1, final_norm_g[None], _pack_table(peer_v[0]), lp, s)
    return out.reshape(b, s, D_MODEL)
```
